```python
import jax, jax.numpy as jnp
from jax import lax
import numpy as np

D_MODEL = 1024
BATCH = 2
SEQ = 8192
DEPTH = 2

D_PLE = 256
N_MIXERS = 4
GROUP_WIDTH = D_MODEL // N_MIXERS
MIX_WIDTH = N_MIXERS * GROUP_WIDTH
HEAD_DIM = GROUP_WIDTH // 4
EPS = 1e-6
A_HEADS = GROUP_WIDTH // HEAD_DIM
A_KV_HEADS = 2
WINDOW = 128
ROPE_THETA = 10000.0
B_HEADS = GROUP_WIDTH // HEAD_DIM
B_CONV = 4
DN_CHUNK = 64
C_CHUNK = 128
C_GROUPS = 4
D_CONV = 31
D_GROUPS = 4
A_Q = A_HEADS * HEAD_DIM
A_KV = A_KV_HEADS * HEAD_DIM
B_QKV = 3 * GROUP_WIDTH
C_UV = 2 * GROUP_WIDTH
D_GLU = 2 * GROUP_WIDTH
COL_SIZES = (A_Q, A_KV, A_KV, B_QKV, B_HEADS, B_HEADS, C_UV, D_GLU, MIX_WIDTH)
IN_WIDTH = A_Q + 2 * A_KV + B_QKV + 2 * B_HEADS + C_UV + D_GLU + MIX_WIDTH

kernel_name = "hybrid_parallel_group_trunk"


def rms_norm(x, gain):
    xf = x.astype(jnp.float32)
    y = xf * lax.rsqrt(jnp.mean(xf * xf, -1, keepdims=True) + EPS)
    return (y * gain.astype(jnp.float32)).astype(x.dtype)


def group_layer_norm(x, gain, bias, groups):
    lead, c = x.shape[:-1], x.shape[-1]
    xf = x.astype(jnp.float32).reshape(*lead, groups, c // groups)
    mu = jnp.mean(xf, -1, keepdims=True)
    var = jnp.mean(jnp.square(xf - mu), -1, keepdims=True)
    y = ((xf - mu) * lax.rsqrt(var + EPS)).reshape(*lead, c)
    return (y * gain.astype(jnp.float32) + bias.astype(jnp.float32)).astype(x.dtype)


def l2_norm(x):
    return x * lax.rsqrt(jnp.sum(x * x, -1, keepdims=True) + EPS)


def causal_depthwise_conv(x, w):
    k, c = w.shape
    return lax.conv_general_dilated(x, w[:, None, :].astype(x.dtype), window_strides=(1,),
                                    padding=[(k - 1, 0)], dimension_numbers=('NWC', 'WIO', 'NWC'),
                                    feature_group_count=c)


def rope_tables(seq, dim):
    inv = ROPE_THETA ** (-jnp.arange(0, dim, 2, dtype=jnp.float32) / dim)
    ang = jnp.arange(seq, dtype=jnp.float32)[:, None] * inv[None, :]
    return jnp.cos(ang), jnp.sin(ang)


def apply_rope(x, cos, sin):
    x1, x2 = jnp.split(x.astype(jnp.float32), 2, -1)
    c, s = cos[None, :, None, :], sin[None, :, None, :]
    return jnp.concatenate([x1 * c - x2 * s, x2 * c + x1 * s], -1).astype(x.dtype)


def split_columns(z, sizes):
    out, start = [], 0
    for n in sizes:
        out.append(z[..., start:start + n])
        start += n
    return out


def sliding_window_attention(q, k, v, q_gain, k_gain, sink):
    b, s, _ = q.shape
    g = A_HEADS // A_KV_HEADS
    nb = s // WINDOW
    cos, sin = rope_tables(s, HEAD_DIM)
    q = apply_rope(rms_norm(q.reshape(b, s, A_HEADS, HEAD_DIM), q_gain), cos, sin)
    k = apply_rope(rms_norm(k.reshape(b, s, A_KV_HEADS, HEAD_DIM), k_gain), cos, sin)
    v = v.reshape(b, s, A_KV_HEADS, HEAD_DIM)
    qb = q.reshape(b, nb, WINDOW, A_KV_HEADS, g, HEAD_DIM)
    kb = k.reshape(b, nb, WINDOW, A_KV_HEADS, HEAD_DIM)
    vb = v.reshape(b, nb, WINDOW, A_KV_HEADS, HEAD_DIM)
    zk = jnp.zeros_like(kb[:, :1])
    k2 = jnp.concatenate([jnp.concatenate([zk, kb[:, :-1]], 1), kb], 2)
    v2 = jnp.concatenate([jnp.concatenate([zk, vb[:, :-1]], 1), vb], 2)
    sc = jnp.einsum('bnqhgd,bnkhd->bnhgqk', qb, k2,
                    preferred_element_type=jnp.float32) * (HEAD_DIM ** -0.5)
    qi = jnp.arange(WINDOW)[:, None]
    kj = jnp.arange(2 * WINDOW)[None, :]
    rel = qi + WINDOW - kj
    band = (rel >= 0) & (rel < WINDOW)
    blk = jnp.arange(nb)[:, None, None]
    mask = band[None] & ((blk > 0) | (kj >= WINDOW)[None])
    sc = jnp.where(mask[None, :, None, None], sc, -jnp.inf)
    sk = sink.astype(jnp.float32).reshape(A_KV_HEADS, g)[None, None, :, :, None, None]
    m = jnp.maximum(jnp.max(sc, -1, keepdims=True), sk)
    e = jnp.exp(sc - m)
    probs = e / (jnp.sum(e, -1, keepdims=True) + jnp.exp(sk - m))
    o = jnp.einsum('bnhgqk,bnkhd->bnqhgd', probs.astype(v.dtype), v2)
    return o.reshape(b, s, A_Q)


def gated_deltanet(qkv, beta_logit, a_logit, conv_w, a_log, dt_bias, out_gain):
    b, s, _ = qkv.shape
    h, d, c = B_HEADS, HEAD_DIM, DN_CHUNK
    nc = s // c
    f32 = jnp.float32
    qkv = jax.nn.silu(causal_depthwise_conv(qkv, conv_w)).astype(f32)
    q, k, v = jnp.split(qkv, 3, -1)

    def heads(t):
        return t.reshape(b, nc, c, h, d).transpose(0, 3, 1, 2, 4)

    def per_head(t):
        return t.reshape(b, nc, c, h).transpose(0, 3, 1, 2)

    q = l2_norm(heads(q)) * (d ** -0.5)
    k = l2_norm(heads(k))
    v = heads(v)
    beta = per_head(jax.nn.sigmoid(beta_logit.astype(f32)))
    gdec = -jnp.exp(a_log.astype(f32)) * jax.nn.softplus(a_logit.astype(f32) + dt_bias.astype(f32))
    gc = jnp.cumsum(per_head(gdec), -1)
    ci = jnp.arange(c)
    tril = ci[:, None] >= ci[None, :]
    strict = ci[:, None] > ci[None, :]
    decay = jnp.exp(jnp.where(tril, gc[..., :, None] - gc[..., None, :], -jnp.inf))
    kb = k * beta[..., None]
    amat = jnp.where(strict, jnp.einsum('bhncd,bhnsd->bhncs', kb, k) * decay, 0.0)
    rhs = jnp.concatenate([v * beta[..., None], kb * jnp.exp(gc)[..., None]], -1)
    sol = lax.linalg.triangular_solve(amat + jnp.eye(c, dtype=f32), rhs, left_side=True, lower=True)
    u, w = jnp.split(sol, 2, -1)
    qk = jnp.einsum('bhncd,bhnsd->bhncs', q, k) * decay
    qg = q * jnp.exp(gc)[..., None]
    kg = k * jnp.exp(gc[..., -1:] - gc)[..., None]
    cdec = jnp.exp(gc[..., -1])

    def step(state, xs):
        u_c, w_c, qg_c, qk_c, kg_c, dec_c = xs
        v_new = u_c - jnp.einsum('bhcd,bhde->bhce', w_c, state)
        o_c = jnp.einsum('bhcd,bhde->bhce', qg_c, state) + jnp.einsum('bhcs,bhse->bhce', qk_c, v_new)
        state = state * dec_c[..., None, None] + jnp.einsum('bhcd,bhce->bhde', kg_c, v_new)
        return state, o_c

    xs = (jnp.moveaxis(u, 2, 0), jnp.moveaxis(w, 2, 0), jnp.moveaxis(qg, 2, 0),
          jnp.moveaxis(qk, 2, 0), jnp.moveaxis(kg, 2, 0), jnp.moveaxis(cdec, 2, 0))
    _, o = lax.scan(step, jnp.zeros((b, h, d, d), f32), xs)
    o = o.transpose(1, 0, 3, 2, 4).reshape(b, s, h, d)
    return rms_norm(o, out_gain).reshape(b, s, h * d)


def chunked_spatial_gating(uv, ln_gain, ln_bias, w_s, b_s):
    b, s, _ = uv.shape
    nc = s // C_CHUNK
    u, v = jnp.split(jax.nn.gelu(uv), 2, -1)
    v = group_layer_norm(v, ln_gain, ln_bias, 1)
    width = v.shape[-1]
    v = v.reshape(b, nc, C_CHUNK, C_GROUPS, width // C_GROUPS)
    ci = jnp.arange(C_CHUNK)
    w = jnp.where((ci[:, None] >= ci[None, :])[None], w_s, 0.0).astype(v.dtype)
    z = jnp.einsum('gij,bnjgc->bnigc', w, v) + b_s.T.astype(v.dtype)[None, None, :, :, None]
    return u * z.reshape(b, s, width)


def conformer_conv(glu_in, conv_w, conv_b, ln_gain, ln_bias):
    a, g = jnp.split(glu_in, 2, -1)
    y = a * jax.nn.sigmoid(g)
    y = causal_depthwise_conv(y, conv_w) + conv_b.astype(y.dtype)
    y = group_layer_norm(y, ln_gain, ln_bias, D_GROUPS)
    return jax.nn.silu(y)


def setup_inputs(seed: int = 0) -> dict:
    key = jax.random.key(seed)
    ks = jax.random.split(key, 24)
    f32 = jnp.float32
    nrm = lambda k, shape, scale: jax.random.normal(k, shape, f32) * scale
    dt = jnp.exp(jax.random.uniform(ks[10], (DEPTH, B_HEADS), f32, np.log(1e-3), np.log(1e-1)))
    return {
        "x": nrm(ks[0], (BATCH, SEQ, D_MODEL), 1.0),
        "p": nrm(ks[1], (DEPTH, BATCH, SEQ, D_PLE), 1.0),
        "norm_gain": 1.0 + nrm(ks[2], (DEPTH, D_MODEL), 0.02),
        "w_in": nrm(ks[3], (DEPTH, D_MODEL, IN_WIDTH), D_MODEL ** -0.5),
        "w_out": nrm(ks[4], (DEPTH, MIX_WIDTH, D_MODEL), MIX_WIDTH ** -0.5),
        "a_q_gain": 1.0 + nrm(ks[5], (DEPTH, HEAD_DIM), 0.02),
        "a_k_gain": 1.0 + nrm(ks[6], (DEPTH, HEAD_DIM), 0.02),
        "a_sink": nrm(ks[7], (DEPTH, A_HEADS), 1.0),
        "b_conv": nrm(ks[8], (DEPTH, B_CONV, B_QKV), B_CONV ** -0.5),
        "b_a_log": jnp.log(jax.random.uniform(ks[9], (DEPTH, B_HEADS), f32, 1.0, 16.0)),
        "b_dt_bias": dt + jnp.log(-jnp.expm1(-dt)),
        "b_out_gain": 1.0 + nrm(ks[11], (DEPTH, HEAD_DIM), 0.02),
        "c_ln_gain": 1.0 + nrm(ks[12], (DEPTH, GROUP_WIDTH), 0.02),
        "c_ln_bias": nrm(ks[13], (DEPTH, GROUP_WIDTH), 0.02),
        "c_w_s": nrm(ks[14], (DEPTH, C_GROUPS, C_CHUNK, C_CHUNK), C_CHUNK ** -0.5),
        "c_b_s": 1.0 + nrm(ks[15], (DEPTH, C_GROUPS, C_CHUNK), 0.02),
        "d_conv": nrm(ks[16], (DEPTH, D_CONV, GROUP_WIDTH), D_CONV ** -0.5),
        "d_conv_bias": nrm(ks[17], (DEPTH, GROUP_WIDTH), 0.02),
        "d_ln_gain": 1.0 + nrm(ks[18], (DEPTH, GROUP_WIDTH), 0.02),
        "d_ln_bias": nrm(ks[19], (DEPTH, GROUP_WIDTH), 0.02),
        "ple_w": nrm(ks[20], (DEPTH, D_PLE, D_MODEL), D_PLE ** -0.5),
        "ple_gate_norm": 1.0 + nrm(ks[21], (DEPTH, D_MODEL), 0.02),
        "ple_gate_w": nrm(ks[22], (DEPTH, D_MODEL, D_MODEL), D_MODEL ** -0.5),
    }


def reference(x, p, norm_gain, w_in, w_out, a_q_gain, a_k_gain, a_sink, b_conv, b_a_log, b_dt_bias,
              b_out_gain, c_ln_gain, c_ln_bias, c_w_s, c_b_s, d_conv, d_conv_bias, d_ln_gain, d_ln_bias,
              ple_w, ple_gate_norm, ple_gate_w):
    for i in range(DEPTH):
        h = rms_norm(x, norm_gain[i])
        z = h @ w_in[i]
        a_q, a_k, a_v, b_qkv, b_beta, b_alpha, c_uv, d_glu, gate = split_columns(z, COL_SIZES)
        o_a = sliding_window_attention(a_q, a_k, a_v, a_q_gain[i], a_k_gain[i], a_sink[i])
        o_b = gated_deltanet(b_qkv, b_beta, b_alpha, b_conv[i], b_a_log[i], b_dt_bias[i],
                             b_out_gain[i]).astype(h.dtype)
        o_c = chunked_spatial_gating(c_uv, c_ln_gain[i], c_ln_bias[i], c_w_s[i], c_b_s[i])
        o_d = conformer_conv(d_glu, d_conv[i], d_conv_bias[i], d_ln_gain[i], d_ln_bias[i])
        mix = jnp.concatenate([o_a, o_b, o_c, o_d], -1) * jax.nn.silu(gate)
        x = x + mix @ w_out[i]
        ple_gate = jax.nn.sigmoid(rms_norm(x, ple_gate_norm[i]) @ ple_gate_w[i])
        x = x + ple_gate * (p[i] @ ple_w[i])
    return x
```

```python
import functools

import jax
import jax.numpy as jnp
from jax import lax
from jax.experimental import pallas as pl
from jax.experimental.pallas import tpu as pltpu

F32 = jnp.float32
BF16 = jnp.bfloat16

EPS = 1e-6
D_MODEL = 1024
D_PLE = 256
GROUP_WIDTH = 256
HEAD_DIM = 64
N_HEADS = 4
WINDOW = 128
ROPE_THETA = 10000.0
B_CONV = 4
DN_CHUNK = 64
C_CHUNK = 128
D_CONV = 31

COL_AQ, COL_AK, COL_AV = 0, 256, 384
COL_B = 512
COL_C = 1280
COL_D = 1792
COL_GATE = 2304
COL_TAIL = 3328
Z_WIDTH = 3456
HALO = 32

SEQ_TILE = 256
VMEM_LIMIT_BYTES = 56 * 1024 * 1024


def _sigmoid(x):
    return 1.0 / (1.0 + jnp.exp(-x))


def _silu(x):
    return x * _sigmoid(x)


def _dot(a, b):
    return jnp.dot(a, b, preferred_element_type=F32)


def _dot_nt(a, b):
    return lax.dot_general(a, b, (((1,), (1,)), ((), ())), preferred_element_type=F32)


def _dot_tn(a, b):
    return lax.dot_general(a, b, (((0,), (0,)), ((), ())), preferred_element_type=F32)


def _split_lhs_dot(a, b_bf16, n):
    acc = None
    r = a
    for i in range(n):
        hi = r.astype(BF16)
        t = _dot(hi, b_bf16)
        acc = t if acc is None else acc + t
        if i + 1 < n:
            r = r - hi.astype(F32)
    return acc


def _split_rhs_dot(a_bf16, b, n):
    acc = None
    r = b
    for i in range(n):
        hi = r.astype(BF16)
        t = _dot(a_bf16, hi)
        acc = t if acc is None else acc + t
        if i + 1 < n:
            r = r - hi.astype(F32)
    return acc


def _layer_kernel(sink_ref, x_ref, p_ref, cos_ref, sin_ref, w_in_ref, w_out_ref, ple_w_ref, gate_w_ref,
                  vec_ref, bconv_ref, cw_ref, cbias_ref, dconv_ref, o_ref,
                  z_ref, kbuf, vbuf, st_ref, mix_ref, ybuf, *, ts):
    sblk = pl.program_id(1)

    @pl.when(sblk == 0)
    def _():
        z_ref[0:HALO, :] = jnp.zeros((HALO, Z_WIDTH), F32)
        kbuf[0:WINDOW, :] = jnp.zeros((WINDOW, 128), F32)
        vbuf[0:WINDOW, :] = jnp.zeros((WINDOW, 128), F32)
        st_ref[...] = jnp.zeros((256, 256), F32)

    lane256 = lax.broadcasted_iota(jnp.int32, (1, 256), 1)
    head_mask = [(lane256 // HEAD_DIM) == h for h in range(N_HEADS)]
    r256 = lax.broadcasted_iota(jnp.int32, (256, 256), 0)
    c256 = lax.broadcasted_iota(jnp.int32, (256, 256), 1)
    same_head = (r256 // HEAD_DIM) == (c256 // HEAD_DIM)
    g_sum = jnp.where(same_head, 1.0, 0.0).astype(BF16)
    g_mean = jnp.where(same_head, 1.0 / HEAD_DIM, 0.0).astype(BF16)

    def head_mean(v, n=2):
        return _split_lhs_dot(v, g_mean, n)

    def head_sum(v, n=2):
        return _split_lhs_dot(v, g_sum, n)

    def bd(xv):
        return jnp.concatenate([jnp.where(head_mask[h], xv, 0.0).astype(BF16) for h in range(N_HEADS)], axis=0)

    def gate_silu(rows, c0, c1):
        return _silu(z_ref[rows, COL_GATE + c0:COL_GATE + c1])

    x = x_ref[0]
    ms = jnp.mean(x * x, axis=-1, keepdims=True)
    h = (x * lax.rsqrt(ms + EPS) * vec_ref[0:1, :]).astype(BF16)
    z_ref[HALO:HALO + ts, :] = _dot(h, w_in_ref[...])

    lane128 = lax.broadcasted_iota(jnp.int32, (1, 128), 1)
    first_half = (lane128 % HEAD_DIM) < (HEAD_DIM // 2)
    lo128 = lane128 < HEAD_DIM
    q_gain = vec_ref[2:3, 0:256]
    k_gain = vec_ref[2:3, 256:384]
    qi = lax.broadcasted_iota(jnp.int32, (4 * WINDOW, 2 * WINDOW), 0) % WINDOW
    kj = lax.broadcasted_iota(jnp.int32, (4 * WINDOW, 2 * WINDOW), 1)
    rel = qi + WINDOW - kj
    band = (rel >= 0) & (rel < WINDOW)

    def rope(v, cos, sin):
        sw = jnp.where(first_half, pltpu.roll(v, 128 - HEAD_DIM // 2, 1), pltpu.roll(v, HEAD_DIM // 2, 1))
        return v * cos + sw * sin

    for blk in range(ts // WINDOW):
        r0 = HALO + blk * WINDOW
        rows = slice(r0, r0 + WINDOW)
        q = z_ref[rows, COL_AQ:COL_AQ + 256]
        k = z_ref[rows, COL_AK:COL_AK + 128]
        v = z_ref[rows, COL_AV:COL_AV + 128]
        cos = cos_ref[blk * WINDOW:(blk + 1) * WINDOW, :]
        sin = sin_ref[blk * WINDOW:(blk + 1) * WINDOW, :]
        qn = q * lax.rsqrt(head_mean(q * q) + EPS) * q_gain
        kn = k * lax.rsqrt(_split_lhs_dot(k * k, g_mean[0:128, 0:128], 2) + EPS) * k_gain
        qa = rope(qn[:, 0:128], cos, sin)
        qb = rope(qn[:, 128:256], cos, sin)
        kbuf[WINDOW + blk * WINDOW:2 * WINDOW + blk * WINDOW, :] = rope(kn, cos, sin)
        vbuf[WINDOW + blk * WINDOW:2 * WINDOW + blk * WINDOW, :] = v
        k2 = kbuf[blk * WINDOW:(blk + 2) * WINDOW, :].astype(BF16)
        v2 = vbuf[blk * WINDOW:(blk + 2) * WINDOW, :].astype(BF16)
        lhs = jnp.concatenate([jnp.where(lo128, qa, 0.0), jnp.where(lo128, 0.0, qa),
                               jnp.where(lo128, qb, 0.0), jnp.where(lo128, 0.0, qb)], axis=0).astype(BF16)
        sc = _dot_nt(lhs, k2)
        valid = band
        if blk == 0:
            valid = valid & (kj >= jnp.where(sblk > 0, 0, WINDOW))
        sc = jnp.where(valid, sc, -jnp.inf)
        outs = []
        for r in range(4):
            s_r = sc[r * WINDOW:(r + 1) * WINDOW]
            sink = sink_ref[r]
            m = jnp.maximum(jnp.max(s_r, axis=-1, keepdims=True), sink)
            e = jnp.exp(s_r - m)
            den = jnp.sum(e, axis=-1, keepdims=True) + jnp.exp(sink - m)
            outs.append(_dot(e.astype(BF16), v2) / den)
        o_a0 = jnp.where(lo128, outs[0], outs[1])
        o_a1 = jnp.where(lo128, outs[2], outs[3])
        mrows = slice(blk * WINDOW, (blk + 1) * WINDOW)
        mix_ref[mrows, 0:128] = (o_a0 * gate_silu(rows, 0, 128)).astype(BF16)
        mix_ref[mrows, 128:256] = (o_a1 * gate_silu(rows, 128, 256)).astype(BF16)

    wb = bconv_ref[...]
    acc = None
    for kk in range(B_CONV):
        t = z_ref[HALO - (B_CONV - 1) + kk:HALO - (B_CONV - 1) + kk + ts, COL_B:COL_B + 768] * wb[kk:kk + 1, :]
        acc = t if acc is None else acc + t
    qkv = _silu(acc)
    bq = qkv[:, 0:256]
    bk = qkv[:, 256:512]
    bv = qkv[:, 512:768]
    bq = bq * lax.rsqrt(head_sum(bq * bq) + EPS) * (HEAD_DIM ** -0.5)
    bk = bk * lax.rsqrt(head_sum(bk * bk) + EPS)

    tail = z_ref[HALO:HALO + ts, COL_TAIL:COL_TAIL + 128]
    beta = _sigmoid(tail)
    xg = tail + vec_ref[2:3, 640:768]
    softplus = jnp.maximum(xg, 0.0) + jnp.log1p(jnp.exp(-jnp.abs(xg)))
    gdec = -jnp.exp(vec_ref[2:3, 512:640]) * softplus
    er = lax.broadcasted_iota(jnp.int32, (128, 256), 0)
    ec = lax.broadcasted_iota(jnp.int32, (128, 256), 1) // HEAD_DIM
    e_beta = jnp.where(er == ec, 1.0, 0.0).astype(BF16)
    e_alpha = jnp.where(er == ec + N_HEADS, 1.0, 0.0).astype(BF16)
    beta_f = _split_lhs_dot(beta, e_beta, 2)
    gdec_f = _split_lhs_dot(gdec, e_alpha, 3)
    tr = lax.broadcasted_iota(jnp.int32, (ts, ts), 0)
    tc = lax.broadcasted_iota(jnp.int32, (ts, ts), 1)
    l_bd = jnp.where(((tr // DN_CHUNK) == (tc // DN_CHUNK)) & (tr >= tc), 1.0, 0.0).astype(BF16)
    gc = _split_rhs_dot(l_bd, gdec_f, 3)

    ci = lax.broadcasted_iota(jnp.int32, (DN_CHUNK, 256), 0)
    cj = lax.broadcasted_iota(jnp.int32, (DN_CHUNK, 256), 1) % HEAD_DIM
    eye = ci == cj
    tril = ci >= cj
    strict = ci > cj
    same16 = (ci // 16) == (cj // 16)
    same32 = (ci // 32) == (cj // 32)
    eye_f = jnp.where(eye, 1.0, 0.0)
    out_gain = vec_ref[3:4, 0:256]

    def mm(lhs, xv):
        return _dot(lhs.astype(BF16), bd(xv))

    for c in range(ts // DN_CHUNK):
        sl = slice(c * DN_CHUNK, (c + 1) * DN_CHUNK)
        gc_c = gc[sl]
        gl = gc_c[DN_CHUNK - 1:DN_CHUNK, :]
        egc = jnp.exp(gc_c)
        ekg = jnp.exp(gl - gc_c)
        egl = jnp.exp(gl)
        k_c = bk[sl]
        q_c = bq[sl]
        b_c = beta_f[sl]
        kb = k_c * b_c
        vb = bv[sl] * b_c
        grow = jnp.sum(jnp.where(eye, gc_c, 0.0), axis=0, keepdims=True)
        decay = jnp.exp(jnp.minimum(gc_c - grow, 0.0))
        aq = _dot_nt(jnp.concatenate([kb, q_c], axis=0).astype(BF16), bd(k_c))
        a_m = jnp.where(strict, aq[0:DN_CHUNK] * decay, 0.0)
        qk = jnp.where(tril, aq[DN_CHUNK:2 * DN_CHUNK] * decay, 0.0)

        d1 = jnp.where(same16, a_m, 0.0)
        pm = eye_f - d1
        d2 = mm(d1, d1)
        r = mm(jnp.concatenate([pm, d2], axis=0), d2)
        pm = pm + r[0:DN_CHUNK]
        d4 = r[DN_CHUNK:]
        r = mm(jnp.concatenate([pm, d4], axis=0), d4)
        pm = pm + r[0:DN_CHUNK]
        d8 = r[DN_CHUNK:]
        t0 = pm + mm(pm, d8)
        m1 = jnp.where(same32 & jnp.logical_not(same16), a_m, 0.0)
        t1 = t0 - mm(mm(t0, m1), t0)
        m2 = jnp.where(same32, 0.0, a_m)
        t2 = t1 - mm(mm(t1, m2), t1)
        t2b = t2.astype(BF16)
        u_c = _dot(t2b, bd(vb))
        w_c = _dot(t2b, bd(kb * egc))

        st = st_ref[...]
        ws = _dot_nt(jnp.concatenate([w_c, q_c * egc], axis=0).astype(BF16), st.astype(BF16))
        vn = u_c - ws[0:DN_CHUNK]
        o_c = ws[DN_CHUNK:] + _dot(qk.astype(BF16), bd(vn))
        upd = _dot_tn(vn.astype(BF16), (k_c * ekg).astype(BF16))
        st_ref[...] = st * egl + jnp.where(same_head, upd, 0.0)

        o_n = o_c * lax.rsqrt(head_mean(o_c * o_c) + EPS) * out_gain
        zrows = slice(HALO + c * DN_CHUNK, HALO + (c + 1) * DN_CHUNK)
        mix_ref[sl, 256:512] = (o_n * gate_silu(zrows, 256, 512)).astype(BF16)

    uv = z_ref[HALO:HALO + ts, COL_C:COL_C + 512]
    gl_uv = 0.5 * uv * (1.0 + jnp.tanh(0.7978845608028654 * (uv + 0.044715 * (uv * uv * uv))))
    cu = gl_uv[:, 0:256]
    cv = gl_uv[:, 256:512]
    mu = jnp.mean(cv, axis=-1, keepdims=True)
    dv = cv - mu
    var = jnp.mean(dv * dv, axis=-1, keepdims=True)
    cvn = dv * lax.rsqrt(var + EPS) * vec_ref[3:4, 256:512] + vec_ref[3:4, 512:768]
    wi = lax.broadcasted_iota(jnp.int32, (C_CHUNK, 4 * C_CHUNK), 0)
    wj = lax.broadcasted_iota(jnp.int32, (C_CHUNK, 4 * C_CHUNK), 1) % C_CHUNK
    cw = jnp.where(wi >= wj, cw_ref[...], 0.0).astype(BF16)
    cbias = cbias_ref[...]
    for c in range(ts // C_CHUNK):
        sl = slice(c * C_CHUNK, (c + 1) * C_CHUNK)
        v_c = cvn[sl]
        vblk = jnp.concatenate([jnp.where(head_mask[g], v_c, 0.0).astype(BF16) for g in range(4)], axis=0)
        zc = _dot(cw, vblk) + cbias
        zrows = slice(HALO + c * C_CHUNK, HALO + (c + 1) * C_CHUNK)
        mix_ref[sl, 512:768] = (cu[sl] * zc * gate_silu(zrows, 512, 768)).astype(BF16)

    ag = z_ref[0:HALO + ts, COL_D:COL_D + 512]
    ybuf[...] = ag[:, 0:256] * _sigmoid(ag[:, 256:512])
    wd = dconv_ref[...]
    acc = None
    for kk in range(D_CONV):
        r0 = HALO - (D_CONV - 1) + kk
        t = ybuf[r0:r0 + ts, :] * wd[kk:kk + 1, :]
        acc = t if acc is None else acc + t
    yc = acc + vec_ref[3:4, 768:1024]
    mu = head_mean(yc, 3)
    dy = yc - mu
    var = head_mean(dy * dy, 2)
    yn = dy * lax.rsqrt(var + EPS) * vec_ref[4:5, 0:256] + vec_ref[4:5, 256:512]
    mix_ref[:, 768:1024] = (_silu(yn) * gate_silu(slice(HALO, HALO + ts), 768, 1024)).astype(BF16)

    x1 = x + _dot(mix_ref[...], w_out_ref[...])
    ms1 = jnp.mean(x1 * x1, axis=-1, keepdims=True)
    hg = (x1 * lax.rsqrt(ms1 + EPS) * vec_ref[1:2, :]).astype(BF16)
    ple_gate = _sigmoid(_dot(hg, gate_w_ref[...]))
    pe = _dot(p_ref[0].astype(BF16), ple_w_ref[...])
    o_ref[0] = x1 + ple_gate * pe

    z_ref[0:HALO, :] = z_ref[ts:ts + HALO, :]
    kbuf[0:WINDOW, :] = kbuf[ts:ts + WINDOW, :]
    vbuf[0:WINDOW, :] = vbuf[ts:ts + WINDOW, :]


def _const_spec(shape):
    return pl.BlockSpec(shape, lambda b, s: (0,) * len(shape))


def _layer_call(x, p, cos_t, sin_t, sink, w_in, w_out, ple_w, gate_w, vec, bconv, cw, cbias, dconv, ts):
    bsz, seq, _ = x.shape
    grid = (bsz, seq // ts)
    in_specs = [
        pl.BlockSpec(memory_space=pltpu.SMEM),
        pl.BlockSpec((1, ts, D_MODEL), lambda b, s: (b, s, 0)),
        pl.BlockSpec((1, ts, D_PLE), lambda b, s: (b, s, 0)),
        pl.BlockSpec((ts, 128), lambda b, s: (s, 0)),
        pl.BlockSpec((ts, 128), lambda b, s: (s, 0)),
        _const_spec(w_in.shape), _const_spec(w_out.shape), _const_spec(ple_w.shape), _const_spec(gate_w.shape),
        _const_spec(vec.shape), _const_spec(bconv.shape), _const_spec(cw.shape), _const_spec(cbias.shape),
        _const_spec(dconv.shape),
    ]
    scratch = [
        pltpu.VMEM((HALO + ts, Z_WIDTH), F32),
        pltpu.VMEM((WINDOW + ts, 128), F32),
        pltpu.VMEM((WINDOW + ts, 128), F32),
        pltpu.VMEM((256, 256), F32),
        pltpu.VMEM((ts, D_MODEL), BF16),
        pltpu.VMEM((HALO + ts, GROUP_WIDTH), F32),
    ]
    return pl.pallas_call(
        functools.partial(_layer_kernel, ts=ts),
        grid=grid,
        in_specs=in_specs,
        out_specs=pl.BlockSpec((1, ts, D_MODEL), lambda b, s: (b, s, 0)),
        out_shape=jax.ShapeDtypeStruct(x.shape, F32),
        scratch_shapes=scratch,
        compiler_params=pltpu.CompilerParams(
            dimension_semantics=("arbitrary", "arbitrary"),
            vmem_limit_bytes=VMEM_LIMIT_BYTES,
        ),
        name="hybrid_trunk_layer",
    )(sink, x, p, cos_t, sin_t, w_in, w_out, ple_w, gate_w, vec, bconv, cw, cbias, dconv)


_HEAD_PERM = (0, 2, 1, 3)


def _perm_cols(w):
    parts = [w[..., h * HEAD_DIM:(h + 1) * HEAD_DIM] for h in _HEAD_PERM]
    return jnp.concatenate(parts, axis=-1)


def _pad_lanes(v, start, width=128):
    out = jnp.zeros((width,), F32)
    return out.at[start:start + v.shape[0]].set(v.astype(F32))


def kernel(x, p, norm_gain, w_in, w_out, a_q_gain, a_k_gain, a_sink, b_conv, b_a_log, b_dt_bias, b_out_gain,
           c_ln_gain, c_ln_bias, c_w_s, c_b_s, d_conv, d_conv_bias, d_ln_gain, d_ln_bias,
           ple_w, ple_gate_norm, ple_gate_w):
    depth = w_in.shape[0]
    bsz, seq, _ = x.shape
    ts = min(SEQ_TILE, seq)
    assert seq % ts == 0 and ts % WINDOW == 0

    inv = ROPE_THETA ** (-jnp.arange(0, HEAD_DIM, 2, dtype=F32) / HEAD_DIM)
    ang = jnp.arange(seq, dtype=F32)[:, None] * inv[None, :]
    cos, sin = jnp.cos(ang), jnp.sin(ang)
    cos_t = jnp.concatenate([cos, cos, cos, cos], axis=-1)
    sin_t = jnp.concatenate([-sin, sin, -sin, sin], axis=-1)

    for i in range(depth):
        wi = w_in[i]
        gate_cols = wi[:, 2312:3336]
        w_in_r = jnp.concatenate([
            _perm_cols(wi[:, 0:256]), wi[:, 256:512], wi[:, 512:1280], wi[:, 1288:1800], wi[:, 1800:2312],
            _perm_cols(gate_cols[:, 0:256]), gate_cols[:, 256:],
            wi[:, 1280:1288], jnp.zeros((D_MODEL, 120), F32)], axis=1).astype(BF16)
        wo = w_out[i]
        w_out_r = jnp.concatenate([
            jnp.concatenate([wo[h * HEAD_DIM:(h + 1) * HEAD_DIM] for h in _HEAD_PERM], axis=0), wo[256:]],
            axis=0).astype(BF16)
        sink = a_sink[i][jnp.array(_HEAD_PERM)].astype(F32)
        zeros = jnp.zeros((D_MODEL,), F32)
        row2 = jnp.concatenate([
            jnp.tile(a_q_gain[i] * (HEAD_DIM ** -0.5), 4), jnp.tile(a_k_gain[i], 2), jnp.zeros((128,), F32),
            _pad_lanes(b_a_log[i], N_HEADS), _pad_lanes(b_dt_bias[i], N_HEADS), jnp.zeros((256,), F32)])
        row3 = jnp.concatenate([jnp.tile(b_out_gain[i], 4), c_ln_gain[i], c_ln_bias[i], d_conv_bias[i]])
        row4 = jnp.concatenate([d_ln_gain[i], d_ln_bias[i], jnp.zeros((512,), F32)])
        vec = jnp.stack([norm_gain[i], ple_gate_norm[i], row2, row3, row4, zeros, zeros, zeros]).astype(F32)
        cw = jnp.transpose(c_w_s[i], (1, 0, 2)).reshape(C_CHUNK, 4 * C_CHUNK)
        cbias = jnp.repeat(c_b_s[i].T, GROUP_WIDTH // 4, axis=1)
        x = _layer_call(x, p[i], cos_t, sin_t, sink, w_in_r, w_out_r, ple_w[i].astype(BF16),
                        ple_gate_w[i].astype(BF16), vec, b_conv[i], cw, cbias, d_conv[i], ts)
    return x
```

```python
import functools

import jax
import jax.numpy as jnp
from jax import lax
from jax.experimental import pallas as pl
from jax.experimental.pallas import tpu as pltpu

F32 = jnp.float32
BF16 = jnp.bfloat16

EPS = 1e-6
D_MODEL = 1024
D_PLE = 256
GROUP_WIDTH = 256
HEAD_DIM = 64
N_HEADS = 4
WINDOW = 128
ROPE_THETA = 10000.0
B_CONV = 4
DN_CHUNK = 64
C_CHUNK = 128
D_CONV = 31

COL_AQ, COL_AK, COL_AV = 0, 256, 384
COL_B = 512
COL_C = 1280
COL_D = 1792
COL_GATE = 2304
COL_TAIL = 3328
Z_WIDTH = 3456
HALO = 32

SEQ_TILE = 256
VMEM_LIMIT_BYTES = 56 * 1024 * 1024


def _sigmoid(x):
    return 1.0 / (1.0 + jnp.exp(-x))


def _silu(x):
    return x * _sigmoid(x)


def _dot(a, b):
    return jnp.dot(a, b, preferred_element_type=F32)


def _dot_nt(a, b):
    return lax.dot_general(a, b, (((1,), (1,)), ((), ())), preferred_element_type=F32)


def _dot_tn(a, b):
    return lax.dot_general(a, b, (((0,), (0,)), ((), ())), preferred_element_type=F32)


def _split_lhs_dot(a, b_bf16, n):
    acc = None
    r = a
    for i in range(n):
        hi = r.astype(BF16)
        t = _dot(hi, b_bf16)
        acc = t if acc is None else acc + t
        if i + 1 < n:
            r = r - hi.astype(F32)
    return acc


def _split_rhs_dot(a_bf16, b, n):
    acc = None
    r = b
    for i in range(n):
        hi = r.astype(BF16)
        t = _dot(a_bf16, hi)
        acc = t if acc is None else acc + t
        if i + 1 < n:
            r = r - hi.astype(F32)
    return acc


def _layer_kernel(sink_ref, x_ref, p_ref, cos_ref, sin_ref, w_in_ref, w_out_ref, ple_w_ref, gate_w_ref,
                  vec_ref, bconv_ref, cw_ref, cbias_ref, dconv_ref, o_ref,
                  z_ref, kbuf, vbuf, st_ref, mix_ref, ybuf, *, ts):
    sblk = pl.program_id(1)

    @pl.when(sblk == 0)
    def _():
        z_ref[0:HALO, :] = jnp.zeros((HALO, Z_WIDTH), F32)
        kbuf[0:WINDOW, :] = jnp.zeros((WINDOW, 128), F32)
        vbuf[0:WINDOW, :] = jnp.zeros((WINDOW, 128), F32)
        st_ref[...] = jnp.zeros((HEAD_DIM, 256), F32)

    lane256 = lax.broadcasted_iota(jnp.int32, (1, 256), 1)
    head_mask = [(lane256 // HEAD_DIM) == h for h in range(N_HEADS)]
    r256 = lax.broadcasted_iota(jnp.int32, (256, 256), 0)
    c256 = lax.broadcasted_iota(jnp.int32, (256, 256), 1)
    same_head = (r256 // HEAD_DIM) == (c256 // HEAD_DIM)
    g_sum = jnp.where(same_head, 1.0, 0.0).astype(BF16)
    g_mean = jnp.where(same_head, 1.0 / HEAD_DIM, 0.0).astype(BF16)

    def head_mean(v, n=2):
        return _split_lhs_dot(v, g_mean, n)

    def head_sum(v, n=2):
        return _split_lhs_dot(v, g_sum, n)

    def bd(xv):
        xb = xv.astype(BF16)
        zero = jnp.zeros_like(xb)
        return jnp.concatenate([jnp.where(head_mask[h], xb, zero) for h in range(N_HEADS)], axis=0)

    def gate_silu(rows, c0, c1):
        return _silu(z_ref[rows, COL_GATE + c0:COL_GATE + c1])

    x = x_ref[0]
    ms = jnp.mean(x * x, axis=-1, keepdims=True)
    h = (x * lax.rsqrt(ms + EPS) * vec_ref[0:1, :]).astype(BF16)
    z_ref[HALO:HALO + ts, :] = _dot(h, w_in_ref[...])

    lane128 = lax.broadcasted_iota(jnp.int32, (1, 128), 1)
    first_half = (lane128 % HEAD_DIM) < (HEAD_DIM // 2)
    lo128 = lane128 < HEAD_DIM
    q_gain = vec_ref[2:3, 0:256]
    k_gain = vec_ref[2:3, 256:384]
    qi = lax.broadcasted_iota(jnp.int32, (4 * WINDOW, 2 * WINDOW), 0) % WINDOW
    kj = lax.broadcasted_iota(jnp.int32, (4 * WINDOW, 2 * WINDOW), 1)
    rel = qi + WINDOW - kj
    band = (rel >= 0) & (rel < WINDOW)

    def rope(v, cos, sin):
        sw = jnp.where(first_half, pltpu.roll(v, 128 - HEAD_DIM // 2, 1), pltpu.roll(v, HEAD_DIM // 2, 1))
        return v * cos + sw * sin

    for blk in range(ts // WINDOW):
        r0 = HALO + blk * WINDOW
        rows = slice(r0, r0 + WINDOW)
        q = z_ref[rows, COL_AQ:COL_AQ + 256]
        k = z_ref[rows, COL_AK:COL_AK + 128]
        v = z_ref[rows, COL_AV:COL_AV + 128]
        cos = cos_ref[blk * WINDOW:(blk + 1) * WINDOW, :]
        sin = sin_ref[blk * WINDOW:(blk + 1) * WINDOW, :]
        qn = q * lax.rsqrt(head_mean(q * q) + EPS) * q_gain
        kn = k * lax.rsqrt(_split_lhs_dot(k * k, g_mean[0:128, 0:128], 2) + EPS) * k_gain
        qa = rope(qn[:, 0:128], cos, sin)
        qb = rope(qn[:, 128:256], cos, sin)
        kbuf[WINDOW + blk * WINDOW:2 * WINDOW + blk * WINDOW, :] = rope(kn, cos, sin)
        vbuf[WINDOW + blk * WINDOW:2 * WINDOW + blk * WINDOW, :] = v
        k2 = kbuf[blk * WINDOW:(blk + 2) * WINDOW, :].astype(BF16)
        v2 = vbuf[blk * WINDOW:(blk + 2) * WINDOW, :].astype(BF16)
        lhs = jnp.concatenate([jnp.where(lo128, qa, 0.0), jnp.where(lo128, 0.0, qa),
                               jnp.where(lo128, qb, 0.0), jnp.where(lo128, 0.0, qb)], axis=0).astype(BF16)
        sc = _dot_nt(lhs, k2)
        valid = band
        if blk == 0:
            valid = valid & (kj >= jnp.where(sblk > 0, 0, WINDOW))
        sc = jnp.where(valid, sc, -jnp.inf)
        outs = []
        for r in range(4):
            s_r = sc[r * WINDOW:(r + 1) * WINDOW]
            sink = sink_ref[r]
            m = jnp.maximum(jnp.max(s_r, axis=-1, keepdims=True), sink)
            e = jnp.exp(s_r - m)
            den = jnp.sum(e, axis=-1, keepdims=True) + jnp.exp(sink - m)
            outs.append(_dot(e.astype(BF16), v2) / den)
        o_a0 = jnp.where(lo128, outs[0], outs[1])
        o_a1 = jnp.where(lo128, outs[2], outs[3])
        mrows = slice(blk * WINDOW, (blk + 1) * WINDOW)
        mix_ref[mrows, 0:128] = (o_a0 * gate_silu(rows, 0, 128)).astype(BF16)
        mix_ref[mrows, 128:256] = (o_a1 * gate_silu(rows, 128, 256)).astype(BF16)

    wb = bconv_ref[...]
    acc = None
    for kk in range(B_CONV):
        t = z_ref[HALO - (B_CONV - 1) + kk:HALO - (B_CONV - 1) + kk + ts, COL_B:COL_B + 768] * wb[kk:kk + 1, :]
        acc = t if acc is None else acc + t
    qkv = _silu(acc)
    bq = qkv[:, 0:256]
    bk = qkv[:, 256:512]
    bv = qkv[:, 512:768]
    bq = bq * lax.rsqrt(head_sum(bq * bq) + EPS) * (HEAD_DIM ** -0.5)
    bk = bk * lax.rsqrt(head_sum(bk * bk) + EPS)

    tail = z_ref[HALO:HALO + ts, COL_TAIL:COL_TAIL + 128]
    beta = _sigmoid(tail)
    xg = tail + vec_ref[2:3, 640:768]
    softplus = jnp.maximum(xg, 0.0) + jnp.log1p(jnp.exp(-jnp.abs(xg)))
    gdec = -jnp.exp(vec_ref[2:3, 512:640]) * softplus
    er = lax.broadcasted_iota(jnp.int32, (128, 256), 0)
    ec = lax.broadcasted_iota(jnp.int32, (128, 256), 1) // HEAD_DIM
    e_beta = jnp.where(er == ec, 1.0, 0.0).astype(BF16)
    e_alpha = jnp.where(er == ec + N_HEADS, 1.0, 0.0).astype(BF16)
    beta_f = _split_lhs_dot(beta, e_beta, 2)
    gdec_f = _split_lhs_dot(gdec, e_alpha, 3)
    tr = lax.broadcasted_iota(jnp.int32, (ts, ts), 0)
    tc = lax.broadcasted_iota(jnp.int32, (ts, ts), 1)
    l_bd = jnp.where(((tr // DN_CHUNK) == (tc // DN_CHUNK)) & (tr >= tc), 1.0, 0.0).astype(BF16)
    gc = _split_rhs_dot(l_bd, gdec_f, 3)

    ci = lax.broadcasted_iota(jnp.int32, (DN_CHUNK, 256), 0)
    cj = lax.broadcasted_iota(jnp.int32, (DN_CHUNK, 256), 1) % HEAD_DIM
    eye = ci == cj
    tril = ci >= cj
    strict = ci > cj
    same16 = (ci // 16) == (cj // 16)
    same32 = (ci // 32) == (cj // 32)
    eye_f = jnp.where(eye, 1.0, 0.0)
    out_gain = vec_ref[3:4, 0:256]

    def mm(lhs, xv):
        return _dot(lhs.astype(BF16), bd(xv))

    cs = range(ts // DN_CHUNK)
    sls = [slice(c * DN_CHUNK, (c + 1) * DN_CHUNK) for c in cs]
    gc_c = [gc[s] for s in sls]
    gl = [g[DN_CHUNK - 1:DN_CHUNK, :] for g in gc_c]
    egc = [jnp.exp(g) for g in gc_c]
    ekg = [jnp.exp(gl[c] - gc_c[c]) for c in cs]
    egl = [jnp.exp(g) for g in gl]
    k_c = [bk[s] for s in sls]
    q_c = [bq[s] for s in sls]
    kb = [k_c[c] * beta_f[sls[c]] for c in cs]
    vb = [bv[sls[c]] * beta_f[sls[c]] for c in cs]
    grow = [jnp.sum(jnp.where(eye, g, 0.0), axis=0, keepdims=True) for g in gc_c]
    decay = [jnp.exp(jnp.minimum(gc_c[c] - grow[c], 0.0)) for c in cs]
    aq = [_dot_nt(jnp.concatenate([kb[c], q_c[c]], axis=0).astype(BF16), bd(k_c[c])) for c in cs]
    a_m = [jnp.where(strict, aq[c][0:DN_CHUNK] * decay[c], 0.0) for c in cs]
    qk = [jnp.where(tril, aq[c][DN_CHUNK:2 * DN_CHUNK] * decay[c], 0.0) for c in cs]

    d1 = [jnp.where(same16, a, 0.0) for a in a_m]
    pm = [eye_f - d for d in d1]
    d2 = [mm(d, d) for d in d1]
    r = [mm(jnp.concatenate([pm[c], d2[c]], axis=0), d2[c]) for c in cs]
    pm = [pm[c] + r[c][0:DN_CHUNK] for c in cs]
    d4 = [r[c][DN_CHUNK:] for c in cs]
    r = [mm(jnp.concatenate([pm[c], d4[c]], axis=0), d4[c]) for c in cs]
    pm = [pm[c] + r[c][0:DN_CHUNK] for c in cs]
    d8 = [r[c][DN_CHUNK:] for c in cs]
    t0 = [pm[c] + mm(pm[c], d8[c]) for c in cs]
    m1 = [jnp.where(same32 & jnp.logical_not(same16), a, 0.0) for a in a_m]
    r = [mm(t0[c], m1[c]) for c in cs]
    t1 = [t0[c] - mm(r[c], t0[c]) for c in cs]
    m2 = [jnp.where(same32, 0.0, a) for a in a_m]
    r = [mm(t1[c], m2[c]) for c in cs]
    t2 = [(t1[c] - mm(r[c], t1[c])).astype(BF16) for c in cs]
    u_c = [_dot(t2[c], bd(vb[c])) for c in cs]
    w_c = [_dot(t2[c], bd(kb[c] * egc[c])) for c in cs]

    kg = [(k_c[c] * ekg[c]).astype(BF16) for c in cs]
    nmat = [jnp.where(same_head, _dot_tn(w_c[c].astype(BF16), kg[c]), 0.0).astype(BF16) for c in cs]
    bfull = [jnp.where(same_head, _dot_tn(u_c[c].astype(BF16), kg[c]), 0.0) for c in cs]
    bcomp = [b[0:64] + b[64:128] + b[128:192] + b[192:256] for b in bfull]
    qkb = [q.astype(BF16) for q in qk]
    qt = [(q_c[c] * egc[c] - _dot(qkb[c], bd(w_c[c]))).astype(BF16) for c in cs]
    ou = [_dot(qkb[c], bd(u_c[c])) for c in cs]

    st = st_ref[...]
    o_c = []
    for c in cs:
        o_c.append(_dot_nt(qt[c], bd(st)) + ou[c])
        st = st * egl[c] - _dot(st.astype(BF16), nmat[c]) + bcomp[c]
    st_ref[...] = st

    for c in cs:
        o_n = o_c[c] * lax.rsqrt(head_mean(o_c[c] * o_c[c]) + EPS) * out_gain
        zrows = slice(HALO + c * DN_CHUNK, HALO + (c + 1) * DN_CHUNK)
        mix_ref[sls[c], 256:512] = (o_n * gate_silu(zrows, 256, 512)).astype(BF16)

    uv = z_ref[HALO:HALO + ts, COL_C:COL_C + 512]
    gl_uv = 0.5 * uv * (1.0 + jnp.tanh(0.7978845608028654 * (uv + 0.044715 * (uv * uv * uv))))
    cu = gl_uv[:, 0:256]
    cv = gl_uv[:, 256:512]
    mu = jnp.mean(cv, axis=-1, keepdims=True)
    dv = cv - mu
    var = jnp.mean(dv * dv, axis=-1, keepdims=True)
    cvn = dv * lax.rsqrt(var + EPS) * vec_ref[3:4, 256:512] + vec_ref[3:4, 512:768]
    wi = lax.broadcasted_iota(jnp.int32, (C_CHUNK, 4 * C_CHUNK), 0)
    wj = lax.broadcasted_iota(jnp.int32, (C_CHUNK, 4 * C_CHUNK), 1) % C_CHUNK
    cw = jnp.where(wi >= wj, cw_ref[...], 0.0).astype(BF16)
    cbias = cbias_ref[...]
    for c in range(ts // C_CHUNK):
        sl = slice(c * C_CHUNK, (c + 1) * C_CHUNK)
        v_c = cvn[sl]
        vblk = jnp.concatenate([jnp.where(head_mask[g], v_c, 0.0).astype(BF16) for g in range(4)], axis=0)
        zc = _dot(cw, vblk) + cbias
        zrows = slice(HALO + c * C_CHUNK, HALO + (c + 1) * C_CHUNK)
        mix_ref[sl, 512:768] = (cu[sl] * zc * gate_silu(zrows, 512, 768)).astype(BF16)

    ag = z_ref[0:HALO + ts, COL_D:COL_D + 512]
    ybuf[...] = ag[:, 0:256] * _sigmoid(ag[:, 256:512])
    wd = dconv_ref[...]
    acc = None
    for kk in range(D_CONV):
        r0 = HALO - (D_CONV - 1) + kk
        t = ybuf[r0:r0 + ts, :] * wd[kk:kk + 1, :]
        acc = t if acc is None else acc + t
    yc = acc + vec_ref[3:4, 768:1024]
    mu = head_mean(yc, 3)
    dy = yc - mu
    var = head_mean(dy * dy, 2)
    yn = dy * lax.rsqrt(var + EPS) * vec_ref[4:5, 0:256] + vec_ref[4:5, 256:512]
    mix_ref[:, 768:1024] = (_silu(yn) * gate_silu(slice(HALO, HALO + ts), 768, 1024)).astype(BF16)

    x1 = x + _dot(mix_ref[...], w_out_ref[...])
    ms1 = jnp.mean(x1 * x1, axis=-1, keepdims=True)
    hg = (x1 * lax.rsqrt(ms1 + EPS) * vec_ref[1:2, :]).astype(BF16)
    ple_gate = _sigmoid(_dot(hg, gate_w_ref[...]))
    pe = _dot(p_ref[0].astype(BF16), ple_w_ref[...])
    o_ref[0] = x1 + ple_gate * pe

    z_ref[0:HALO, :] = z_ref[ts:ts + HALO, :]
    kbuf[0:WINDOW, :] = kbuf[ts:ts + WINDOW, :]
    vbuf[0:WINDOW, :] = vbuf[ts:ts + WINDOW, :]


def _const_spec(shape):
    return pl.BlockSpec(shape, lambda b, s: (0,) * len(shape))


def _layer_call(x, p, cos_t, sin_t, sink, w_in, w_out, ple_w, gate_w, vec, bconv, cw, cbias, dconv, ts):
    bsz, seq, _ = x.shape
    grid = (bsz, seq // ts)
    in_specs = [
        pl.BlockSpec(memory_space=pltpu.SMEM),
        pl.BlockSpec((1, ts, D_MODEL), lambda b, s: (b, s, 0)),
        pl.BlockSpec((1, ts, D_PLE), lambda b, s: (b, s, 0)),
        pl.BlockSpec((ts, 128), lambda b, s: (s, 0)),
        pl.BlockSpec((ts, 128), lambda b, s: (s, 0)),
        _const_spec(w_in.shape), _const_spec(w_out.shape), _const_spec(ple_w.shape), _const_spec(gate_w.shape),
        _const_spec(vec.shape), _const_spec(bconv.shape), _const_spec(cw.shape), _const_spec(cbias.shape),
        _const_spec(dconv.shape),
    ]
    scratch = [
        pltpu.VMEM((HALO + ts, Z_WIDTH), F32),
        pltpu.VMEM((WINDOW + ts, 128), F32),
        pltpu.VMEM((WINDOW + ts, 128), F32),
        pltpu.VMEM((HEAD_DIM, 256), F32),
        pltpu.VMEM((ts, D_MODEL), BF16),
        pltpu.VMEM((HALO + ts, GROUP_WIDTH), F32),
    ]
    return pl.pallas_call(
        functools.partial(_layer_kernel, ts=ts),
        grid=grid,
        in_specs=in_specs,
        out_specs=pl.BlockSpec((1, ts, D_MODEL), lambda b, s: (b, s, 0)),
        out_shape=jax.ShapeDtypeStruct(x.shape, F32),
        scratch_shapes=scratch,
        compiler_params=pltpu.CompilerParams(
            dimension_semantics=("arbitrary", "arbitrary"),
            vmem_limit_bytes=VMEM_LIMIT_BYTES,
        ),
        name="hybrid_trunk_layer",
    )(sink, x, p, cos_t, sin_t, w_in, w_out, ple_w, gate_w, vec, bconv, cw, cbias, dconv)


_HEAD_PERM = (0, 2, 1, 3)


def _perm_cols(w):
    parts = [w[..., h * HEAD_DIM:(h + 1) * HEAD_DIM] for h in _HEAD_PERM]
    return jnp.concatenate(parts, axis=-1)


def _pad_lanes(v, start, width=128):
    out = jnp.zeros((width,), F32)
    return out.at[start:start + v.shape[0]].set(v.astype(F32))


def kernel(x, p, norm_gain, w_in, w_out, a_q_gain, a_k_gain, a_sink, b_conv, b_a_log, b_dt_bias, b_out_gain,
           c_ln_gain, c_ln_bias, c_w_s, c_b_s, d_conv, d_conv_bias, d_ln_gain, d_ln_bias,
           ple_w, ple_gate_norm, ple_gate_w):
    depth = w_in.shape[0]
    bsz, seq, _ = x.shape
    ts = min(SEQ_TILE, seq)
    assert seq % ts == 0 and ts % WINDOW == 0

    inv = ROPE_THETA ** (-jnp.arange(0, HEAD_DIM, 2, dtype=F32) / HEAD_DIM)
    ang = jnp.arange(seq, dtype=F32)[:, None] * inv[None, :]
    cos, sin = jnp.cos(ang), jnp.sin(ang)
    cos_t = jnp.concatenate([cos, cos, cos, cos], axis=-1)
    sin_t = jnp.concatenate([-sin, sin, -sin, sin], axis=-1)

    for i in range(depth):
        wi = w_in[i]
        gate_cols = wi[:, 2312:3336]
        w_in_r = jnp.concatenate([
            _perm_cols(wi[:, 0:256]), wi[:, 256:512], wi[:, 512:1280], wi[:, 1288:1800], wi[:, 1800:2312],
            _perm_cols(gate_cols[:, 0:256]), gate_cols[:, 256:],
            wi[:, 1280:1288], jnp.zeros((D_MODEL, 120), F32)], axis=1).astype(BF16)
        wo = w_out[i]
        w_out_r = jnp.concatenate([
            jnp.concatenate([wo[h * HEAD_DIM:(h + 1) * HEAD_DIM] for h in _HEAD_PERM], axis=0), wo[256:]],
            axis=0).astype(BF16)
        sink = a_sink[i][jnp.array(_HEAD_PERM)].astype(F32)
        zeros = jnp.zeros((D_MODEL,), F32)
        row2 = jnp.concatenate([
            jnp.tile(a_q_gain[i] * (HEAD_DIM ** -0.5), 4), jnp.tile(a_k_gain[i], 2), jnp.zeros((128,), F32),
            _pad_lanes(b_a_log[i], N_HEADS), _pad_lanes(b_dt_bias[i], N_HEADS), jnp.zeros((256,), F32)])
        row3 = jnp.concatenate([jnp.tile(b_out_gain[i], 4), c_ln_gain[i], c_ln_bias[i], d_conv_bias[i]])
        row4 = jnp.concatenate([d_ln_gain[i], d_ln_bias[i], jnp.zeros((512,), F32)])
        vec = jnp.stack([norm_gain[i], ple_gate_norm[i], row2, row3, row4, zeros, zeros, zeros]).astype(F32)
        cw = jnp.transpose(c_w_s[i], (1, 0, 2)).reshape(C_CHUNK, 4 * C_CHUNK)
        cbias = jnp.repeat(c_b_s[i].T, GROUP_WIDTH // 4, axis=1)
        x = _layer_call(x, p[i], cos_t, sin_t, sink, w_in_r, w_out_r, ple_w[i].astype(BF16),
                        ple_gate_w[i].astype(BF16), vec, b_conv[i], cw, cbias, d_conv[i], ts)
    return x
```

```python
import functools

import jax
import jax.numpy as jnp
from jax import lax
from jax.experimental import pallas as pl
from jax.experimental.pallas import tpu as pltpu

F32 = jnp.float32
BF16 = jnp.bfloat16

EPS = 1e-6
D_MODEL = 1024
D_PLE = 256
GROUP_WIDTH = 256
HEAD_DIM = 64
N_HEADS = 4
WINDOW = 128
ROPE_THETA = 10000.0
B_CONV = 4
DN_CHUNK = 64
C_CHUNK = 128
D_CONV = 31

COL_AQ, COL_AK, COL_AV = 0, 256, 384
COL_B = 512
COL_C = 1280
COL_D = 1792
COL_GATE = 2304
COL_TAIL = 3328
Z_WIDTH = 3456
HALO = 32

SEQ_TILE = 256
PROJ_CHUNK = 256
VMEM_LIMIT_BYTES = 56 * 1024 * 1024


def _sigmoid(x):
    return 0.5 + 0.5 * jnp.tanh(0.5 * x)


def _silu(x):
    h = 0.5 * x
    return h + h * jnp.tanh(h)


def _dot(a, b):
    return jnp.dot(a, b, preferred_element_type=F32)


def _dot_nt(a, b):
    return lax.dot_general(a, b, (((1,), (1,)), ((), ())), preferred_element_type=F32)


def _dot_tn(a, b):
    return lax.dot_general(a, b, (((0,), (0,)), ((), ())), preferred_element_type=F32)


def _split_lhs_dot(a, b_bf16, n):
    acc = None
    r = a
    for i in range(n):
        hi = r.astype(BF16)
        t = _dot(hi, b_bf16)
        acc = t if acc is None else acc + t
        if i + 1 < n:
            r = r - hi.astype(F32)
    return acc


def _split_rhs_dot(a_bf16, b, n):
    acc = None
    r = b
    for i in range(n):
        hi = r.astype(BF16)
        t = _dot(a_bf16, hi)
        acc = t if acc is None else acc + t
        if i + 1 < n:
            r = r - hi.astype(F32)
    return acc


def _layer_kernel(sink_ref, x_ref, xn_ref, p_ref, cos_ref, sin_ref, w_in_ref, w_out_ref, ple_w_ref, gate_w_ref,
                  vec_ref, bconv_ref, cw_ref, cbias_ref, dconv_ref, o_ref,
                  za_ref, zb_ref, h_ref, kbuf, vbuf, st_ref, mix_ref, ybuf, ysh, *, ts, steps_per_seq):
    step = pl.program_id(0)
    sp = step % steps_per_seq

    lane256 = lax.broadcasted_iota(jnp.int32, (1, 256), 1)
    head_mask = [(lane256 // HEAD_DIM) == h for h in range(N_HEADS)]
    r256 = lax.broadcasted_iota(jnp.int32, (256, 256), 0)
    c256 = lax.broadcasted_iota(jnp.int32, (256, 256), 1)
    same_head = (r256 // HEAD_DIM) == (c256 // HEAD_DIM)
    g_sum = jnp.where(same_head, 1.0, 0.0).astype(BF16)
    g_mean = jnp.where(same_head, 1.0 / HEAD_DIM, 0.0).astype(BF16)
    lane128 = lax.broadcasted_iota(jnp.int32, (1, 128), 1)
    first_half = (lane128 % HEAD_DIM) < (HEAD_DIM // 2)
    lo128 = lane128 < HEAD_DIM
    q_gain = vec_ref[2:3, 0:256]
    k_gain = vec_ref[2:3, 256:384]
    qi = lax.broadcasted_iota(jnp.int32, (4 * WINDOW, 2 * WINDOW), 0) % WINDOW
    kj = lax.broadcasted_iota(jnp.int32, (4 * WINDOW, 2 * WINDOW), 1)
    rel = qi + WINDOW - kj
    band = (rel >= 0) & (rel < WINDOW)
    wb = bconv_ref[...]
    er = lax.broadcasted_iota(jnp.int32, (128, 256), 0)
    ec = lax.broadcasted_iota(jnp.int32, (128, 256), 1) // HEAD_DIM
    e_beta = jnp.where(er == ec, 1.0, 0.0).astype(BF16)
    e_alpha = jnp.where(er == ec + N_HEADS, 1.0, 0.0).astype(BF16)
    tr = lax.broadcasted_iota(jnp.int32, (ts, ts), 0)
    tc = lax.broadcasted_iota(jnp.int32, (ts, ts), 1)
    l_bd = jnp.where(((tr // DN_CHUNK) == (tc // DN_CHUNK)) & (tr >= tc), 1.0, 0.0).astype(BF16)
    ci = lax.broadcasted_iota(jnp.int32, (DN_CHUNK, 256), 0)
    cj = lax.broadcasted_iota(jnp.int32, (DN_CHUNK, 256), 1) % HEAD_DIM
    eye = ci == cj
    tril = ci >= cj
    strict = ci > cj
    same16 = (ci // 16) == (cj // 16)
    same32 = (ci // 32) == (cj // 32)
    eye_f = jnp.where(eye, 1.0, 0.0)
    out_gain = vec_ref[3:4, 0:256]
    wi = lax.broadcasted_iota(jnp.int32, (C_CHUNK, 4 * C_CHUNK), 0)
    wj = lax.broadcasted_iota(jnp.int32, (C_CHUNK, 4 * C_CHUNK), 1) % C_CHUNK
    cw = jnp.where(wi >= wj, cw_ref[...], 0.0).astype(BF16)
    cbias = cbias_ref[...]
    wd = dconv_ref[...]

    def head_mean(v, n=1):
        return _split_lhs_dot(v, g_mean, n)

    def head_sum(v, n=1):
        return _split_lhs_dot(v, g_sum, n)

    def bd(xv):
        xb = xv.astype(BF16)
        zero = jnp.zeros_like(xb)
        return jnp.concatenate([jnp.where(head_mask[h], xb, zero) for h in range(N_HEADS)], axis=0)

    def mm(lhs, xv):
        return _dot(lhs.astype(BF16), bd(xv))

    def rope(v, cos, sin):
        sw = jnp.where(first_half, pltpu.roll(v, 128 - HEAD_DIM // 2, 1), pltpu.roll(v, HEAD_DIM // 2, 1))
        return v * cos + sw * sin

    def in_proj(xv, z_ref):
        ms = jnp.mean(xv * xv, axis=-1, keepdims=True)
        h_ref[...] = (xv * lax.rsqrt(ms + EPS) * vec_ref[0:1, :]).astype(BF16)
        for c0 in range(0, Z_WIDTH, PROJ_CHUNK):
            c1 = min(c0 + PROJ_CHUNK, Z_WIDTH)
            z_ref[HALO:HALO + ts, c0:c1] = _dot(h_ref[...], w_in_ref[:, c0:c1])
            yield

    def run_interleaved(main, side):
        for _ in main:
            next(side, None)
        for _ in side:
            pass

    def mixers(z_ref, tile):
        mix_t, ybuf_t, ysh_t = mix_ref.at[tile], ybuf.at[tile], ysh.at[tile]

        def gate_silu(rows, c0, c1):
            return _silu(z_ref[rows, COL_GATE + c0:COL_GATE + c1])

        for blk in range(ts // WINDOW):
            r0 = HALO + blk * WINDOW
            rows = slice(r0, r0 + WINDOW)
            q = z_ref[rows, COL_AQ:COL_AQ + 256]
            k = z_ref[rows, COL_AK:COL_AK + 128]
            v = z_ref[rows, COL_AV:COL_AV + 128]
            cos = cos_ref[tile * ts + blk * WINDOW:tile * ts + (blk + 1) * WINDOW, :]
            sin = sin_ref[tile * ts + blk * WINDOW:tile * ts + (blk + 1) * WINDOW, :]
            qn = q * lax.rsqrt(head_mean(q * q) + EPS) * q_gain
            kn = k * lax.rsqrt(_split_lhs_dot(k * k, g_mean[0:128, 0:128], 1) + EPS) * k_gain
            qa = rope(qn[:, 0:128], cos, sin)
            qb = rope(qn[:, 128:256], cos, sin)
            kbuf[WINDOW + blk * WINDOW:2 * WINDOW + blk * WINDOW, :] = rope(kn, cos, sin)
            vbuf[WINDOW + blk * WINDOW:2 * WINDOW + blk * WINDOW, :] = v
            k2 = kbuf[blk * WINDOW:(blk + 2) * WINDOW, :].astype(BF16)
            v2 = vbuf[blk * WINDOW:(blk + 2) * WINDOW, :].astype(BF16)
            lhs = jnp.concatenate([jnp.where(lo128, qa, 0.0), jnp.where(lo128, 0.0, qa),
                                   jnp.where(lo128, qb, 0.0), jnp.where(lo128, 0.0, qb)], axis=0).astype(BF16)
            sc = _dot_nt(lhs, k2)
            valid = band
            if blk == 0 and tile == 0:
                valid = valid & (kj >= jnp.where(sp > 0, 0, WINDOW))
            sc = jnp.where(valid, sc, -jnp.inf)
            outs = []
            for r in range(4):
                s_r = sc[r * WINDOW:(r + 1) * WINDOW]
                sink = sink_ref[r]
                m = jnp.maximum(jnp.max(s_r, axis=-1, keepdims=True), sink)
                e = jnp.exp(s_r - m)
                den = jnp.sum(e, axis=-1, keepdims=True) + jnp.exp(sink - m)
                outs.append(_dot(e.astype(BF16), v2) / den)
            o_a0 = jnp.where(lo128, outs[0], outs[1])
            o_a1 = jnp.where(lo128, outs[2], outs[3])
            mrows = slice(blk * WINDOW, (blk + 1) * WINDOW)
            mix_t[mrows, 0:128] = (o_a0 * gate_silu(rows, 0, 128)).astype(BF16)
            mix_t[mrows, 128:256] = (o_a1 * gate_silu(rows, 128, 256)).astype(BF16)
            yield

        acc = None
        for kk in range(B_CONV):
            t = z_ref[HALO - (B_CONV - 1) + kk:HALO - (B_CONV - 1) + kk + ts, COL_B:COL_B + 768] * wb[kk:kk + 1, :]
            acc = t if acc is None else acc + t
            if kk % 2 == 1:
                yield
        qkv = _silu(acc)
        bq = qkv[:, 0:256]
        bk = qkv[:, 256:512]
        bv = qkv[:, 512:768]
        bq = bq * lax.rsqrt(head_sum(bq * bq) + EPS) * (HEAD_DIM ** -0.5)
        bk = bk * lax.rsqrt(head_sum(bk * bk) + EPS)
        yield

        tail = z_ref[HALO:HALO + ts, COL_TAIL:COL_TAIL + 128]
        beta = _sigmoid(tail)
        xg = tail + vec_ref[2:3, 640:768]
        softplus = jnp.maximum(xg, 0.0) + jnp.log1p(jnp.exp(-jnp.abs(xg)))
        gdec = -jnp.exp(vec_ref[2:3, 512:640]) * softplus
        beta_f = _split_lhs_dot(beta, e_beta, 1)
        gdec_f = _split_lhs_dot(gdec, e_alpha, 3)
        gc = _split_rhs_dot(l_bd, gdec_f, 3)

        cs = range(ts // DN_CHUNK)
        sls = [slice(c * DN_CHUNK, (c + 1) * DN_CHUNK) for c in cs]
        gc_c = [gc[s] for s in sls]
        gl = [g[DN_CHUNK - 1:DN_CHUNK, :] for g in gc_c]
        egc = [jnp.exp(g) for g in gc_c]
        ekg = [jnp.exp(gl[c] - gc_c[c]) for c in cs]
        egl = [jnp.exp(g) for g in gl]
        k_c = [bk[s] for s in sls]
        q_c = [bq[s] for s in sls]
        kb = [k_c[c] * beta_f[sls[c]] for c in cs]
        vb = [bv[sls[c]] * beta_f[sls[c]] for c in cs]
        grow = [jnp.sum(jnp.where(eye, g, 0.0), axis=0, keepdims=True) for g in gc_c]
        decay = [jnp.exp(jnp.minimum(gc_c[c] - grow[c], 0.0)) for c in cs]
        aq = [_dot_nt(jnp.concatenate([kb[c], q_c[c]], axis=0).astype(BF16), bd(k_c[c])) for c in cs]
        a_m = [jnp.where(strict, aq[c][0:DN_CHUNK] * decay[c], 0.0) for c in cs]
        qk = [jnp.where(tril, aq[c][DN_CHUNK:2 * DN_CHUNK] * decay[c], 0.0) for c in cs]
        yield

        d1 = [jnp.where(same16, a, 0.0) for a in a_m]
        pm = [eye_f - d for d in d1]
        d2 = [mm(d, d) for d in d1]
        r = [mm(jnp.concatenate([pm[c], d2[c]], axis=0), d2[c]) for c in cs]
        pm = [pm[c] + r[c][0:DN_CHUNK] for c in cs]
        d4 = [r[c][DN_CHUNK:] for c in cs]
        r = [mm(jnp.concatenate([pm[c], d4[c]], axis=0), d4[c]) for c in cs]
        pm = [pm[c] + r[c][0:DN_CHUNK] for c in cs]
        d8 = [r[c][DN_CHUNK:] for c in cs]
        t0 = [pm[c] + mm(pm[c], d8[c]) for c in cs]
        yield
        m1 = [jnp.where(same32 & jnp.logical_not(same16), a, 0.0) for a in a_m]
        r = [mm(t0[c], m1[c]) for c in cs]
        t1 = [t0[c] - mm(r[c], t0[c]) for c in cs]
        m2 = [jnp.where(same32, 0.0, a) for a in a_m]
        r = [mm(t1[c], m2[c]) for c in cs]
        t2 = [(t1[c] - mm(r[c], t1[c])).astype(BF16) for c in cs]
        u_c = [_dot(t2[c], bd(vb[c])) for c in cs]
        w_c = [_dot(t2[c], bd(kb[c] * egc[c])) for c in cs]
        yield

        kg = [(k_c[c] * ekg[c]).astype(BF16) for c in cs]
        nmat = [jnp.where(same_head, _dot_tn(w_c[c].astype(BF16), kg[c]), 0.0).astype(BF16) for c in cs]
        bfull = [jnp.where(same_head, _dot_tn(u_c[c].astype(BF16), kg[c]), 0.0) for c in cs]
        bcomp = [b[0:64] + b[64:128] + b[128:192] + b[192:256] for b in bfull]
        qkb = [q.astype(BF16) for q in qk]
        qt = [(q_c[c] * egc[c] - _dot(qkb[c], bd(w_c[c]))).astype(BF16) for c in cs]
        ou = [_dot(qkb[c], bd(u_c[c])) for c in cs]

        st = st_ref[...]
        o_c = []
        for c in cs:
            o_c.append(_dot_nt(qt[c], bd(st)) + ou[c])
            st = st * egl[c] - _dot(st.astype(BF16), nmat[c]) + bcomp[c]
        st_ref[...] = st
        yield

        for c in cs:
            o_n = o_c[c] * lax.rsqrt(head_mean(o_c[c] * o_c[c]) + EPS) * out_gain
            zrows = slice(HALO + c * DN_CHUNK, HALO + (c + 1) * DN_CHUNK)
            mix_t[sls[c], 256:512] = (o_n * gate_silu(zrows, 256, 512)).astype(BF16)

        uv = z_ref[HALO:HALO + ts, COL_C:COL_C + 512]
        gl_uv = 0.5 * uv * (1.0 + jnp.tanh(0.7978845608028654 * (uv + 0.044715 * (uv * uv * uv))))
        cu = gl_uv[:, 0:256]
        cv = gl_uv[:, 256:512]
        mu = jnp.mean(cv, axis=-1, keepdims=True)
        dv = cv - mu
        var = jnp.mean(dv * dv, axis=-1, keepdims=True)
        cvn = dv * lax.rsqrt(var + EPS) * vec_ref[3:4, 256:512] + vec_ref[3:4, 512:768]
        yield
        for c in range(ts // C_CHUNK):
            sl = slice(c * C_CHUNK, (c + 1) * C_CHUNK)
            v_c = cvn[sl]
            vblk = jnp.concatenate([jnp.where(head_mask[g], v_c, 0.0).astype(BF16) for g in range(4)], axis=0)
            zc = _dot(cw, vblk) + cbias
            zrows = slice(HALO + c * C_CHUNK, HALO + (c + 1) * C_CHUNK)
            mix_t[sl, 512:768] = (cu[sl] * zc * gate_silu(zrows, 512, 768)).astype(BF16)

        ag = z_ref[0:HALO + ts, COL_D:COL_D + 512]
        ybuf_t[...] = ag[:, 0:256] * _sigmoid(ag[:, 256:512])
        acc = None
        for r in range(1, 8):
            ysh_t[r - 1] = ybuf_t[r:r + ts + HALO - 8, :]
            if r % 4 == 0:
                yield
        for kk in range(D_CONV):
            r0 = HALO - (D_CONV - 1) + kk
            a8, r = (r0 // 8) * 8, r0 % 8
            win = ybuf_t[a8:a8 + ts, :] if r == 0 else ysh_t[r - 1, a8:a8 + ts, :]
            t = win * wd[kk:kk + 1, :]
            acc = t if acc is None else acc + t
            if kk % 5 == 4:
                yield
        yc = acc + vec_ref[3:4, 768:1024]
        mu = head_mean(yc)
        dy = yc - mu
        var = head_mean(dy * dy)
        yn = dy * lax.rsqrt(var + EPS) * vec_ref[4:5, 0:256] + vec_ref[4:5, 256:512]
        mix_t[:, 768:1024] = (_silu(yn) * gate_silu(slice(HALO, HALO + ts), 768, 1024)).astype(BF16)

        x = x_ref[0, tile * ts:(tile + 1) * ts, :]
        x1 = x + _dot(mix_t[...], w_out_ref[...])
        ms1 = jnp.mean(x1 * x1, axis=-1, keepdims=True)
        hg = (x1 * lax.rsqrt(ms1 + EPS) * vec_ref[1:2, :]).astype(BF16)
        ple_gate = _sigmoid(_dot(hg, gate_w_ref[...]))
        pe = _dot(p_ref[0, tile * ts:(tile + 1) * ts, :].astype(BF16), ple_w_ref[...])
        o_ref[0, tile * ts:(tile + 1) * ts, :] = x1 + ple_gate * pe

        kbuf[0:WINDOW, :] = kbuf[ts:ts + WINDOW, :]
        vbuf[0:WINDOW, :] = vbuf[ts:ts + WINDOW, :]

    @pl.when(sp == 0)
    def _():
        za_ref[0:HALO, :] = jnp.zeros((HALO, Z_WIDTH), F32)
        kbuf[0:WINDOW, :] = jnp.zeros((WINDOW, 128), F32)
        vbuf[0:WINDOW, :] = jnp.zeros((WINDOW, 128), F32)
        st_ref[...] = jnp.zeros((HEAD_DIM, 256), F32)

    @pl.when(step == 0)
    def _():
        for _ in in_proj(x_ref[0, 0:ts, :], za_ref):
            pass

    zb_ref[0:HALO, :] = za_ref[ts:ts + HALO, :]
    run_interleaved(mixers(za_ref, 0), in_proj(x_ref[0, ts:2 * ts, :], zb_ref))
    za_ref[0:HALO, :] = zb_ref[ts:ts + HALO, :]
    run_interleaved(mixers(zb_ref, 1), in_proj(xn_ref[0], za_ref))


def _const_spec(shape):
    return pl.BlockSpec(shape, lambda g: (0,) * len(shape))


def _layer_call(x, p, cos_t, sin_t, sink, w_in, w_out, ple_w, gate_w, vec, bconv, cw, cbias, dconv, ts):
    bsz, seq, _ = x.shape
    tiles_per_seq = seq // ts
    steps_per_seq = tiles_per_seq // 2
    n_steps = bsz * steps_per_seq
    last_tile = bsz * tiles_per_seq - 1

    def pair_map(g):
        return (g // steps_per_seq, g % steps_per_seq, 0)

    def next_tile_map(g):
        t = jnp.minimum(2 * g + 2, last_tile)
        return (t // tiles_per_seq, t % tiles_per_seq, 0)

    in_specs = [
        pl.BlockSpec(memory_space=pltpu.SMEM),
        pl.BlockSpec((1, 2 * ts, D_MODEL), pair_map),
        pl.BlockSpec((1, ts, D_MODEL), next_tile_map),
        pl.BlockSpec((1, 2 * ts, D_PLE), pair_map),
        pl.BlockSpec((2 * ts, 128), lambda g: (g % steps_per_seq, 0)),
        pl.BlockSpec((2 * ts, 128), lambda g: (g % steps_per_seq, 0)),
        _const_spec(w_in.shape), _const_spec(w_out.shape), _const_spec(ple_w.shape), _const_spec(gate_w.shape),
        _const_spec(vec.shape), _const_spec(bconv.shape), _const_spec(cw.shape), _const_spec(cbias.shape),
        _const_spec(dconv.shape),
    ]
    scratch = [
        pltpu.VMEM((HALO + ts, Z_WIDTH), F32),
        pltpu.VMEM((HALO + ts, Z_WIDTH), F32),
        pltpu.VMEM((ts, D_MODEL), BF16),
        pltpu.VMEM((WINDOW + ts, 128), F32),
        pltpu.VMEM((WINDOW + ts, 128), F32),
        pltpu.VMEM((HEAD_DIM, 256), F32),
        pltpu.VMEM((2, ts, D_MODEL), BF16),
        pltpu.VMEM((2, HALO + ts, GROUP_WIDTH), F32),
        pltpu.VMEM((2, 7, HALO + ts - 8, GROUP_WIDTH), F32),
    ]
    return pl.pallas_call(
        functools.partial(_layer_kernel, ts=ts, steps_per_seq=steps_per_seq),
        grid=(n_steps,),
        in_specs=in_specs,
        out_specs=pl.BlockSpec((1, 2 * ts, D_MODEL), pair_map),
        out_shape=jax.ShapeDtypeStruct(x.shape, F32),
        scratch_shapes=scratch,
        compiler_params=pltpu.CompilerParams(
            dimension_semantics=("arbitrary",),
            vmem_limit_bytes=VMEM_LIMIT_BYTES,
        ),
        name="hybrid_trunk_layer",
    )(sink, x, x, p, cos_t, sin_t, w_in, w_out, ple_w, gate_w, vec, bconv, cw, cbias, dconv)


_HEAD_PERM = (0, 2, 1, 3)


def _perm_cols(w):
    parts = [w[..., h * HEAD_DIM:(h + 1) * HEAD_DIM] for h in _HEAD_PERM]
    return jnp.concatenate(parts, axis=-1)


def _pad_lanes(v, start, width=128):
    out = jnp.zeros((width,), F32)
    return out.at[start:start + v.shape[0]].set(v.astype(F32))


def kernel(x, p, norm_gain, w_in, w_out, a_q_gain, a_k_gain, a_sink, b_conv, b_a_log, b_dt_bias, b_out_gain,
           c_ln_gain, c_ln_bias, c_w_s, c_b_s, d_conv, d_conv_bias, d_ln_gain, d_ln_bias,
           ple_w, ple_gate_norm, ple_gate_w):
    depth = w_in.shape[0]
    bsz, seq, _ = x.shape
    ts = min(SEQ_TILE, seq)
    assert seq % (2 * ts) == 0 and ts % WINDOW == 0

    inv = ROPE_THETA ** (-jnp.arange(0, HEAD_DIM, 2, dtype=F32) / HEAD_DIM)
    ang = jnp.arange(seq, dtype=F32)[:, None] * inv[None, :]
    cos, sin = jnp.cos(ang), jnp.sin(ang)
    cos_t = jnp.concatenate([cos, cos, cos, cos], axis=-1)
    sin_t = jnp.concatenate([-sin, sin, -sin, sin], axis=-1)

    for i in range(depth):
        wi = w_in[i]
        gate_cols = wi[:, 2312:3336]
        w_in_r = jnp.concatenate([
            _perm_cols(wi[:, 0:256]), wi[:, 256:512], wi[:, 512:1280], wi[:, 1288:1800], wi[:, 1800:2312],
            _perm_cols(gate_cols[:, 0:256]), gate_cols[:, 256:],
            wi[:, 1280:1288], jnp.zeros((D_MODEL, 120), F32)], axis=1).astype(BF16)
        wo = w_out[i]
        w_out_r = jnp.concatenate([
            jnp.concatenate([wo[h * HEAD_DIM:(h + 1) * HEAD_DIM] for h in _HEAD_PERM], axis=0), wo[256:]],
            axis=0).astype(BF16)
        sink = a_sink[i][jnp.array(_HEAD_PERM)].astype(F32)
        zeros = jnp.zeros((D_MODEL,), F32)
        row2 = jnp.concatenate([
            jnp.tile(a_q_gain[i] * (HEAD_DIM ** -0.5), 4), jnp.tile(a_k_gain[i], 2), jnp.zeros((128,), F32),
            _pad_lanes(b_a_log[i], N_HEADS), _pad_lanes(b_dt_bias[i], N_HEADS), jnp.zeros((256,), F32)])
        row3 = jnp.concatenate([jnp.tile(b_out_gain[i], 4), c_ln_gain[i], c_ln_bias[i], d_conv_bias[i]])
        row4 = jnp.concatenate([d_ln_gain[i], d_ln_bias[i], jnp.zeros((512,), F32)])
        vec = jnp.stack([norm_gain[i], ple_gate_norm[i], row2, row3, row4, zeros, zeros, zeros]).astype(F32)
        cw = jnp.transpose(c_w_s[i], (1, 0, 2)).reshape(C_CHUNK, 4 * C_CHUNK)
        cbias = jnp.repeat(c_b_s[i].T, GROUP_WIDTH // 4, axis=1)
        x = _layer_call(x, p[i], cos_t, sin_t, sink, w_in_r, w_out_r, ple_w[i].astype(BF16),
                        ple_gate_w[i].astype(BF16), vec, b_conv[i], cw, cbias, d_conv[i], ts)
    return x
```

```python
import functools

import jax
import jax.numpy as jnp
from jax import lax
from jax.experimental import pallas as pl
from jax.experimental.pallas import tpu as pltpu

F32 = jnp.float32
BF16 = jnp.bfloat16

EPS = 1e-6
D_MODEL = 1024
D_PLE = 256
GROUP_WIDTH = 256
HEAD_DIM = 64
N_HEADS = 4
WINDOW = 128
ROPE_THETA = 10000.0
B_CONV = 4
DN_CHUNK = 64
C_CHUNK = 128
D_CONV = 31

COL_AQ, COL_AK, COL_AV = 0, 256, 384
COL_B = 512
COL_C = 1280
COL_D = 1792
COL_GATE = 2304
COL_TAIL = 3328
Z_WIDTH = 3456
HALO = 32

SEQ_TILE = 256
PROJ_CHUNK = 256
VMEM_LIMIT_BYTES = 56 * 1024 * 1024


def _sigmoid(x):
    return 0.5 + 0.5 * jnp.tanh(0.5 * x)


def _silu(x):
    h = 0.5 * x
    return h + h * jnp.tanh(h)


def _dot(a, b):
    return jnp.dot(a, b, preferred_element_type=F32)


def _dot_nt(a, b):
    return lax.dot_general(a, b, (((1,), (1,)), ((), ())), preferred_element_type=F32)


def _dot_tn(a, b):
    return lax.dot_general(a, b, (((0,), (0,)), ((), ())), preferred_element_type=F32)


def _split_lhs_dot(a, b_bf16, n):
    acc = None
    r = a
    for i in range(n):
        hi = r.astype(BF16)
        t = _dot(hi, b_bf16)
        acc = t if acc is None else acc + t
        if i + 1 < n:
            r = r - hi.astype(F32)
    return acc


def _split_rhs_dot(a_bf16, b, n):
    acc = None
    r = b
    for i in range(n):
        hi = r.astype(BF16)
        t = _dot(a_bf16, hi)
        acc = t if acc is None else acc + t
        if i + 1 < n:
            r = r - hi.astype(F32)
    return acc


def _layer_kernel(sink_ref, x_ref, xn_ref, p_ref, cos_ref, sin_ref, w_in_ref, w_out_ref, ple_w_ref, gate_w_ref,
                  vec_ref, bconv_ref, cw_ref, cbias_ref, dconv_ref, o_ref,
                  za_ref, zb_ref, h_ref, kbuf, vbuf, st_ref, mix_ref, ybuf, ysh, *, ts, steps_per_seq, layer):
    step = pl.program_id(0)
    sp = step % steps_per_seq

    lane256 = lax.broadcasted_iota(jnp.int32, (1, 256), 1)
    head_mask = [(lane256 // HEAD_DIM) == h for h in range(N_HEADS)]
    r256 = lax.broadcasted_iota(jnp.int32, (256, 256), 0)
    c256 = lax.broadcasted_iota(jnp.int32, (256, 256), 1)
    same_head = (r256 // HEAD_DIM) == (c256 // HEAD_DIM)
    g_sum = jnp.where(same_head, 1.0, 0.0).astype(BF16)
    g_mean = jnp.where(same_head, 1.0 / HEAD_DIM, 0.0).astype(BF16)
    lane128 = lax.broadcasted_iota(jnp.int32, (1, 128), 1)
    first_half = (lane128 % HEAD_DIM) < (HEAD_DIM // 2)
    lo128 = lane128 < HEAD_DIM
    q_gain = vec_ref[2:3, 0:256]
    k_gain = vec_ref[2:3, 256:384]
    qi = lax.broadcasted_iota(jnp.int32, (4 * WINDOW, 2 * WINDOW), 0) % WINDOW
    kj = lax.broadcasted_iota(jnp.int32, (4 * WINDOW, 2 * WINDOW), 1)
    rel = qi + WINDOW - kj
    band = (rel >= 0) & (rel < WINDOW)
    wb = bconv_ref[...]
    er = lax.broadcasted_iota(jnp.int32, (128, 256), 0)
    ec = lax.broadcasted_iota(jnp.int32, (128, 256), 1) // HEAD_DIM
    e_beta = jnp.where(er == ec, 1.0, 0.0).astype(BF16)
    e_alpha = jnp.where(er == ec + N_HEADS, 1.0, 0.0).astype(BF16)
    tr = lax.broadcasted_iota(jnp.int32, (ts, ts), 0)
    tc = lax.broadcasted_iota(jnp.int32, (ts, ts), 1)
    l_bd = jnp.where(((tr // DN_CHUNK) == (tc // DN_CHUNK)) & (tr >= tc), 1.0, 0.0).astype(BF16)
    ci = lax.broadcasted_iota(jnp.int32, (DN_CHUNK, 256), 0)
    cj = lax.broadcasted_iota(jnp.int32, (DN_CHUNK, 256), 1) % HEAD_DIM
    eye = ci == cj
    tril = ci >= cj
    strict = ci > cj
    same16 = (ci // 16) == (cj // 16)
    same32 = (ci // 32) == (cj // 32)
    eye_f = jnp.where(eye, 1.0, 0.0)
    out_gain = vec_ref[3:4, 0:256]
    wi = lax.broadcasted_iota(jnp.int32, (C_CHUNK, 4 * C_CHUNK), 0)
    wj = lax.broadcasted_iota(jnp.int32, (C_CHUNK, 4 * C_CHUNK), 1) % C_CHUNK
    cw = jnp.where(wi >= wj, cw_ref[...], 0.0).astype(BF16)
    cbias = cbias_ref[...]
    wd = dconv_ref[...]

    def head_mean(v, n=1):
        return _split_lhs_dot(v, g_mean, n)

    def head_sum(v, n=1):
        return _split_lhs_dot(v, g_sum, n)

    def bd(xv):
        xb = xv.astype(BF16)
        zero = jnp.zeros_like(xb)
        return jnp.concatenate([jnp.where(head_mask[h], xb, zero) for h in range(N_HEADS)], axis=0)

    def mm(lhs, xv):
        return _dot(lhs.astype(BF16), bd(xv))

    def rope(v, cos, sin):
        sw = jnp.where(first_half, pltpu.roll(v, 128 - HEAD_DIM // 2, 1), pltpu.roll(v, HEAD_DIM // 2, 1))
        return v * cos + sw * sin

    def in_proj(xv, z_ref):
        ms = jnp.mean(xv * xv, axis=-1, keepdims=True)
        h_ref[...] = (xv * lax.rsqrt(ms + EPS) * vec_ref[0:1, :]).astype(BF16)
        for c0 in range(0, Z_WIDTH, PROJ_CHUNK):
            c1 = min(c0 + PROJ_CHUNK, Z_WIDTH)
            z_ref[HALO:HALO + ts, c0:c1] = _dot(h_ref[...], w_in_ref[:, c0:c1])
            yield

    def run_interleaved(main, side):
        for _ in main:
            next(side, None)
        for _ in side:
            pass

    def mixers(z_ref, tile):
        mix_t, ybuf_t, ysh_t = mix_ref.at[tile], ybuf.at[tile], ysh.at[tile]

        def gate_silu(rows, c0, c1):
            return _silu(z_ref[rows, COL_GATE + c0:COL_GATE + c1])

        for blk in range(ts // WINDOW):
            r0 = HALO + blk * WINDOW
            rows = slice(r0, r0 + WINDOW)
            q = z_ref[rows, COL_AQ:COL_AQ + 256]
            k = z_ref[rows, COL_AK:COL_AK + 128]
            v = z_ref[rows, COL_AV:COL_AV + 128]
            cos = cos_ref[tile * ts + blk * WINDOW:tile * ts + (blk + 1) * WINDOW, :]
            sin = sin_ref[tile * ts + blk * WINDOW:tile * ts + (blk + 1) * WINDOW, :]
            qn = q * lax.rsqrt(head_mean(q * q) + EPS) * q_gain
            kn = k * lax.rsqrt(_split_lhs_dot(k * k, g_mean[0:128, 0:128], 1) + EPS) * k_gain
            qa = rope(qn[:, 0:128], cos, sin)
            qb = rope(qn[:, 128:256], cos, sin)
            kr = rope(kn, cos, sin)
            cur = slice(WINDOW + blk * WINDOW, 2 * WINDOW + blk * WINDOW)
            both = slice(blk * WINDOW, (blk + 2) * WINDOW)
            kbuf[0, cur, :] = kr
            kbuf[1, cur, :] = pltpu.roll(kr, HEAD_DIM, 1)
            vbuf[0, cur, :] = v
            vbuf[1, cur, :] = pltpu.roll(v, HEAD_DIM, 1)
            k2 = [kbuf[i, both, :].astype(BF16) for i in range(2)]
            v2 = [vbuf[i, both, :].astype(BF16) for i in range(2)]
            lhs0 = jnp.concatenate([jnp.where(lo128, qa, 0.0), jnp.where(lo128, 0.0, qb)], axis=0).astype(BF16)
            lhs1 = jnp.concatenate([jnp.where(lo128, 0.0, qa), jnp.where(lo128, qb, 0.0)], axis=0).astype(BF16)
            sc0 = _dot_nt(lhs0, k2[0])
            sc1 = _dot_nt(lhs1, k2[1])
            sc = jnp.concatenate([sc0[0:WINDOW], sc1, sc0[WINDOW:]], axis=0)
            valid = band
            if blk == 0 and tile == 0:
                valid = valid & (kj >= jnp.where(sp > 0, 0, WINDOW))
            sc = jnp.where(valid, sc, -jnp.inf)
            outs = []
            for r in range(4):
                s_r = sc[r * WINDOW:(r + 1) * WINDOW]
                sink = sink_ref[layer, r]
                m = jnp.maximum(jnp.max(s_r, axis=-1, keepdims=True), sink)
                e = jnp.exp(s_r - m)
                den = jnp.sum(e, axis=-1, keepdims=True) + jnp.exp(sink - m)
                outs.append(_dot(e.astype(BF16), v2[0 if r in (0, 3) else 1]) / den)
            o_a0 = jnp.where(lo128, outs[0], outs[1])
            o_a1 = jnp.where(lo128, outs[2], outs[3])
            mrows = slice(blk * WINDOW, (blk + 1) * WINDOW)
            mix_t[mrows, 0:128] = (o_a0 * gate_silu(rows, 0, 128)).astype(BF16)
            mix_t[mrows, 128:256] = (o_a1 * gate_silu(rows, 128, 256)).astype(BF16)
            yield

        acc = None
        for kk in range(B_CONV):
            t = z_ref[HALO - (B_CONV - 1) + kk:HALO - (B_CONV - 1) + kk + ts, COL_B:COL_B + 768] * wb[kk:kk + 1, :]
            acc = t if acc is None else acc + t
            if kk % 2 == 1:
                yield
        qkv = _silu(acc)
        bq = qkv[:, 0:256]
        bk = qkv[:, 256:512]
        bv = qkv[:, 512:768]
        bq = bq * lax.rsqrt(head_sum(bq * bq) + EPS) * (HEAD_DIM ** -0.5)
        bk = bk * lax.rsqrt(head_sum(bk * bk) + EPS)
        yield

        tail = z_ref[HALO:HALO + ts, COL_TAIL:COL_TAIL + 128]
        beta = _sigmoid(tail)
        xg = tail + vec_ref[2:3, 640:768]
        softplus = jnp.maximum(xg, 0.0) + jnp.log1p(jnp.exp(-jnp.abs(xg)))
        gdec = -jnp.exp(vec_ref[2:3, 512:640]) * softplus
        beta_f = _split_lhs_dot(beta, e_beta, 1)
        gdec_f = _split_lhs_dot(gdec, e_alpha, 3)
        gc = _split_rhs_dot(l_bd, gdec_f, 3)

        cs = range(ts // DN_CHUNK)
        sls = [slice(c * DN_CHUNK, (c + 1) * DN_CHUNK) for c in cs]
        gc_c = [gc[s] for s in sls]
        gl = [g[DN_CHUNK - 1:DN_CHUNK, :] for g in gc_c]
        egc = [jnp.exp(g) for g in gc_c]
        ekg = [jnp.exp(gl[c] - gc_c[c]) for c in cs]
        egl = [jnp.exp(g) for g in gl]
        k_c = [bk[s] for s in sls]
        q_c = [bq[s] for s in sls]
        kb = [k_c[c] * beta_f[sls[c]] for c in cs]
        vb = [bv[sls[c]] * beta_f[sls[c]] for c in cs]
        grow = [jnp.sum(jnp.where(eye, g, 0.0), axis=0, keepdims=True) for g in gc_c]
        decay = [jnp.exp(jnp.minimum(gc_c[c] - grow[c], 0.0)) for c in cs]
        aq = [_dot_nt(jnp.concatenate([kb[c], q_c[c]], axis=0).astype(BF16), bd(k_c[c])) for c in cs]
        a_m = [jnp.where(strict, aq[c][0:DN_CHUNK] * decay[c], 0.0) for c in cs]
        qk = [jnp.where(tril, aq[c][DN_CHUNK:2 * DN_CHUNK] * decay[c], 0.0) for c in cs]
        yield

        d1 = [jnp.where(same16, a, 0.0) for a in a_m]
        pm = [eye_f - d for d in d1]
        d2 = [mm(d, d) for d in d1]
        r = [mm(jnp.concatenate([pm[c], d2[c]], axis=0), d2[c]) for c in cs]
        pm = [pm[c] + r[c][0:DN_CHUNK] for c in cs]
        d4 = [r[c][DN_CHUNK:] for c in cs]
        r = [mm(jnp.concatenate([pm[c], d4[c]], axis=0), d4[c]) for c in cs]
        pm = [pm[c] + r[c][0:DN_CHUNK] for c in cs]
        d8 = [r[c][DN_CHUNK:] for c in cs]
        t0 = [pm[c] + mm(pm[c], d8[c]) for c in cs]
        yield
        m1 = [jnp.where(same32 & jnp.logical_not(same16), a, 0.0) for a in a_m]
        r = [mm(t0[c], m1[c]) for c in cs]
        t1 = [t0[c] - mm(r[c], t0[c]) for c in cs]
        m2 = [jnp.where(same32, 0.0, a) for a in a_m]
        r = [mm(t1[c], m2[c]) for c in cs]
        t2 = [(t1[c] - mm(r[c], t1[c])).astype(BF16) for c in cs]
        u_c = [_dot(t2[c], bd(vb[c])) for c in cs]
        w_c = [_dot(t2[c], bd(kb[c] * egc[c])) for c in cs]
        yield

        kg = [(k_c[c] * ekg[c]).astype(BF16) for c in cs]
        nmat = [jnp.where(same_head, _dot_tn(w_c[c].astype(BF16), kg[c]), 0.0).astype(BF16) for c in cs]
        bfull = [jnp.where(same_head, _dot_tn(u_c[c].astype(BF16), kg[c]), 0.0) for c in cs]
        bcomp = [b[0:64] + b[64:128] + b[128:192] + b[192:256] for b in bfull]
        qkb = [q.astype(BF16) for q in qk]
        qt = [(q_c[c] * egc[c] - _dot(qkb[c], bd(w_c[c]))).astype(BF16) for c in cs]
        ou = [_dot(qkb[c], bd(u_c[c])) for c in cs]

        st = st_ref[...]
        o_c = []
        for c in cs:
            o_c.append(_dot_nt(qt[c], bd(st)) + ou[c])
            st = st * egl[c] - _dot(st.astype(BF16), nmat[c]) + bcomp[c]
        st_ref[...] = st
        yield

        for c in cs:
            o_n = o_c[c] * lax.rsqrt(head_mean(o_c[c] * o_c[c]) + EPS) * out_gain
            zrows = slice(HALO + c * DN_CHUNK, HALO + (c + 1) * DN_CHUNK)
            mix_t[sls[c], 256:512] = (o_n * gate_silu(zrows, 256, 512)).astype(BF16)

        uv = z_ref[HALO:HALO + ts, COL_C:COL_C + 512]
        gl_uv = 0.5 * uv * (1.0 + jnp.tanh(0.7978845608028654 * (uv + 0.044715 * (uv * uv * uv))))
        cu = gl_uv[:, 0:256]
        cv = gl_uv[:, 256:512]
        mu = jnp.mean(cv, axis=-1, keepdims=True)
        dv = cv - mu
        var = jnp.mean(dv * dv, axis=-1, keepdims=True)
        cvn = dv * lax.rsqrt(var + EPS) * vec_ref[3:4, 256:512] + vec_ref[3:4, 512:768]
        yield
        for c in range(ts // C_CHUNK):
            sl = slice(c * C_CHUNK, (c + 1) * C_CHUNK)
            v_c = cvn[sl]
            vblk = jnp.concatenate([jnp.where(head_mask[g], v_c, 0.0).astype(BF16) for g in range(4)], axis=0)
            zc = _dot(cw, vblk) + cbias
            zrows = slice(HALO + c * C_CHUNK, HALO + (c + 1) * C_CHUNK)
            mix_t[sl, 512:768] = (cu[sl] * zc * gate_silu(zrows, 512, 768)).astype(BF16)

        ag = z_ref[0:HALO + ts, COL_D:COL_D + 512]
        ybuf_t[...] = ag[:, 0:256] * _sigmoid(ag[:, 256:512])
        acc = None
        for r in range(1, 8):
            ysh_t[r - 1] = ybuf_t[r:r + ts + HALO - 8, :]
            if r % 4 == 0:
                yield
        for kk in range(D_CONV):
            r0 = HALO - (D_CONV - 1) + kk
            a8, r = (r0 // 8) * 8, r0 % 8
            win = ybuf_t[a8:a8 + ts, :] if r == 0 else ysh_t[r - 1, a8:a8 + ts, :]
            t = win * wd[kk:kk + 1, :]
            acc = t if acc is None else acc + t
            if kk % 5 == 4:
                yield
        yc = acc + vec_ref[3:4, 768:1024]
        mu = head_mean(yc)
        dy = yc - mu
        var = head_mean(dy * dy)
        yn = dy * lax.rsqrt(var + EPS) * vec_ref[4:5, 0:256] + vec_ref[4:5, 256:512]
        mix_t[:, 768:1024] = (_silu(yn) * gate_silu(slice(HALO, HALO + ts), 768, 1024)).astype(BF16)

        x = x_ref[0, tile * ts:(tile + 1) * ts, :]
        x1 = x + _dot(mix_t[...], w_out_ref[...])
        ms1 = jnp.mean(x1 * x1, axis=-1, keepdims=True)
        hg = (x1 * lax.rsqrt(ms1 + EPS) * vec_ref[1:2, :]).astype(BF16)
        ple_gate = _sigmoid(_dot(hg, gate_w_ref[...]))
        pe = _dot(p_ref[0, tile * ts:(tile + 1) * ts, :].astype(BF16), ple_w_ref[...])
        o_ref[0, tile * ts:(tile + 1) * ts, :] = x1 + ple_gate * pe

        kbuf[:, 0:WINDOW, :] = kbuf[:, ts:ts + WINDOW, :]
        vbuf[:, 0:WINDOW, :] = vbuf[:, ts:ts + WINDOW, :]

    @pl.when(sp == 0)
    def _():
        za_ref[0:HALO, :] = jnp.zeros((HALO, Z_WIDTH), F32)
        kbuf[:, 0:WINDOW, :] = jnp.zeros((2, WINDOW, 128), F32)
        vbuf[:, 0:WINDOW, :] = jnp.zeros((2, WINDOW, 128), F32)
        st_ref[...] = jnp.zeros((HEAD_DIM, 256), F32)

    @pl.when(step == 0)
    def _():
        for _ in in_proj(x_ref[0, 0:ts, :], za_ref):
            pass

    zb_ref[0:HALO, :] = za_ref[ts:ts + HALO, :]
    run_interleaved(mixers(za_ref, 0), in_proj(x_ref[0, ts:2 * ts, :], zb_ref))
    za_ref[0:HALO, :] = zb_ref[ts:ts + HALO, :]
    run_interleaved(mixers(zb_ref, 1), in_proj(xn_ref[0], za_ref))


def _layer_spec(arr, layer):
    return pl.BlockSpec((None,) + arr.shape[1:], lambda g: (layer,) + (0,) * (arr.ndim - 1),
                        pipeline_mode=pl.Buffered(1))


def _layer_call(x, p, layer, cos_t, sin_t, sink, w_in, w_out, ple_w, gate_w, vec, bconv, cw, cbias, dconv, ts):
    bsz, seq, _ = x.shape
    tiles_per_seq = seq // ts
    steps_per_seq = tiles_per_seq // 2
    n_steps = bsz * steps_per_seq
    last_tile = bsz * tiles_per_seq - 1

    def pair_map(g):
        return (g // steps_per_seq, g % steps_per_seq, 0)

    def next_tile_map(g):
        t = jnp.minimum(2 * g + 2, last_tile)
        return (t // tiles_per_seq, t % tiles_per_seq, 0)

    in_specs = [
        pl.BlockSpec(memory_space=pltpu.SMEM),
        pl.BlockSpec((1, 2 * ts, D_MODEL), pair_map),
        pl.BlockSpec((1, ts, D_MODEL), next_tile_map),
        pl.BlockSpec((None, 1, 2 * ts, D_PLE), lambda g: (layer,) + pair_map(g)),
        pl.BlockSpec((2 * ts, 128), lambda g: (g % steps_per_seq, 0)),
        pl.BlockSpec((2 * ts, 128), lambda g: (g % steps_per_seq, 0)),
    ] + [_layer_spec(a, layer) for a in (w_in, w_out, ple_w, gate_w, vec, bconv, cw, cbias, dconv)]
    scratch = [
        pltpu.VMEM((HALO + ts, Z_WIDTH), F32),
        pltpu.VMEM((HALO + ts, Z_WIDTH), F32),
        pltpu.VMEM((ts, D_MODEL), BF16),
        pltpu.VMEM((2, WINDOW + ts, 128), F32),
        pltpu.VMEM((2, WINDOW + ts, 128), F32),
        pltpu.VMEM((HEAD_DIM, 256), F32),
        pltpu.VMEM((2, ts, D_MODEL), BF16),
        pltpu.VMEM((2, HALO + ts, GROUP_WIDTH), F32),
        pltpu.VMEM((2, 7, HALO + ts - 8, GROUP_WIDTH), F32),
    ]
    return pl.pallas_call(
        functools.partial(_layer_kernel, ts=ts, steps_per_seq=steps_per_seq, layer=layer),
        grid=(n_steps,),
        in_specs=in_specs,
        out_specs=pl.BlockSpec((1, 2 * ts, D_MODEL), pair_map),
        out_shape=jax.ShapeDtypeStruct(x.shape, F32),
        scratch_shapes=scratch,
        compiler_params=pltpu.CompilerParams(
            dimension_semantics=("arbitrary",),
            vmem_limit_bytes=VMEM_LIMIT_BYTES,
        ),
        name="hybrid_trunk_layer",
    )(sink, x, x, p, cos_t, sin_t, w_in, w_out, ple_w, gate_w, vec, bconv, cw, cbias, dconv)


def _lanes(v, start, width):
    return jnp.pad(v.astype(F32), ((0, 0), (start, width - start - v.shape[1])))


def kernel(x, p, norm_gain, w_in, w_out, a_q_gain, a_k_gain, a_sink, b_conv, b_a_log, b_dt_bias, b_out_gain,
           c_ln_gain, c_ln_bias, c_w_s, c_b_s, d_conv, d_conv_bias, d_ln_gain, d_ln_bias,
           ple_w, ple_gate_norm, ple_gate_w):
    depth = w_in.shape[0]
    bsz, seq, _ = x.shape
    ts = min(SEQ_TILE, seq)
    assert seq % (2 * ts) == 0 and ts % WINDOW == 0

    inv = ROPE_THETA ** (-jnp.arange(0, HEAD_DIM, 2, dtype=F32) / HEAD_DIM)
    ang = jnp.arange(seq, dtype=F32)[:, None] * inv[None, :]
    cos, sin = jnp.cos(ang), jnp.sin(ang)
    cos_t = jnp.concatenate([cos, cos, cos, cos], axis=-1)
    sin_t = jnp.concatenate([-sin, sin, -sin, sin], axis=-1)

    wb = w_in.astype(BF16)
    w_in_r = jnp.concatenate([wb[:, :, 0:1280], wb[:, :, 1288:3336], wb[:, :, 1280:1288],
                              jnp.zeros((depth, D_MODEL, 120), BF16)], axis=2)
    zeros = jnp.zeros((depth, D_MODEL), F32)
    row2 = jnp.concatenate([
        jnp.tile(a_q_gain * (HEAD_DIM ** -0.5), (1, 4)), jnp.tile(a_k_gain, (1, 2)), jnp.zeros((depth, 128), F32),
        _lanes(b_a_log, N_HEADS, 128), _lanes(b_dt_bias, N_HEADS, 128), jnp.zeros((depth, 256), F32)], axis=1)
    row3 = jnp.concatenate([jnp.tile(b_out_gain, (1, 4)), c_ln_gain, c_ln_bias, d_conv_bias], axis=1)
    row4 = jnp.concatenate([d_ln_gain, d_ln_bias, jnp.zeros((depth, 512), F32)], axis=1)
    vec = jnp.stack([norm_gain, ple_gate_norm, row2, row3, row4, zeros, zeros, zeros], axis=1).astype(F32)
    cw = jnp.transpose(c_w_s, (0, 2, 1, 3)).reshape(depth, C_CHUNK, 4 * C_CHUNK)
    cbias = jnp.repeat(jnp.swapaxes(c_b_s, 1, 2), GROUP_WIDTH // 4, axis=2)
    w_out_b, ple_w_b, gate_w_b = w_out.astype(BF16), ple_w.astype(BF16), ple_gate_w.astype(BF16)
    sink = a_sink.astype(F32)

    for i in range(depth):
        x = _layer_call(x, p, i, cos_t, sin_t, sink, w_in_r, w_out_b, ple_w_b, gate_w_b, vec, b_conv, cw, cbias,
                        d_conv, ts)
    return x
```

```python
import functools

import jax
import jax.numpy as jnp
from jax import lax
from jax.experimental import pallas as pl
from jax.experimental.pallas import tpu as pltpu

F32 = jnp.float32
BF16 = jnp.bfloat16

EPS = 1e-6
D_MODEL = 1024
D_PLE = 256
GROUP_WIDTH = 256
HEAD_DIM = 64
N_HEADS = 4
WINDOW = 128
ROPE_THETA = 10000.0
B_CONV = 4
DN_CHUNK = 64
C_CHUNK = 128
D_CONV = 31

COL_AQ, COL_AK, COL_AV = 0, 256, 384
COL_B = 512
COL_C = 1280
COL_D = 1792
COL_GATE = 2304
COL_TAIL = 3328
Z_WIDTH = 3456
W_IN_SPLIT = 1280
HALO = 32

SEQ_TILE = 256
PROJ_CHUNK = 256
CONV_ROWS = 32
VMEM_LIMIT_BYTES = 56 * 1024 * 1024


def _sigmoid(x):
    return 0.5 + 0.5 * jnp.tanh(0.5 * x)


def _silu(x):
    h = 0.5 * x
    return h + h * jnp.tanh(h)


def _dot(a, b):
    return jnp.dot(a, b, preferred_element_type=F32)


def _dot_nt(a, b):
    return lax.dot_general(a, b, (((1,), (1,)), ((), ())), preferred_element_type=F32)


def _dot_tn(a, b):
    return lax.dot_general(a, b, (((0,), (0,)), ((), ())), preferred_element_type=F32)


def _split_lhs_dot(a, b_bf16, n):
    acc = None
    r = a
    for i in range(n):
        hi = r.astype(BF16)
        t = _dot(hi, b_bf16)
        acc = t if acc is None else acc + t
        if i + 1 < n:
            r = r - hi.astype(F32)
    return acc


def _split_rhs_dot(a_bf16, b, n):
    acc = None
    r = b
    for i in range(n):
        hi = r.astype(BF16)
        t = _dot(a_bf16, hi)
        acc = t if acc is None else acc + t
        if i + 1 < n:
            r = r - hi.astype(F32)
    return acc


def _run_streams(streams):
    live = list(streams)
    rnd = 0
    while live:
        for item in list(live):
            gen, period = item
            if rnd % period == 0 and next(gen, StopIteration) is StopIteration:
                live.remove(item)
        rnd += 1


def _layer_kernel(sink_ref, x_ref, xn_ref, p_ref, cos_ref, sin_ref, w1_ref, w2_ref, w3_ref, w_out_ref, ple_w_ref,
                  gate_w_ref, vec_ref, bconv_ref, cw_ref, cbias_ref, dconv_ref, o_ref,
                  za_ref, zb_ref, h_ref, kbuf, vbuf, st_ref, mix_ref, ybuf, ysh, yc_ref, x1_ref, hg_ref,
                  *, ts, steps_per_seq, layer):
    step = pl.program_id(0)
    sp = step % steps_per_seq

    lane256 = lax.broadcasted_iota(jnp.int32, (1, 256), 1)
    head_mask = [(lane256 // HEAD_DIM) == h for h in range(N_HEADS)]
    r256 = lax.broadcasted_iota(jnp.int32, (256, 256), 0)
    c256 = lax.broadcasted_iota(jnp.int32, (256, 256), 1)
    same_head = (r256 // HEAD_DIM) == (c256 // HEAD_DIM)
    g_sum = jnp.where(same_head, 1.0, 0.0).astype(BF16)
    g_mean = jnp.where(same_head, 1.0 / HEAD_DIM, 0.0).astype(BF16)
    lane128 = lax.broadcasted_iota(jnp.int32, (1, 128), 1)
    first_half = (lane128 % HEAD_DIM) < (HEAD_DIM // 2)
    lo128 = lane128 < HEAD_DIM
    q_gain = vec_ref[2:3, 0:256]
    k_gain = vec_ref[2:3, 256:384]
    qi = lax.broadcasted_iota(jnp.int32, (WINDOW, 2 * WINDOW), 0)
    kj = lax.broadcasted_iota(jnp.int32, (WINDOW, 2 * WINDOW), 1)
    rel = qi + WINDOW - kj
    band = (rel >= 0) & (rel < WINDOW)
    wb = bconv_ref[...]
    er = lax.broadcasted_iota(jnp.int32, (128, 256), 0)
    ec = lax.broadcasted_iota(jnp.int32, (128, 256), 1) // HEAD_DIM
    e_beta = jnp.where(er == ec, 1.0, 0.0).astype(BF16)
    e_alpha = jnp.where(er == ec + N_HEADS, 1.0, 0.0).astype(BF16)
    tr = lax.broadcasted_iota(jnp.int32, (ts, ts), 0)
    tc = lax.broadcasted_iota(jnp.int32, (ts, ts), 1)
    l_bd = jnp.where(((tr // DN_CHUNK) == (tc // DN_CHUNK)) & (tr >= tc), 1.0, 0.0).astype(BF16)
    ci = lax.broadcasted_iota(jnp.int32, (DN_CHUNK, 256), 0)
    cj = lax.broadcasted_iota(jnp.int32, (DN_CHUNK, 256), 1) % HEAD_DIM
    eye = ci == cj
    tril = ci >= cj
    strict = ci > cj
    same16 = (ci // 16) == (cj // 16)
    same32 = (ci // 32) == (cj // 32)
    eye_f = jnp.where(eye, 1.0, 0.0)
    out_gain = vec_ref[3:4, 0:256]
    wi = lax.broadcasted_iota(jnp.int32, (C_CHUNK, 4 * C_CHUNK), 0)
    wj = lax.broadcasted_iota(jnp.int32, (C_CHUNK, 4 * C_CHUNK), 1) % C_CHUNK
    cw = jnp.where(wi >= wj, cw_ref[...], 0.0).astype(BF16)
    cbias = cbias_ref[...]
    wd = dconv_ref[...]

    def head_mean(v, n=1):
        return _split_lhs_dot(v, g_mean, n)

    def head_sum(v, n=1):
        return _split_lhs_dot(v, g_sum, n)

    def bd(xv):
        xb = xv.astype(BF16)
        zero = jnp.zeros_like(xb)
        return jnp.concatenate([jnp.where(head_mask[h], xb, zero) for h in range(N_HEADS)], axis=0)

    def mm(lhs, xv):
        return _dot(lhs.astype(BF16), bd(xv))

    def rope(v, cos, sin):
        sw = jnp.where(first_half, pltpu.roll(v, 128 - HEAD_DIM // 2, 1), pltpu.roll(v, HEAD_DIM // 2, 1))
        return v * cos + sw * sin

    proj_chunks = ([(w1_ref, c, c, PROJ_CHUNK) for c in range(0, W_IN_SPLIT, PROJ_CHUNK)]
                   + [(w2_ref, c, W_IN_SPLIT + c, PROJ_CHUNK) for c in range(0, COL_TAIL - W_IN_SPLIT, PROJ_CHUNK)]
                   + [(w3_ref, 0, COL_TAIL, Z_WIDTH - COL_TAIL)])

    def in_proj(x_rows, z_ref):
        for r0 in range(0, ts, 64):
            xv = x_rows(r0, 64)
            ms = jnp.mean(xv * xv, axis=-1, keepdims=True)
            h_ref[r0:r0 + 64, :] = (xv * lax.rsqrt(ms + EPS) * vec_ref[0:1, :]).astype(BF16)
        for w_ref, wc, zc, width in proj_chunks:
            z_ref[HALO:HALO + ts, zc:zc + width] = _dot(h_ref[...], w_ref[:, wc:wc + width])
            yield

    def gate_silu(z_ref, rows, c0, c1):
        return _silu(z_ref[rows, COL_GATE + c0:COL_GATE + c1])

    def attention(z_ref, tile):
        mix_t = mix_ref.at[tile]
        for blk in range(ts // WINDOW):
            rows = slice(HALO + blk * WINDOW, HALO + (blk + 1) * WINDOW)
            q = z_ref[rows, COL_AQ:COL_AQ + 256]
            k = z_ref[rows, COL_AK:COL_AK + 128]
            v = z_ref[rows, COL_AV:COL_AV + 128]
            cos = cos_ref[tile * ts + blk * WINDOW:tile * ts + (blk + 1) * WINDOW, :]
            sin = sin_ref[tile * ts + blk * WINDOW:tile * ts + (blk + 1) * WINDOW, :]
            qn = q * lax.rsqrt(head_mean(q * q) + EPS) * q_gain
            kn = k * lax.rsqrt(_split_lhs_dot(k * k, g_mean[0:128, 0:128], 1) + EPS) * k_gain
            qa = rope(qn[:, 0:128], cos, sin)
            qb = rope(qn[:, 128:256], cos, sin)
            kr = rope(kn, cos, sin)
            cur = slice(WINDOW + blk * WINDOW, 2 * WINDOW + blk * WINDOW)
            both = slice(blk * WINDOW, (blk + 2) * WINDOW)
            kbuf[0, cur, :] = kr.astype(BF16)
            kbuf[1, cur, :] = pltpu.roll(kr, HEAD_DIM, 1).astype(BF16)
            vbuf[0, cur, :] = v.astype(BF16)
            vbuf[1, cur, :] = pltpu.roll(v, HEAD_DIM, 1).astype(BF16)
            lhs = [jnp.where(lo128, qa, 0.0).astype(BF16), jnp.where(lo128, 0.0, qa).astype(BF16),
                   jnp.where(lo128, qb, 0.0).astype(BF16), jnp.where(lo128, 0.0, qb).astype(BF16)]
            yield
            valid = band
            if blk == 0 and tile == 0:
                valid = valid & (kj >= jnp.where(sp > 0, 0, WINDOW))
            outs = []
            for r in range(N_HEADS):
                slot = 0 if r in (0, 3) else 1
                sc = jnp.where(valid, _dot_nt(lhs[r], kbuf[slot, both, :]), -jnp.inf)
                sink = sink_ref[layer, r]
                m = jnp.maximum(jnp.max(sc, axis=-1, keepdims=True), sink)
                e = jnp.exp(sc - m)
                den = jnp.sum(e, axis=-1, keepdims=True) + jnp.exp(sink - m)
                outs.append(_dot(e.astype(BF16), vbuf[slot, both, :]) / den)
                yield
            o_a0 = jnp.where(lo128, outs[0], outs[1])
            o_a1 = jnp.where(lo128, outs[2], outs[3])
            mrows = slice(blk * WINDOW, (blk + 1) * WINDOW)
            mix_t[mrows, 0:128] = (o_a0 * gate_silu(z_ref, rows, 0, 128)).astype(BF16)
            mix_t[mrows, 128:256] = (o_a1 * gate_silu(z_ref, rows, 128, 256)).astype(BF16)
            yield
        kbuf[:, 0:WINDOW, :] = kbuf[:, ts:ts + WINDOW, :]
        vbuf[:, 0:WINDOW, :] = vbuf[:, ts:ts + WINDOW, :]

    def deltanet(z_ref, tile):
        mix_t = mix_ref.at[tile]
        cs = range(ts // DN_CHUNK)
        sls = [slice(c * DN_CHUNK, (c + 1) * DN_CHUNK) for c in cs]
        q_c, k_c, v_c = [], [], []
        for c in cs:
            r0 = HALO + c * DN_CHUNK - (B_CONV - 1)
            acc = None
            for kk in range(B_CONV):
                term = z_ref[r0 + kk:r0 + kk + DN_CHUNK, COL_B:COL_B + 768] * wb[kk:kk + 1, :]
                acc = term if acc is None else acc + term
            qkv = _silu(acc)
            q, k = qkv[:, 0:256], qkv[:, 256:512]
            ss = head_sum(jnp.concatenate([q * q, k * k], axis=0))
            q_c.append(q * lax.rsqrt(ss[0:DN_CHUNK] + EPS) * (HEAD_DIM ** -0.5))
            k_c.append(k * lax.rsqrt(ss[DN_CHUNK:] + EPS))
            v_c.append(qkv[:, 512:768])
            yield

        tail = z_ref[HALO:HALO + ts, COL_TAIL:COL_TAIL + 128]
        beta = _sigmoid(tail)
        xg = tail + vec_ref[2:3, 640:768]
        softplus = jnp.maximum(xg, 0.0) + jnp.log1p(jnp.exp(-jnp.abs(xg)))
        gdec = -jnp.exp(vec_ref[2:3, 512:640]) * softplus
        beta_f = _split_lhs_dot(beta, e_beta, 1)
        gdec_f = _split_lhs_dot(gdec, e_alpha, 2)
        gc = _split_rhs_dot(l_bd, gdec_f, 2)
        yield

        gc_c = [gc[s] for s in sls]
        gl = [g[DN_CHUNK - 1:DN_CHUNK, :] for g in gc_c]
        egc = [jnp.exp(g) for g in gc_c]
        ekg = [jnp.exp(gl[c] - gc_c[c]) for c in cs]
        egl = [jnp.exp(g) for g in gl]
        kb = [k_c[c] * beta_f[sls[c]] for c in cs]
        vb = [v_c[c] * beta_f[sls[c]] for c in cs]
        grow = [jnp.sum(jnp.where(eye, g, 0.0), axis=0, keepdims=True) for g in gc_c]
        decay = [jnp.exp(jnp.minimum(gc_c[c] - grow[c], 0.0)) for c in cs]
        yield
        aq = [_dot_nt(jnp.concatenate([kb[c], q_c[c]], axis=0).astype(BF16), bd(k_c[c])) for c in cs]
        a_m = [jnp.where(strict, aq[c][0:DN_CHUNK] * decay[c], 0.0) for c in cs]
        qk = [jnp.where(tril, aq[c][DN_CHUNK:2 * DN_CHUNK] * decay[c], 0.0) for c in cs]
        yield

        d1 = [jnp.where(same16, a, 0.0) for a in a_m]
        pm = [eye_f - d for d in d1]
        d2 = [mm(d, d) for d in d1]
        yield
        r = [mm(jnp.concatenate([pm[c], d2[c]], axis=0), d2[c]) for c in cs]
        pm = [pm[c] + r[c][0:DN_CHUNK] for c in cs]
        d4 = [r[c][DN_CHUNK:] for c in cs]
        yield
        r = [mm(jnp.concatenate([pm[c], d4[c]], axis=0), d4[c]) for c in cs]
        pm = [pm[c] + r[c][0:DN_CHUNK] for c in cs]
        d8 = [r[c][DN_CHUNK:] for c in cs]
        yield
        t0 = [pm[c] + mm(pm[c], d8[c]) for c in cs]
        yield
        m1 = [jnp.where(same32 & jnp.logical_not(same16), a, 0.0) for a in a_m]
        r = [mm(t0[c], m1[c]) for c in cs]
        yield
        t1 = [t0[c] - mm(r[c], t0[c]) for c in cs]
        yield
        m2 = [jnp.where(same32, 0.0, a) for a in a_m]
        r = [mm(t1[c], m2[c]) for c in cs]
        yield
        t2 = [(t1[c] - mm(r[c], t1[c])).astype(BF16) for c in cs]
        yield
        u_c = [_dot(t2[c], bd(vb[c])) for c in cs]
        w_c = [_dot(t2[c], bd(kb[c] * egc[c])) for c in cs]
        yield

        kg = [(k_c[c] * ekg[c]).astype(BF16) for c in cs]
        nmat = [jnp.where(same_head, _dot_tn(w_c[c].astype(BF16), kg[c]), 0.0).astype(BF16) for c in cs]
        bfull = [jnp.where(same_head, _dot_tn(u_c[c].astype(BF16), kg[c]), 0.0) for c in cs]
        bcomp = [b[0:64] + b[64:128] + b[128:192] + b[192:256] for b in bfull]
        yield
        qkb = [q.astype(BF16) for q in qk]
        qt = [(q_c[c] * egc[c] - _dot(qkb[c], bd(w_c[c]))).astype(BF16) for c in cs]
        ou = [_dot(qkb[c], bd(u_c[c])) for c in cs]
        yield

        st = st_ref[...]
        for c in cs:
            o_c = _dot_nt(qt[c], bd(st)) + ou[c]
            st = st * egl[c] - _dot(st.astype(BF16), nmat[c]) + bcomp[c]
            if c == cs[-1]:
                st_ref[...] = st
            yield
            o_n = o_c * lax.rsqrt(head_mean(o_c * o_c) + EPS) * out_gain
            zrows = slice(HALO + c * DN_CHUNK, HALO + (c + 1) * DN_CHUNK)
            mix_t[sls[c], 256:512] = (o_n * gate_silu(z_ref, zrows, 256, 512)).astype(BF16)

    def convs(z_ref, tile):
        mix_t, ybuf_t, ysh_t, yc_t = mix_ref.at[tile], ybuf.at[tile], ysh.at[tile], yc_ref.at[tile]
        for c in range(ts // C_CHUNK):
            sl = slice(c * C_CHUNK, (c + 1) * C_CHUNK)
            zrows = slice(HALO + c * C_CHUNK, HALO + (c + 1) * C_CHUNK)
            uv = z_ref[zrows, COL_C:COL_C + 512]
            gl_uv = 0.5 * uv * (1.0 + jnp.tanh(0.7978845608028654 * (uv + 0.044715 * (uv * uv * uv))))
            cu = gl_uv[:, 0:256]
            cv = gl_uv[:, 256:512]
            mu = jnp.mean(cv, axis=-1, keepdims=True)
            dv = cv - mu
            var = jnp.mean(dv * dv, axis=-1, keepdims=True)
            cvn = dv * lax.rsqrt(var + EPS) * vec_ref[3:4, 256:512] + vec_ref[3:4, 512:768]
            yield
            vblk = jnp.concatenate([jnp.where(head_mask[g], cvn, 0.0).astype(BF16) for g in range(4)], axis=0)
            zc = _dot(cw, vblk) + cbias
            mix_t[sl, 512:768] = (cu * zc * gate_silu(z_ref, zrows, 512, 768)).astype(BF16)
            yield

        for r0 in range(0, HALO + ts, 96):
            ag = z_ref[r0:r0 + 96, COL_D:COL_D + 512]
            ybuf_t[r0:r0 + 96, :] = ag[:, 0:256] * _sigmoid(ag[:, 256:512])
            yield
        for r in range(1, 8):
            ysh_t[r - 1] = ybuf_t[r:r + ts + HALO - 8, :]
            if r % 2 == 1:
                yield
        for rb in range(0, ts, CONV_ROWS):
            acc = None
            for kk in range(D_CONV):
                r0 = HALO - (D_CONV - 1) + kk
                a8, r = (r0 // 8) * 8 + rb, r0 % 8
                win = ybuf_t[a8:a8 + CONV_ROWS, :] if r == 0 else ysh_t[r - 1, a8:a8 + CONV_ROWS, :]
                term = win * wd[kk:kk + 1, :]
                acc = term if acc is None else acc + term
            yc_t[rb:rb + CONV_ROWS, :] = acc + vec_ref[3:4, 768:1024]
            yield
        for c in range(ts // C_CHUNK):
            sl = slice(c * C_CHUNK, (c + 1) * C_CHUNK)
            zrows = slice(HALO + c * C_CHUNK, HALO + (c + 1) * C_CHUNK)
            yc = yc_t[sl, :]
            mu = head_mean(yc)
            dy = yc - mu
            var = head_mean(dy * dy)
            yn = dy * lax.rsqrt(var + EPS) * vec_ref[4:5, 0:256] + vec_ref[4:5, 256:512]
            mix_t[sl, 768:1024] = (_silu(yn) * gate_silu(z_ref, zrows, 768, 1024)).astype(BF16)
            yield

    def out_proj(tile):
        mix_t, x1_t, hg_t = mix_ref.at[tile], x1_ref.at[tile], hg_ref.at[tile]
        rows = slice(tile * ts, (tile + 1) * ts)
        for c0 in range(0, D_MODEL, PROJ_CHUNK):
            cols = slice(c0, c0 + PROJ_CHUNK)
            x1_t[:, cols] = x_ref[0, rows, cols] + _dot(mix_t[...], w_out_ref[:, cols])
            yield
        for r0 in range(0, ts, 64):
            x1 = x1_t[r0:r0 + 64, :]
            ms1 = jnp.mean(x1 * x1, axis=-1, keepdims=True)
            hg_t[r0:r0 + 64, :] = (x1 * lax.rsqrt(ms1 + EPS) * vec_ref[1:2, :]).astype(BF16)
        yield
        pb = p_ref[0, rows, :].astype(BF16)
        for c0 in range(0, D_MODEL, PROJ_CHUNK):
            cols = slice(c0, c0 + PROJ_CHUNK)
            ple_gate = _sigmoid(_dot(hg_t[...], gate_w_ref[:, cols]))
            pe = _dot(pb, ple_w_ref[:, cols])
            o_ref[0, rows, cols] = x1_t[:, cols] + ple_gate * pe
            yield

    @pl.when(sp == 0)
    def _():
        za_ref[0:HALO, :] = jnp.zeros((HALO, Z_WIDTH), F32)
        kbuf[:, 0:WINDOW, :] = jnp.zeros((2, WINDOW, 128), BF16)
        vbuf[:, 0:WINDOW, :] = jnp.zeros((2, WINDOW, 128), BF16)
        st_ref[...] = jnp.zeros((HEAD_DIM, 256), F32)

    @pl.when(step == 0)
    def _():
        _run_streams([(in_proj(lambda r0, n: x_ref[0, r0:r0 + n, :], za_ref), 1)])

    zb_ref[0:HALO, :] = za_ref[ts:ts + HALO, :]
    _run_streams([(deltanet(za_ref, 0), 1), (convs(za_ref, 0), 1), (attention(za_ref, 0), 2),
                  (in_proj(lambda r0, n: x_ref[0, ts + r0:ts + r0 + n, :], zb_ref), 2)])
    za_ref[0:HALO, :] = zb_ref[ts:ts + HALO, :]
    _run_streams([(deltanet(zb_ref, 1), 1), (convs(zb_ref, 1), 1), (attention(zb_ref, 1), 2),
                  (in_proj(lambda r0, n: xn_ref[0, r0:r0 + n, :], za_ref), 2), (out_proj(0), 2)])
    _run_streams([(out_proj(1), 1)])


def _layer_spec(arr, layer, cols=None):
    shape = arr.shape[1:] if cols is None else arr.shape[1:-1] + (cols,)
    return pl.BlockSpec((None,) + shape, lambda g: (layer,) + (0,) * (arr.ndim - 1),
                        pipeline_mode=pl.Buffered(1))


def _layer_call(x, p, layer, cos_t, sin_t, sink, w_in, w_mid, w_tail, w_out, ple_w, gate_w, vec, bconv, cw, cbias,
                dconv, ts):
    bsz, seq, _ = x.shape
    tiles_per_seq = seq // ts
    steps_per_seq = tiles_per_seq // 2
    n_steps = bsz * steps_per_seq
    last_tile = bsz * tiles_per_seq - 1

    def pair_map(g):
        return (g // steps_per_seq, g % steps_per_seq, 0)

    def next_tile_map(g):
        t = jnp.minimum(2 * g + 2, last_tile)
        return (t // tiles_per_seq, t % tiles_per_seq, 0)

    in_specs = [
        pl.BlockSpec(memory_space=pltpu.SMEM),
        pl.BlockSpec((1, 2 * ts, D_MODEL), pair_map),
        pl.BlockSpec((1, ts, D_MODEL), next_tile_map),
        pl.BlockSpec((None, 1, 2 * ts, D_PLE), lambda g: (layer,) + pair_map(g)),
        pl.BlockSpec((2 * ts, 128), lambda g: (g % steps_per_seq, 0)),
        pl.BlockSpec((2 * ts, 128), lambda g: (g % steps_per_seq, 0)),
        _layer_spec(w_in, layer, cols=W_IN_SPLIT),
    ] + [_layer_spec(a, layer) for a in (w_mid, w_tail, w_out, ple_w, gate_w, vec, bconv, cw, cbias, dconv)]
    scratch = [
        pltpu.VMEM((HALO + ts, Z_WIDTH), F32),
        pltpu.VMEM((HALO + ts, Z_WIDTH), F32),
        pltpu.VMEM((ts, D_MODEL), BF16),
        pltpu.VMEM((2, WINDOW + ts, 128), BF16),
        pltpu.VMEM((2, WINDOW + ts, 128), BF16),
        pltpu.VMEM((HEAD_DIM, 256), F32),
        pltpu.VMEM((2, ts, D_MODEL), BF16),
        pltpu.VMEM((2, HALO + ts, GROUP_WIDTH), F32),
        pltpu.VMEM((2, 7, HALO + ts - 8, GROUP_WIDTH), F32),
        pltpu.VMEM((2, ts, GROUP_WIDTH), F32),
        pltpu.VMEM((2, ts, D_MODEL), F32),
        pltpu.VMEM((2, ts, D_MODEL), BF16),
    ]
    return pl.pallas_call(
        functools.partial(_layer_kernel, ts=ts, steps_per_seq=steps_per_seq, layer=layer),
        grid=(n_steps,),
        in_specs=in_specs,
        out_specs=pl.BlockSpec((1, 2 * ts, D_MODEL), pair_map),
        out_shape=jax.ShapeDtypeStruct(x.shape, F32),
        scratch_shapes=scratch,
        compiler_params=pltpu.CompilerParams(
            dimension_semantics=("arbitrary",),
            vmem_limit_bytes=VMEM_LIMIT_BYTES,
        ),
        name="hybrid_trunk_layer",
    )(sink, x, x, p, cos_t, sin_t, w_in, w_mid, w_tail, w_out, ple_w, gate_w, vec, bconv, cw, cbias, dconv)


def _lanes(v, start, width):
    return jnp.pad(v.astype(F32), ((0, 0), (start, width - start - v.shape[1])))


def kernel(x, p, norm_gain, w_in, w_out, a_q_gain, a_k_gain, a_sink, b_conv, b_a_log, b_dt_bias, b_out_gain,
           c_ln_gain, c_ln_bias, c_w_s, c_b_s, d_conv, d_conv_bias, d_ln_gain, d_ln_bias,
           ple_w, ple_gate_norm, ple_gate_w):
    depth = w_in.shape[0]
    bsz, seq, _ = x.shape
    ts = min(SEQ_TILE, seq)
    assert seq % (2 * ts) == 0 and ts % WINDOW == 0

    inv = ROPE_THETA ** (-jnp.arange(0, HEAD_DIM, 2, dtype=F32) / HEAD_DIM)
    ang = jnp.arange(seq, dtype=F32)[:, None] * inv[None, :]
    cos, sin = jnp.cos(ang), jnp.sin(ang)
    cos_t = jnp.concatenate([cos, cos, cos, cos], axis=-1)
    sin_t = jnp.concatenate([-sin, sin, -sin, sin], axis=-1)

    w_in_b = w_in.astype(BF16)
    w_mid = w_in_b[:, :, W_IN_SPLIT + 2 * N_HEADS:]
    w_tail = jnp.pad(w_in_b[:, :, W_IN_SPLIT:W_IN_SPLIT + 2 * N_HEADS], ((0, 0), (0, 0), (0, 128 - 2 * N_HEADS)))
    zeros = jnp.zeros((depth, D_MODEL), F32)
    row2 = jnp.concatenate([
        jnp.tile(a_q_gain * (HEAD_DIM ** -0.5), (1, 4)), jnp.tile(a_k_gain, (1, 2)), jnp.zeros((depth, 128), F32),
        _lanes(b_a_log, N_HEADS, 128), _lanes(b_dt_bias, N_HEADS, 128), jnp.zeros((depth, 256), F32)], axis=1)
    row3 = jnp.concatenate([jnp.tile(b_out_gain, (1, 4)), c_ln_gain, c_ln_bias, d_conv_bias], axis=1)
    row4 = jnp.concatenate([d_ln_gain, d_ln_bias, jnp.zeros((depth, 512), F32)], axis=1)
    vec = jnp.stack([norm_gain, ple_gate_norm, row2, row3, row4, zeros, zeros, zeros], axis=1).astype(F32)
    cw = jnp.transpose(c_w_s, (0, 2, 1, 3)).reshape(depth, C_CHUNK, 4 * C_CHUNK)
    cbias = jnp.repeat(jnp.swapaxes(c_b_s, 1, 2), GROUP_WIDTH // 4, axis=2)
    w_out_b, ple_w_b, gate_w_b = w_out.astype(BF16), ple_w.astype(BF16), ple_gate_w.astype(BF16)
    sink = a_sink.astype(F32)

    for i in range(depth):
        x = _layer_call(x, p, i, cos_t, sin_t, sink, w_in_b, w_mid, w_tail, w_out_b, ple_w_b, gate_w_b, vec,
                        b_conv, cw, cbias, d_conv, ts)
    return x
```

```python
import functools

import jax
import jax.numpy as jnp
from jax import lax
from jax.experimental import pallas as pl
from jax.experimental.pallas import tpu as pltpu

F32 = jnp.float32
BF16 = jnp.bfloat16

EPS = 1e-6
D_MODEL = 1024
D_PLE = 256
GROUP_WIDTH = 256
HEAD_DIM = 64
N_HEADS = 4
WINDOW = 128
ROPE_THETA = 10000.0
B_CONV = 4
DN_CHUNK = 64
C_CHUNK = 128
D_CONV = 31

COL_AQ, COL_AK, COL_AV = 0, 256, 384
COL_B = 512
COL_C = 1280
COL_D = 1792
COL_GATE = 2304
COL_TAIL = 3328
Z_WIDTH = 3456
W_IN_SPLIT = 1280
HALO = 32

SEQ_TILE = 256
PROJ_CHUNK = 256
CONV_ROWS = 32
VMEM_LIMIT_BYTES = 56 * 1024 * 1024


def _sigmoid(x):
    return 0.5 + 0.5 * jnp.tanh(0.5 * x)


def _silu(x):
    h = 0.5 * x
    return h + h * jnp.tanh(h)


def _dot(a, b):
    return jnp.dot(a, b, preferred_element_type=F32)


def _dot_nt(a, b):
    return lax.dot_general(a, b, (((1,), (1,)), ((), ())), preferred_element_type=F32)


def _dot_tn(a, b):
    return lax.dot_general(a, b, (((0,), (0,)), ((), ())), preferred_element_type=F32)


def _split_lhs_dot(a, b_bf16, n):
    acc = None
    r = a
    for i in range(n):
        hi = r.astype(BF16)
        t = _dot(hi, b_bf16)
        acc = t if acc is None else acc + t
        if i + 1 < n:
            r = r - hi.astype(F32)
    return acc


def _split_rhs_dot(a_bf16, b, n):
    acc = None
    r = b
    for i in range(n):
        hi = r.astype(BF16)
        t = _dot(a_bf16, hi)
        acc = t if acc is None else acc + t
        if i + 1 < n:
            r = r - hi.astype(F32)
    return acc


def _run_streams(streams):
    live = list(streams)
    rnd = 0
    while live:
        for item in list(live):
            gen, period = item
            if rnd % period == 0 and next(gen, StopIteration) is StopIteration:
                live.remove(item)
        rnd += 1


def _layer_kernel(sink_ref, x_ref, xn_ref, p_ref, cos_ref, sin_ref, w1_ref, w2_ref, w3_ref, w_out_ref, ple_w_ref,
                  gate_w_ref, vec_ref, bconv_ref, cw_ref, cbias_ref, dconv_ref, o_ref,
                  za_ref, zb_ref, h_ref, kbuf, vbuf, st_ref, mix_ref, ybuf, ysh, yc_ref, x1_ref, hg_ref, sg_ref,
                  *, ts, steps_per_seq, layer):
    step = pl.program_id(0)
    sp = step % steps_per_seq

    lane256 = lax.broadcasted_iota(jnp.int32, (1, 256), 1)
    head_mask = [(lane256 // HEAD_DIM) == h for h in range(N_HEADS)]
    r256 = lax.broadcasted_iota(jnp.int32, (256, 256), 0)
    c256 = lax.broadcasted_iota(jnp.int32, (256, 256), 1)
    same_head = (r256 // HEAD_DIM) == (c256 // HEAD_DIM)
    g_sum = jnp.where(same_head, 1.0, 0.0).astype(BF16)
    g_mean = jnp.where(same_head, 1.0 / HEAD_DIM, 0.0).astype(BF16)
    lane128 = lax.broadcasted_iota(jnp.int32, (1, 128), 1)
    first_half = (lane128 % HEAD_DIM) < (HEAD_DIM // 2)
    lo128 = lane128 < HEAD_DIM
    q_gain = vec_ref[2:3, 0:256]
    k_gain = vec_ref[2:3, 256:384]
    qi = lax.broadcasted_iota(jnp.int32, (WINDOW, 2 * WINDOW), 0)
    kj = lax.broadcasted_iota(jnp.int32, (WINDOW, 2 * WINDOW), 1)
    rel = qi + WINDOW - kj
    band = (rel >= 0) & (rel < WINDOW)
    wb = bconv_ref[...]
    er = lax.broadcasted_iota(jnp.int32, (128, 256), 0)
    ec = lax.broadcasted_iota(jnp.int32, (128, 256), 1) // HEAD_DIM
    e_beta = jnp.where(er == ec, 1.0, 0.0).astype(BF16)
    e_alpha = jnp.where(er == ec + N_HEADS, 1.0, 0.0).astype(BF16)
    tr = lax.broadcasted_iota(jnp.int32, (ts, ts), 0)
    tc = lax.broadcasted_iota(jnp.int32, (ts, ts), 1)
    l_bd = jnp.where(((tr // DN_CHUNK) == (tc // DN_CHUNK)) & (tr >= tc), 1.0, 0.0).astype(BF16)
    ci = lax.broadcasted_iota(jnp.int32, (DN_CHUNK, 256), 0)
    cj = lax.broadcasted_iota(jnp.int32, (DN_CHUNK, 256), 1) % HEAD_DIM
    eye = ci == cj
    tril = ci >= cj
    strict = ci > cj
    same16 = (ci // 16) == (cj // 16)
    same32 = (ci // 32) == (cj // 32)
    eye_f = jnp.where(eye, 1.0, 0.0)
    out_gain = vec_ref[3:4, 0:256]
    wi = lax.broadcasted_iota(jnp.int32, (C_CHUNK, 4 * C_CHUNK), 0)
    wj = lax.broadcasted_iota(jnp.int32, (C_CHUNK, 4 * C_CHUNK), 1) % C_CHUNK
    cw = jnp.where(wi >= wj, cw_ref[...], 0.0).astype(BF16)
    cbias = cbias_ref[...]
    wd = dconv_ref[...]

    def head_mean(v, n=1):
        return _split_lhs_dot(v, g_mean, n)

    def head_sum(v, n=1):
        return _split_lhs_dot(v, g_sum, n)

    def bd(xv):
        xb = xv.astype(BF16)
        zero = jnp.zeros_like(xb)
        return jnp.concatenate([jnp.where(head_mask[h], xb, zero) for h in range(N_HEADS)], axis=0)

    def mm(lhs, xv):
        return _dot(lhs.astype(BF16), bd(xv))

    def rope(v, cos, sin):
        sw = jnp.where(first_half, pltpu.roll(v, 128 - HEAD_DIM // 2, 1), pltpu.roll(v, HEAD_DIM // 2, 1))
        return v * cos + sw * sin

    proj_chunks = ([(w1_ref, c, c, PROJ_CHUNK) for c in range(0, W_IN_SPLIT, PROJ_CHUNK)]
                   + [(w2_ref, c, W_IN_SPLIT + c, PROJ_CHUNK) for c in range(0, COL_TAIL - W_IN_SPLIT, PROJ_CHUNK)]
                   + [(w3_ref, 0, COL_TAIL, Z_WIDTH - COL_TAIL)])

    def in_proj(x_rows, z_ref, slot):
        for r0 in range(0, ts, 64):
            xv = x_rows(r0, 64)
            ms = jnp.mean(xv * xv, axis=-1, keepdims=True)
            h_ref[r0:r0 + 64, :] = (xv * lax.rsqrt(ms + EPS) * vec_ref[0:1, :]).astype(BF16)
        for w_ref, wc, zc, width in proj_chunks:
            zv = _dot(h_ref[...], w_ref[:, wc:wc + width])
            if COL_C <= zc < COL_D:
                zv = 0.5 * zv * (1.0 + jnp.tanh(0.7978845608028654 * (zv + 0.044715 * (zv * zv * zv))))
            if zc == COL_D + 256:
                ybuf[slot, HALO:HALO + ts, :] = z_ref[HALO:HALO + ts, COL_D:COL_D + 256] * _sigmoid(zv)
            elif zc >= COL_GATE and zc < COL_TAIL:
                sg_ref[slot, :, zc - COL_GATE:zc - COL_GATE + width] = _silu(zv)
            else:
                z_ref[HALO:HALO + ts, zc:zc + width] = zv
            yield

    def gate_silu(tile, rows, c0, c1):
        return sg_ref[tile, rows, c0:c1]

    def attention(z_ref, tile):
        mix_t = mix_ref.at[tile]
        for blk in range(ts // WINDOW):
            rows = slice(HALO + blk * WINDOW, HALO + (blk + 1) * WINDOW)
            q = z_ref[rows, COL_AQ:COL_AQ + 256]
            k = z_ref[rows, COL_AK:COL_AK + 128]
            v = z_ref[rows, COL_AV:COL_AV + 128]
            cos = cos_ref[tile * ts + blk * WINDOW:tile * ts + (blk + 1) * WINDOW, :]
            sin = sin_ref[tile * ts + blk * WINDOW:tile * ts + (blk + 1) * WINDOW, :]
            qn = q * lax.rsqrt(head_mean(q * q) + EPS) * q_gain
            kn = k * lax.rsqrt(_split_lhs_dot(k * k, g_mean[0:128, 0:128], 1) + EPS) * k_gain
            qa = rope(qn[:, 0:128], cos, sin)
            qb = rope(qn[:, 128:256], cos, sin)
            kr = rope(kn, cos, sin)
            cur = slice(WINDOW + blk * WINDOW, 2 * WINDOW + blk * WINDOW)
            both = slice(blk * WINDOW, (blk + 2) * WINDOW)
            kbuf[0, cur, :] = kr.astype(BF16)
            kbuf[1, cur, :] = pltpu.roll(kr, HEAD_DIM, 1).astype(BF16)
            vbuf[0, cur, :] = v.astype(BF16)
            vbuf[1, cur, :] = pltpu.roll(v, HEAD_DIM, 1).astype(BF16)
            lhs = [jnp.where(lo128, qa, 0.0).astype(BF16), jnp.where(lo128, 0.0, qa).astype(BF16),
                   jnp.where(lo128, qb, 0.0).astype(BF16), jnp.where(lo128, 0.0, qb).astype(BF16)]
            yield
            valid = band
            if blk == 0 and tile == 0:
                valid = valid & (kj >= jnp.where(sp > 0, 0, WINDOW))
            outs = []
            for r in range(N_HEADS):
                slot = 0 if r in (0, 3) else 1
                sc = jnp.where(valid, _dot_nt(lhs[r], kbuf[slot, both, :]), -jnp.inf)
                sink = sink_ref[layer, r]
                m = jnp.maximum(jnp.max(sc, axis=-1, keepdims=True), sink)
                e = jnp.exp(sc - m)
                den = jnp.sum(e, axis=-1, keepdims=True) + jnp.exp(sink - m)
                outs.append(_dot(e.astype(BF16), vbuf[slot, both, :]) / den)
                yield
            o_a0 = jnp.where(lo128, outs[0], outs[1])
            o_a1 = jnp.where(lo128, outs[2], outs[3])
            mrows = slice(blk * WINDOW, (blk + 1) * WINDOW)
            mix_t[mrows, 0:128] = (o_a0 * gate_silu(tile, mrows, 0, 128)).astype(BF16)
            mix_t[mrows, 128:256] = (o_a1 * gate_silu(tile, mrows, 128, 256)).astype(BF16)
            yield
        kbuf[:, 0:WINDOW, :] = kbuf[:, ts:ts + WINDOW, :]
        vbuf[:, 0:WINDOW, :] = vbuf[:, ts:ts + WINDOW, :]

    def deltanet(z_ref, tile):
        mix_t = mix_ref.at[tile]
        cs = range(ts // DN_CHUNK)
        sls = [slice(c * DN_CHUNK, (c + 1) * DN_CHUNK) for c in cs]
        q_c, k_c, v_c = [], [], []
        for c in cs:
            r0 = HALO + c * DN_CHUNK - (B_CONV - 1)
            acc = None
            for kk in range(B_CONV):
                term = z_ref[r0 + kk:r0 + kk + DN_CHUNK, COL_B:COL_B + 768] * wb[kk:kk + 1, :]
                acc = term if acc is None else acc + term
            qkv = _silu(acc)
            q, k = qkv[:, 0:256], qkv[:, 256:512]
            ss = head_sum(jnp.concatenate([q * q, k * k], axis=0))
            q_c.append(q * lax.rsqrt(ss[0:DN_CHUNK] + EPS) * (HEAD_DIM ** -0.5))
            k_c.append(k * lax.rsqrt(ss[DN_CHUNK:] + EPS))
            v_c.append(qkv[:, 512:768])
            yield

        tail = z_ref[HALO:HALO + ts, COL_TAIL:COL_TAIL + 128]
        beta = _sigmoid(tail)
        xg = tail + vec_ref[2:3, 640:768]
        softplus = jnp.maximum(xg, 0.0) + jnp.log1p(jnp.exp(-jnp.abs(xg)))
        gdec = -jnp.exp(vec_ref[2:3, 512:640]) * softplus
        beta_f = _split_lhs_dot(beta, e_beta, 1)
        gdec_f = _split_lhs_dot(gdec, e_alpha, 2)
        gc = _split_rhs_dot(l_bd, gdec_f, 2)
        yield

        gc_c = [gc[s] for s in sls]
        gl = [g[DN_CHUNK - 1:DN_CHUNK, :] for g in gc_c]
        egc = [jnp.exp(g) for g in gc_c]
        ekg = [jnp.exp(gl[c] - gc_c[c]) for c in cs]
        egl = [jnp.exp(g) for g in gl]
        kb = [k_c[c] * beta_f[sls[c]] for c in cs]
        vb = [v_c[c] * beta_f[sls[c]] for c in cs]
        grow = [jnp.sum(jnp.where(eye, g, 0.0), axis=0, keepdims=True) for g in gc_c]
        decay = [jnp.exp(jnp.minimum(gc_c[c] - grow[c], 0.0)) for c in cs]
        yield
        aq = [_dot_nt(jnp.concatenate([kb[c], q_c[c]], axis=0).astype(BF16), bd(k_c[c])) for c in cs]
        a_m = [jnp.where(strict, aq[c][0:DN_CHUNK] * decay[c], 0.0) for c in cs]
        qk = [jnp.where(tril, aq[c][DN_CHUNK:2 * DN_CHUNK] * decay[c], 0.0) for c in cs]
        yield

        d1 = [jnp.where(same16, a, 0.0) for a in a_m]
        pm = [eye_f - d for d in d1]
        d2 = [mm(d, d) for d in d1]
        yield
        r = [mm(jnp.concatenate([pm[c], d2[c]], axis=0), d2[c]) for c in cs]
        pm = [pm[c] + r[c][0:DN_CHUNK] for c in cs]
        d4 = [r[c][DN_CHUNK:] for c in cs]
        yield
        r = [mm(jnp.concatenate([pm[c], d4[c]], axis=0), d4[c]) for c in cs]
        pm = [pm[c] + r[c][0:DN_CHUNK] for c in cs]
        d8 = [r[c][DN_CHUNK:] for c in cs]
        yield
        t0 = [pm[c] + mm(pm[c], d8[c]) for c in cs]
        yield
        m1 = [jnp.where(same32 & jnp.logical_not(same16), a, 0.0) for a in a_m]
        r = [mm(t0[c], m1[c]) for c in cs]
        yield
        t1 = [t0[c] - mm(r[c], t0[c]) for c in cs]
        yield
        m2 = [jnp.where(same32, 0.0, a) for a in a_m]
        r = [mm(t1[c], m2[c]) for c in cs]
        yield
        t2 = [(t1[c] - mm(r[c], t1[c])).astype(BF16) for c in cs]
        yield
        u_c = [_dot(t2[c], bd(vb[c])) for c in cs]
        w_c = [_dot(t2[c], bd(kb[c] * egc[c])) for c in cs]
        yield

        kg = [(k_c[c] * ekg[c]).astype(BF16) for c in cs]
        nmat = [jnp.where(same_head, _dot_tn(w_c[c].astype(BF16), kg[c]), 0.0).astype(BF16) for c in cs]
        bfull = [jnp.where(same_head, _dot_tn(u_c[c].astype(BF16), kg[c]), 0.0) for c in cs]
        bcomp = [b[0:64] + b[64:128] + b[128:192] + b[192:256] for b in bfull]
        yield
        qkb = [q.astype(BF16) for q in qk]
        qt = [(q_c[c] * egc[c] - _dot(qkb[c], bd(w_c[c]))).astype(BF16) for c in cs]
        ou = [_dot(qkb[c], bd(u_c[c])) for c in cs]
        yield

        st = st_ref[...]
        for c in cs:
            o_c = _dot_nt(qt[c], bd(st)) + ou[c]
            st = st * egl[c] - _dot(st.astype(BF16), nmat[c]) + bcomp[c]
            if c == cs[-1]:
                st_ref[...] = st
            yield
            o_n = o_c * lax.rsqrt(head_mean(o_c * o_c) + EPS) * out_gain
            zrows = slice(HALO + c * DN_CHUNK, HALO + (c + 1) * DN_CHUNK)
            mix_t[sls[c], 256:512] = (o_n * gate_silu(tile, sls[c], 256, 512)).astype(BF16)

    def convs(z_ref, tile):
        mix_t, ybuf_t, ysh_t, yc_t = mix_ref.at[tile], ybuf.at[tile], ysh.at[tile], yc_ref.at[tile]
        for c in range(ts // C_CHUNK):
            sl = slice(c * C_CHUNK, (c + 1) * C_CHUNK)
            zrows = slice(HALO + c * C_CHUNK, HALO + (c + 1) * C_CHUNK)
            cu = z_ref[zrows, COL_C:COL_C + 256]
            cv = z_ref[zrows, COL_C + 256:COL_C + 512]
            mu = jnp.mean(cv, axis=-1, keepdims=True)
            dv = cv - mu
            var = jnp.mean(dv * dv, axis=-1, keepdims=True)
            cvn = dv * lax.rsqrt(var + EPS) * vec_ref[3:4, 256:512] + vec_ref[3:4, 512:768]
            yield
            vblk = jnp.concatenate([jnp.where(head_mask[g], cvn, 0.0).astype(BF16) for g in range(4)], axis=0)
            zc = _dot(cw, vblk) + cbias
            mix_t[sl, 512:768] = (cu * zc * gate_silu(tile, sl, 512, 768)).astype(BF16)
            yield

        for r in range(1, 8):
            ysh_t[r - 1] = ybuf_t[r:r + ts + HALO - 8, :]
            if r % 2 == 1:
                yield
        for rb in range(0, ts, CONV_ROWS):
            acc = None
            for kk in range(D_CONV):
                r0 = HALO - (D_CONV - 1) + kk
                a8, r = (r0 // 8) * 8 + rb, r0 % 8
                win = ybuf_t[a8:a8 + CONV_ROWS, :] if r == 0 else ysh_t[r - 1, a8:a8 + CONV_ROWS, :]
                term = win * wd[kk:kk + 1, :]
                acc = term if acc is None else acc + term
            yc_t[rb:rb + CONV_ROWS, :] = acc + vec_ref[3:4, 768:1024]
            yield
        for c in range(ts // C_CHUNK):
            sl = slice(c * C_CHUNK, (c + 1) * C_CHUNK)
            zrows = slice(HALO + c * C_CHUNK, HALO + (c + 1) * C_CHUNK)
            yc = yc_t[sl, :]
            mu = head_mean(yc)
            dy = yc - mu
            var = head_mean(dy * dy)
            yn = dy * lax.rsqrt(var + EPS) * vec_ref[4:5, 0:256] + vec_ref[4:5, 256:512]
            mix_t[sl, 768:1024] = (_silu(yn) * gate_silu(tile, sl, 768, 1024)).astype(BF16)
            yield

    def out_proj(tile):
        mix_t, x1_t, hg_t = mix_ref.at[tile], x1_ref.at[tile], hg_ref.at[tile]
        rows = slice(tile * ts, (tile + 1) * ts)
        for c0 in range(0, D_MODEL, PROJ_CHUNK):
            cols = slice(c0, c0 + PROJ_CHUNK)
            x1_t[:, cols] = x_ref[0, rows, cols] + _dot(mix_t[...], w_out_ref[:, cols])
            yield
        for r0 in range(0, ts, 64):
            x1 = x1_t[r0:r0 + 64, :]
            ms1 = jnp.mean(x1 * x1, axis=-1, keepdims=True)
            hg_t[r0:r0 + 64, :] = (x1 * lax.rsqrt(ms1 + EPS) * vec_ref[1:2, :]).astype(BF16)
        yield
        pb = p_ref[0, rows, :].astype(BF16)
        for c0 in range(0, D_MODEL, PROJ_CHUNK):
            cols = slice(c0, c0 + PROJ_CHUNK)
            ple_gate = _sigmoid(_dot(hg_t[...], gate_w_ref[:, cols]))
            pe = _dot(pb, ple_w_ref[:, cols])
            o_ref[0, rows, cols] = x1_t[:, cols] + ple_gate * pe
            yield

    @pl.when(sp == 0)
    def _():
        za_ref[0:HALO, :] = jnp.zeros((HALO, Z_WIDTH), F32)
        kbuf[:, 0:WINDOW, :] = jnp.zeros((2, WINDOW, 128), BF16)
        vbuf[:, 0:WINDOW, :] = jnp.zeros((2, WINDOW, 128), BF16)
        st_ref[...] = jnp.zeros((HEAD_DIM, 256), F32)
        ybuf[0, 0:HALO, :] = jnp.zeros((HALO, GROUP_WIDTH), F32)

    @pl.when(step == 0)
    def _():
        _run_streams([(in_proj(lambda r0, n: x_ref[0, r0:r0 + n, :], za_ref, 0), 1)])

    zb_ref[0:HALO, :] = za_ref[ts:ts + HALO, :]
    ybuf[1, 0:HALO, :] = ybuf[0, ts:ts + HALO, :]
    _run_streams([(deltanet(za_ref, 0), 1), (convs(za_ref, 0), 1), (attention(za_ref, 0), 2),
                  (in_proj(lambda r0, n: x_ref[0, ts + r0:ts + r0 + n, :], zb_ref, 1), 2)])
    za_ref[0:HALO, :] = zb_ref[ts:ts + HALO, :]
    ybuf[0, 0:HALO, :] = ybuf[1, ts:ts + HALO, :]
    _run_streams([(deltanet(zb_ref, 1), 1), (convs(zb_ref, 1), 1), (attention(zb_ref, 1), 2),
                  (in_proj(lambda r0, n: xn_ref[0, r0:r0 + n, :], za_ref, 0), 2), (out_proj(0), 2)])
    _run_streams([(out_proj(1), 1)])


def _layer_spec(arr, layer):
    return pl.BlockSpec((None,) + arr.shape[1:], lambda g: (layer,) + (0,) * (arr.ndim - 1),
                        pipeline_mode=pl.Buffered(1))


def _layer_call(x, p, layer, cos_t, sin_t, sink, w_in, w_mid, w_tail, w_out, ple_w, gate_w, vec, bconv, cw, cbias,
                dconv, ts):
    bsz, seq, _ = x.shape
    tiles_per_seq = seq // ts
    steps_per_seq = tiles_per_seq // 2
    n_steps = bsz * steps_per_seq
    last_tile = bsz * tiles_per_seq - 1

    def pair_map(g):
        return (g // steps_per_seq, g % steps_per_seq, 0)

    def next_tile_map(g):
        t = jnp.minimum(2 * g + 2, last_tile)
        return (t // tiles_per_seq, t % tiles_per_seq, 0)

    in_specs = [
        pl.BlockSpec(memory_space=pltpu.SMEM),
        pl.BlockSpec((1, 2 * ts, D_MODEL), pair_map),
        pl.BlockSpec((1, ts, D_MODEL), next_tile_map),
        pl.BlockSpec((None, 1, 2 * ts, D_PLE), lambda g: (layer,) + pair_map(g)),
        pl.BlockSpec((2 * ts, 128), lambda g: (g % steps_per_seq, 0)),
        pl.BlockSpec((2 * ts, 128), lambda g: (g % steps_per_seq, 0)),
    ] + [_layer_spec(a, layer) for a in (w_in, w_mid, w_tail, w_out, ple_w, gate_w, vec, bconv, cw, cbias, dconv)]
    scratch = [
        pltpu.VMEM((HALO + ts, Z_WIDTH), F32),
        pltpu.VMEM((HALO + ts, Z_WIDTH), F32),
        pltpu.VMEM((ts, D_MODEL), BF16),
        pltpu.VMEM((2, WINDOW + ts, 128), BF16),
        pltpu.VMEM((2, WINDOW + ts, 128), BF16),
        pltpu.VMEM((HEAD_DIM, 256), F32),
        pltpu.VMEM((2, ts, D_MODEL), BF16),
        pltpu.VMEM((2, HALO + ts, GROUP_WIDTH), F32),
        pltpu.VMEM((2, 7, HALO + ts - 8, GROUP_WIDTH), F32),
        pltpu.VMEM((2, ts, GROUP_WIDTH), F32),
        pltpu.VMEM((2, ts, D_MODEL), F32),
        pltpu.VMEM((2, ts, D_MODEL), BF16),
        pltpu.VMEM((2, ts, D_MODEL), F32),
    ]
    return pl.pallas_call(
        functools.partial(_layer_kernel, ts=ts, steps_per_seq=steps_per_seq, layer=layer),
        grid=(n_steps,),
        in_specs=in_specs,
        out_specs=pl.BlockSpec((1, 2 * ts, D_MODEL), pair_map),
        out_shape=jax.ShapeDtypeStruct(x.shape, F32),
        scratch_shapes=scratch,
        compiler_params=pltpu.CompilerParams(
            dimension_semantics=("arbitrary",),
            vmem_limit_bytes=VMEM_LIMIT_BYTES,
        ),
        name="hybrid_trunk_layer",
    )(sink, x, x, p, cos_t, sin_t, w_in, w_mid, w_tail, w_out, ple_w, gate_w, vec, bconv, cw, cbias, dconv)


def _lanes(v, start, width):
    return jnp.pad(v.astype(F32), ((0, 0), (start, width - start - v.shape[1])))


def kernel(x, p, norm_gain, w_in, w_out, a_q_gain, a_k_gain, a_sink, b_conv, b_a_log, b_dt_bias, b_out_gain,
           c_ln_gain, c_ln_bias, c_w_s, c_b_s, d_conv, d_conv_bias, d_ln_gain, d_ln_bias,
           ple_w, ple_gate_norm, ple_gate_w):
    depth = w_in.shape[0]
    bsz, seq, _ = x.shape
    ts = min(SEQ_TILE, seq)
    assert seq % (2 * ts) == 0 and ts % WINDOW == 0

    inv = ROPE_THETA ** (-jnp.arange(0, HEAD_DIM, 2, dtype=F32) / HEAD_DIM)
    ang_b = (jnp.arange(seq // WINDOW, dtype=F32) * WINDOW)[:, None] * inv[None, :]
    ang_r = jnp.arange(WINDOW, dtype=F32)[:, None] * inv[None, :]
    cb, sb, cr, sr = jnp.cos(ang_b)[:, None, :], jnp.sin(ang_b)[:, None, :], jnp.cos(ang_r)[None], jnp.sin(ang_r)[None]
    cos = (cb * cr - sb * sr).reshape(seq, HEAD_DIM // 2)
    sin = (sb * cr + cb * sr).reshape(seq, HEAD_DIM // 2)
    cos_t = jnp.concatenate([cos, cos, cos, cos], axis=-1)
    sin_t = jnp.concatenate([-sin, sin, -sin, sin], axis=-1)

    w_head = w_in[:, :, 0:W_IN_SPLIT].astype(BF16)
    w_mid = w_in[:, :, W_IN_SPLIT + 2 * N_HEADS:].astype(BF16)
    w_tail = jnp.pad(w_in[:, :, W_IN_SPLIT:W_IN_SPLIT + 2 * N_HEADS].astype(BF16),
                     ((0, 0), (0, 0), (0, 128 - 2 * N_HEADS)))
    zeros = jnp.zeros((depth, D_MODEL), F32)
    row2 = jnp.concatenate([
        jnp.tile(a_q_gain * (HEAD_DIM ** -0.5), (1, 4)), jnp.tile(a_k_gain, (1, 2)), jnp.zeros((depth, 128), F32),
        _lanes(b_a_log, N_HEADS, 128), _lanes(b_dt_bias, N_HEADS, 128), jnp.zeros((depth, 256), F32)], axis=1)
    row3 = jnp.concatenate([jnp.tile(b_out_gain, (1, 4)), c_ln_gain, c_ln_bias, d_conv_bias], axis=1)
    row4 = jnp.concatenate([d_ln_gain, d_ln_bias, jnp.zeros((depth, 512), F32)], axis=1)
    vec = jnp.stack([norm_gain, ple_gate_norm, row2, row3, row4, zeros, zeros, zeros], axis=1).astype(F32)
    cw = jnp.transpose(c_w_s, (0, 2, 1, 3)).reshape(depth, C_CHUNK, 4 * C_CHUNK)
    cbias = jnp.repeat(jnp.swapaxes(c_b_s, 1, 2), GROUP_WIDTH // 4, axis=2)
    w_out_b, ple_w_b, gate_w_b = w_out.astype(BF16), ple_w.astype(BF16), ple_gate_w.astype(BF16)
    sink = a_sink.astype(F32)

    for i in range(depth):
        x = _layer_call(x, p, i, cos_t, sin_t, sink, w_head, w_mid, w_tail, w_out_b, ple_w_b, gate_w_b, vec,
                        b_conv, cw, cbias, d_conv, ts)
    return x
```

```python
import functools

import jax
import jax.numpy as jnp
from jax import lax
from jax.experimental import pallas as pl
from jax.experimental.pallas import tpu as pltpu

F32 = jnp.float32
BF16 = jnp.bfloat16

EPS = 1e-6
D_MODEL = 1024
D_PLE = 256
GROUP_WIDTH = 256
HEAD_DIM = 64
N_HEADS = 4
WINDOW = 128
ROPE_THETA = 10000.0
B_CONV = 4
DN_CHUNK = 64
C_CHUNK = 128
D_CONV = 31

COL_AQ, COL_AK, COL_AV = 0, 256, 384
COL_B = 512
COL_C = 1280
COL_D = 1792
COL_GATE = 2304
COL_TAIL = 3328
Z_WIDTH = 3456
W_IN_SPLIT = 1280
HALO = 32

SEQ_TILE = 256
PROJ_CHUNK = 256
CONV_ROWS = 32
VMEM_LIMIT_BYTES = 56 * 1024 * 1024


def _sigmoid(x):
    return 0.5 + 0.5 * jnp.tanh(0.5 * x)


def _silu(x):
    h = 0.5 * x
    return h + h * jnp.tanh(h)


def _dot(a, b):
    return jnp.dot(a, b, preferred_element_type=F32)


def _dot_nt(a, b):
    return lax.dot_general(a, b, (((1,), (1,)), ((), ())), preferred_element_type=F32)


def _dot_tn(a, b):
    return lax.dot_general(a, b, (((0,), (0,)), ((), ())), preferred_element_type=F32)


def _split_lhs_dot(a, b_bf16, n):
    acc = None
    r = a
    for i in range(n):
        hi = r.astype(BF16)
        t = _dot(hi, b_bf16)
        acc = t if acc is None else acc + t
        if i + 1 < n:
            r = r - hi.astype(F32)
    return acc


def _split_rhs_dot(a_bf16, b, n):
    acc = None
    r = b
    for i in range(n):
        hi = r.astype(BF16)
        t = _dot(a_bf16, hi)
        acc = t if acc is None else acc + t
        if i + 1 < n:
            r = r - hi.astype(F32)
    return acc


def _run_streams(streams):
    live = list(streams)
    rnd = 0
    while live:
        for item in list(live):
            gen, period = item
            if rnd % period == 0 and next(gen, StopIteration) is StopIteration:
                live.remove(item)
        rnd += 1


def _layer_kernel(sink_ref, x_ref, xn_ref, p_ref, cos_ref, sin_ref, w1_ref, w2_ref, w3_ref, w_out_ref, ple_w_ref,
                  gate_w_ref, vec_ref, bconv_ref, cw_ref, cbias_ref, dconv_ref, o_ref,
                  za_ref, zb_ref, h_ref, kbuf, vbuf, st_ref, mix_ref, ybuf, ysh, yc_ref, x1_ref, hg_ref, sg_ref,
                  *, ts, steps_per_seq, layer):
    step = pl.program_id(0)
    sp = step % steps_per_seq

    lane256 = lax.broadcasted_iota(jnp.int32, (1, 256), 1)
    head_mask = [(lane256 // HEAD_DIM) == h for h in range(N_HEADS)]
    r256 = lax.broadcasted_iota(jnp.int32, (256, 256), 0)
    c256 = lax.broadcasted_iota(jnp.int32, (256, 256), 1)
    same_head = (r256 // HEAD_DIM) == (c256 // HEAD_DIM)
    g_sum = jnp.where(same_head, 1.0, 0.0).astype(BF16)
    g_mean = jnp.where(same_head, 1.0 / HEAD_DIM, 0.0).astype(BF16)
    lane128 = lax.broadcasted_iota(jnp.int32, (1, 128), 1)
    first_half = (lane128 % HEAD_DIM) < (HEAD_DIM // 2)
    lo128 = lane128 < HEAD_DIM
    q_gain = vec_ref[0:1, 0:256]
    k_gain = vec_ref[0:1, 256:384]
    qi = lax.broadcasted_iota(jnp.int32, (WINDOW, 2 * WINDOW), 0)
    kj = lax.broadcasted_iota(jnp.int32, (WINDOW, 2 * WINDOW), 1)
    rel = qi + WINDOW - kj
    band = (rel >= 0) & (rel < WINDOW)
    wb = bconv_ref[...]
    er = lax.broadcasted_iota(jnp.int32, (128, 256), 0)
    ec = lax.broadcasted_iota(jnp.int32, (128, 256), 1) // HEAD_DIM
    e_beta = jnp.where(er == ec, 1.0, 0.0).astype(BF16)
    e_alpha = jnp.where(er == ec + N_HEADS, 1.0, 0.0).astype(BF16)
    tr = lax.broadcasted_iota(jnp.int32, (ts, ts), 0)
    tc = lax.broadcasted_iota(jnp.int32, (ts, ts), 1)
    l_bd = jnp.where(((tr // DN_CHUNK) == (tc // DN_CHUNK)) & (tr >= tc), 1.0, 0.0).astype(BF16)
    ci = lax.broadcasted_iota(jnp.int32, (DN_CHUNK, 256), 0)
    cj = lax.broadcasted_iota(jnp.int32, (DN_CHUNK, 256), 1) % HEAD_DIM
    eye = ci == cj
    tril = ci >= cj
    strict = ci > cj
    same16 = (ci // 16) == (cj // 16)
    same32 = (ci // 32) == (cj // 32)
    eye_f = jnp.where(eye, 1.0, 0.0)
    out_gain = vec_ref[1:2, 0:256]
    wi = lax.broadcasted_iota(jnp.int32, (C_CHUNK, 4 * C_CHUNK), 0)
    wj = lax.broadcasted_iota(jnp.int32, (C_CHUNK, 4 * C_CHUNK), 1) % C_CHUNK
    cw = jnp.where(wi >= wj, cw_ref[...], 0.0).astype(BF16)
    cbias = cbias_ref[...]
    wd = dconv_ref[...]

    def head_mean(v, n=1):
        return _split_lhs_dot(v, g_mean, n)

    def head_sum(v, n=1):
        return _split_lhs_dot(v, g_sum, n)

    def bd(xv):
        xb = xv.astype(BF16)
        zero = jnp.zeros_like(xb)
        return jnp.concatenate([jnp.where(head_mask[h], xb, zero) for h in range(N_HEADS)], axis=0)

    def mm(lhs, xv):
        return _dot(lhs.astype(BF16), bd(xv))

    def rope(v, cos, sin):
        sw = jnp.where(first_half, pltpu.roll(v, 128 - HEAD_DIM // 2, 1), pltpu.roll(v, HEAD_DIM // 2, 1))
        return v * cos + sw * sin

    proj_chunks = ([(w1_ref, c, c, PROJ_CHUNK) for c in range(0, W_IN_SPLIT, PROJ_CHUNK)]
                   + [(w2_ref, c, W_IN_SPLIT + c, PROJ_CHUNK) for c in range(0, COL_TAIL - W_IN_SPLIT, PROJ_CHUNK)]
                   + [(w3_ref, 0, COL_TAIL, Z_WIDTH - COL_TAIL)])

    def in_proj(x_rows, z_ref, slot):
        for r0 in range(0, ts, 64):
            xv = x_rows(r0, 64)
            ms = jnp.mean(xv * xv, axis=-1, keepdims=True)
            h_ref[r0:r0 + 64, :] = (xv * lax.rsqrt(ms + EPS)).astype(BF16)
        for w_ref, wc, zc, width in proj_chunks:
            zv = _dot(h_ref[...], w_ref[:, wc:wc + width])
            if COL_C <= zc < COL_D:
                zv = 0.5 * zv * (1.0 + jnp.tanh(0.7978845608028654 * (zv + 0.044715 * (zv * zv * zv))))
            if zc == COL_D + 256:
                ybuf[slot, HALO:HALO + ts, :] = z_ref[HALO:HALO + ts, COL_D:COL_D + 256] * _sigmoid(zv)
            elif zc >= COL_GATE and zc < COL_TAIL:
                sg_ref[slot, :, zc - COL_GATE:zc - COL_GATE + width] = _silu(zv)
            else:
                z_ref[HALO:HALO + ts, zc:zc + width] = zv
            yield

    def gate_silu(tile, rows, c0, c1):
        return sg_ref[tile, rows, c0:c1]

    def attention(z_ref, tile):
        mix_t = mix_ref.at[tile]
        for blk in range(ts // WINDOW):
            rows = slice(HALO + blk * WINDOW, HALO + (blk + 1) * WINDOW)
            q = z_ref[rows, COL_AQ:COL_AQ + 256]
            k = z_ref[rows, COL_AK:COL_AK + 128]
            v = z_ref[rows, COL_AV:COL_AV + 128]
            cos = cos_ref[tile * ts + blk * WINDOW:tile * ts + (blk + 1) * WINDOW, :]
            sin = sin_ref[tile * ts + blk * WINDOW:tile * ts + (blk + 1) * WINDOW, :]
            qn = q * lax.rsqrt(head_mean(q * q) + EPS) * q_gain
            kn = k * lax.rsqrt(_split_lhs_dot(k * k, g_mean[0:128, 0:128], 1) + EPS) * k_gain
            qa = rope(qn[:, 0:128], cos, sin)
            qb = rope(qn[:, 128:256], cos, sin)
            kr = rope(kn, cos, sin)
            cur = slice(WINDOW + blk * WINDOW, 2 * WINDOW + blk * WINDOW)
            both = slice(blk * WINDOW, (blk + 2) * WINDOW)
            kbuf[0, cur, :] = kr.astype(BF16)
            kbuf[1, cur, :] = pltpu.roll(kr, HEAD_DIM, 1).astype(BF16)
            vbuf[0, cur, :] = v.astype(BF16)
            vbuf[1, cur, :] = pltpu.roll(v, HEAD_DIM, 1).astype(BF16)
            lhs = [jnp.where(lo128, qa, 0.0).astype(BF16), jnp.where(lo128, 0.0, qa).astype(BF16),
                   jnp.where(lo128, qb, 0.0).astype(BF16), jnp.where(lo128, 0.0, qb).astype(BF16)]
            yield
            valid = band
            if blk == 0 and tile == 0:
                valid = valid & (kj >= jnp.where(sp > 0, 0, WINDOW))
            outs = []
            for r in range(N_HEADS):
                slot = 0 if r in (0, 3) else 1
                sc = jnp.where(valid, _dot_nt(lhs[r], kbuf[slot, both, :]), -jnp.inf)
                sink = sink_ref[layer, r]
                m = jnp.maximum(jnp.max(sc, axis=-1, keepdims=True), sink)
                e = jnp.exp(sc - m)
                den = jnp.sum(e, axis=-1, keepdims=True) + jnp.exp(sink - m)
                outs.append(_dot(e.astype(BF16), vbuf[slot, both, :]) / den)
                yield
            o_a0 = jnp.where(lo128, outs[0], outs[1])
            o_a1 = jnp.where(lo128, outs[2], outs[3])
            mrows = slice(blk * WINDOW, (blk + 1) * WINDOW)
            mix_t[mrows, 0:128] = (o_a0 * gate_silu(tile, mrows, 0, 128)).astype(BF16)
            mix_t[mrows, 128:256] = (o_a1 * gate_silu(tile, mrows, 128, 256)).astype(BF16)
            yield
        kbuf[:, 0:WINDOW, :] = kbuf[:, ts:ts + WINDOW, :]
        vbuf[:, 0:WINDOW, :] = vbuf[:, ts:ts + WINDOW, :]

    def deltanet(z_ref, tile):
        mix_t = mix_ref.at[tile]
        cs = range(ts // DN_CHUNK)
        sls = [slice(c * DN_CHUNK, (c + 1) * DN_CHUNK) for c in cs]
        q_c, k_c, v_c = [], [], []
        for c in cs:
            r0 = HALO + c * DN_CHUNK - (B_CONV - 1)
            acc = None
            for kk in range(B_CONV):
                term = z_ref[r0 + kk:r0 + kk + DN_CHUNK, COL_B:COL_B + 768] * wb[kk:kk + 1, :]
                acc = term if acc is None else acc + term
            qkv = _silu(acc)
            q, k = qkv[:, 0:256], qkv[:, 256:512]
            ss = head_sum(jnp.concatenate([q * q, k * k], axis=0))
            q_c.append(q * lax.rsqrt(ss[0:DN_CHUNK] + EPS) * (HEAD_DIM ** -0.5))
            k_c.append(k * lax.rsqrt(ss[DN_CHUNK:] + EPS))
            v_c.append(qkv[:, 512:768])
            yield

        tail = z_ref[HALO:HALO + ts, COL_TAIL:COL_TAIL + 128]
        beta = _sigmoid(tail)
        xg = tail + vec_ref[0:1, 640:768]
        softplus = jnp.maximum(xg, 0.0) + jnp.log1p(jnp.exp(-jnp.abs(xg)))
        gdec = -jnp.exp(vec_ref[0:1, 512:640]) * softplus
        beta_f = _split_lhs_dot(beta, e_beta, 1)
        gdec_f = _split_lhs_dot(gdec, e_alpha, 2)
        gc = _split_rhs_dot(l_bd, gdec_f, 2)
        yield

        gc_c = [gc[s] for s in sls]
        gl = [g[DN_CHUNK - 1:DN_CHUNK, :] for g in gc_c]
        egc = [jnp.exp(g) for g in gc_c]
        ekg = [jnp.exp(gl[c] - gc_c[c]) for c in cs]
        egl = [jnp.exp(g) for g in gl]
        kb = [k_c[c] * beta_f[sls[c]] for c in cs]
        vb = [v_c[c] * beta_f[sls[c]] for c in cs]
        grow = [jnp.sum(jnp.where(eye, g, 0.0), axis=0, keepdims=True) for g in gc_c]
        decay = [jnp.exp(jnp.minimum(gc_c[c] - grow[c], 0.0)) for c in cs]
        yield
        aq = [_dot_nt(jnp.concatenate([kb[c], q_c[c]], axis=0).astype(BF16), bd(k_c[c])) for c in cs]
        a_m = [jnp.where(strict, aq[c][0:DN_CHUNK] * decay[c], 0.0) for c in cs]
        qk = [jnp.where(tril, aq[c][DN_CHUNK:2 * DN_CHUNK] * decay[c], 0.0) for c in cs]
        yield

        d1 = [jnp.where(same16, a, 0.0) for a in a_m]
        pm = [eye_f - d for d in d1]
        d2 = [mm(d, d) for d in d1]
        yield
        r = [mm(jnp.concatenate([pm[c], d2[c]], axis=0), d2[c]) for c in cs]
        pm = [pm[c] + r[c][0:DN_CHUNK] for c in cs]
        d4 = [r[c][DN_CHUNK:] for c in cs]
        yield
        r = [mm(jnp.concatenate([pm[c], d4[c]], axis=0), d4[c]) for c in cs]
        pm = [pm[c] + r[c][0:DN_CHUNK] for c in cs]
        d8 = [r[c][DN_CHUNK:] for c in cs]
        yield
        t0 = [pm[c] + mm(pm[c], d8[c]) for c in cs]
        yield
        m1 = [jnp.where(same32 & jnp.logical_not(same16), a, 0.0) for a in a_m]
        r = [mm(t0[c], m1[c]) for c in cs]
        yield
        t1 = [t0[c] - mm(r[c], t0[c]) for c in cs]
        yield
        m2 = [jnp.where(same32, 0.0, a) for a in a_m]
        r = [mm(t1[c], m2[c]) for c in cs]
        yield
        t2 = [(t1[c] - mm(r[c], t1[c])).astype(BF16) for c in cs]
        yield
        u_c = [_dot(t2[c], bd(vb[c])) for c in cs]
        w_c = [_dot(t2[c], bd(kb[c] * egc[c])) for c in cs]
        yield

        kg = [(k_c[c] * ekg[c]).astype(BF16) for c in cs]
        nmat = [jnp.where(same_head, _dot_tn(w_c[c].astype(BF16), kg[c]), 0.0).astype(BF16) for c in cs]
        bfull = [jnp.where(same_head, _dot_tn(u_c[c].astype(BF16), kg[c]), 0.0) for c in cs]
        bcomp = [b[0:64] + b[64:128] + b[128:192] + b[192:256] for b in bfull]
        yield
        qkb = [q.astype(BF16) for q in qk]
        qt = [(q_c[c] * egc[c] - _dot(qkb[c], bd(w_c[c]))).astype(BF16) for c in cs]
        ou = [_dot(qkb[c], bd(u_c[c])) for c in cs]
        yield

        st = st_ref[...]
        for c in cs:
            o_c = _dot_nt(qt[c], bd(st)) + ou[c]
            st = st * egl[c] - _dot(st.astype(BF16), nmat[c]) + bcomp[c]
            if c == cs[-1]:
                st_ref[...] = st
            yield
            o_n = o_c * lax.rsqrt(head_mean(o_c * o_c) + EPS) * out_gain
            zrows = slice(HALO + c * DN_CHUNK, HALO + (c + 1) * DN_CHUNK)
            mix_t[sls[c], 256:512] = (o_n * gate_silu(tile, sls[c], 256, 512)).astype(BF16)

    def convs(z_ref, tile):
        mix_t, ybuf_t, ysh_t, yc_t = mix_ref.at[tile], ybuf.at[tile], ysh.at[tile], yc_ref.at[tile]
        for c in range(ts // C_CHUNK):
            sl = slice(c * C_CHUNK, (c + 1) * C_CHUNK)
            zrows = slice(HALO + c * C_CHUNK, HALO + (c + 1) * C_CHUNK)
            cu = z_ref[zrows, COL_C:COL_C + 256]
            cv = z_ref[zrows, COL_C + 256:COL_C + 512]
            mu = jnp.mean(cv, axis=-1, keepdims=True)
            dv = cv - mu
            var = jnp.mean(dv * dv, axis=-1, keepdims=True)
            cvn = dv * lax.rsqrt(var + EPS) * vec_ref[1:2, 256:512] + vec_ref[1:2, 512:768]
            yield
            vblk = jnp.concatenate([jnp.where(head_mask[g], cvn, 0.0).astype(BF16) for g in range(4)], axis=0)
            zc = _dot(cw, vblk) + cbias
            mix_t[sl, 512:768] = (cu * zc * gate_silu(tile, sl, 512, 768)).astype(BF16)
            yield

        for r in range(1, 8):
            ysh_t[r - 1] = ybuf_t[r:r + ts + HALO - 8, :]
            if r % 2 == 1:
                yield
        for rb in range(0, ts, CONV_ROWS):
            acc = None
            for kk in range(D_CONV):
                r0 = HALO - (D_CONV - 1) + kk
                a8, r = (r0 // 8) * 8 + rb, r0 % 8
                win = ybuf_t[a8:a8 + CONV_ROWS, :] if r == 0 else ysh_t[r - 1, a8:a8 + CONV_ROWS, :]
                term = win * wd[kk:kk + 1, :]
                acc = term if acc is None else acc + term
            yc_t[rb:rb + CONV_ROWS, :] = acc + vec_ref[1:2, 768:1024]
            yield
        for c in range(ts // C_CHUNK):
            sl = slice(c * C_CHUNK, (c + 1) * C_CHUNK)
            zrows = slice(HALO + c * C_CHUNK, HALO + (c + 1) * C_CHUNK)
            yc = yc_t[sl, :]
            mu = head_mean(yc)
            dy = yc - mu
            var = head_mean(dy * dy)
            yn = dy * lax.rsqrt(var + EPS) * vec_ref[2:3, 0:256] + vec_ref[2:3, 256:512]
            mix_t[sl, 768:1024] = (_silu(yn) * gate_silu(tile, sl, 768, 1024)).astype(BF16)
            yield

    def out_proj(tile):
        mix_t, x1_t, hg_t = mix_ref.at[tile], x1_ref.at[tile], hg_ref.at[tile]
        rows = slice(tile * ts, (tile + 1) * ts)
        for c0 in range(0, D_MODEL, PROJ_CHUNK):
            cols = slice(c0, c0 + PROJ_CHUNK)
            x1_t[:, cols] = x_ref[0, rows, cols] + _dot(mix_t[...], w_out_ref[:, cols])
            yield
        for r0 in range(0, ts, 64):
            x1 = x1_t[r0:r0 + 64, :]
            ms1 = jnp.mean(x1 * x1, axis=-1, keepdims=True)
            hg_t[r0:r0 + 64, :] = (x1 * lax.rsqrt(ms1 + EPS)).astype(BF16)
        yield
        pb = p_ref[0, rows, :].astype(BF16)
        for c0 in range(0, D_MODEL, PROJ_CHUNK):
            cols = slice(c0, c0 + PROJ_CHUNK)
            ple_gate = _sigmoid(_dot(hg_t[...], gate_w_ref[:, cols]))
            pe = _dot(pb, ple_w_ref[:, cols])
            o_ref[0, rows, cols] = x1_t[:, cols] + ple_gate * pe
            yield

    @pl.when(sp == 0)
    def _():
        za_ref[0:HALO, :] = jnp.zeros((HALO, Z_WIDTH), F32)
        kbuf[:, 0:WINDOW, :] = jnp.zeros((2, WINDOW, 128), BF16)
        vbuf[:, 0:WINDOW, :] = jnp.zeros((2, WINDOW, 128), BF16)
        st_ref[...] = jnp.zeros((HEAD_DIM, 256), F32)
        ybuf[0, 0:HALO, :] = jnp.zeros((HALO, GROUP_WIDTH), F32)

    @pl.when(step == 0)
    def _():
        _run_streams([(in_proj(lambda r0, n: x_ref[0, r0:r0 + n, :], za_ref, 0), 1)])

    zb_ref[0:HALO, :] = za_ref[ts:ts + HALO, :]
    ybuf[1, 0:HALO, :] = ybuf[0, ts:ts + HALO, :]
    _run_streams([(deltanet(za_ref, 0), 1), (convs(za_ref, 0), 1), (attention(za_ref, 0), 2),
                  (in_proj(lambda r0, n: x_ref[0, ts + r0:ts + r0 + n, :], zb_ref, 1), 2)])
    za_ref[0:HALO, :] = zb_ref[ts:ts + HALO, :]
    ybuf[0, 0:HALO, :] = ybuf[1, ts:ts + HALO, :]
    _run_streams([(deltanet(zb_ref, 1), 1), (convs(zb_ref, 1), 1), (attention(zb_ref, 1), 2),
                  (in_proj(lambda r0, n: xn_ref[0, r0:r0 + n, :], za_ref, 0), 2), (out_proj(0), 2)])
    _run_streams([(out_proj(1), 1)])


def _layer_spec(arr, layer):
    return pl.BlockSpec((None,) + arr.shape[1:], lambda g: (layer,) + (0,) * (arr.ndim - 1),
                        pipeline_mode=pl.Buffered(1))


def _layer_call(x, p, layer, cos_t, sin_t, sink, w_in, w_mid, w_tail, w_out, ple_w, gate_w, vec, bconv, cw, cbias,
                dconv, ts):
    bsz, seq, _ = x.shape
    tiles_per_seq = seq // ts
    steps_per_seq = tiles_per_seq // 2
    n_steps = bsz * steps_per_seq
    last_tile = bsz * tiles_per_seq - 1

    def pair_map(g):
        return (g // steps_per_seq, g % steps_per_seq, 0)

    def next_tile_map(g):
        t = jnp.minimum(2 * g + 2, last_tile)
        return (t // tiles_per_seq, t % tiles_per_seq, 0)

    in_specs = [
        pl.BlockSpec(memory_space=pltpu.SMEM),
        pl.BlockSpec((1, 2 * ts, D_MODEL), pair_map),
        pl.BlockSpec((1, ts, D_MODEL), next_tile_map),
        pl.BlockSpec((None, 1, 2 * ts, D_PLE), lambda g: (layer,) + pair_map(g)),
        pl.BlockSpec((2 * ts, 128), lambda g: (g % steps_per_seq, 0)),
        pl.BlockSpec((2 * ts, 128), lambda g: (g % steps_per_seq, 0)),
    ] + [_layer_spec(a, layer) for a in (w_in, w_mid, w_tail, w_out, ple_w, gate_w, vec, bconv, cw, cbias, dconv)]
    scratch = [
        pltpu.VMEM((HALO + ts, Z_WIDTH), F32),
        pltpu.VMEM((HALO + ts, Z_WIDTH), F32),
        pltpu.VMEM((ts, D_MODEL), BF16),
        pltpu.VMEM((2, WINDOW + ts, 128), BF16),
        pltpu.VMEM((2, WINDOW + ts, 128), BF16),
        pltpu.VMEM((HEAD_DIM, 256), F32),
        pltpu.VMEM((2, ts, D_MODEL), BF16),
        pltpu.VMEM((2, HALO + ts, GROUP_WIDTH), F32),
        pltpu.VMEM((2, 7, HALO + ts - 8, GROUP_WIDTH), F32),
        pltpu.VMEM((2, ts, GROUP_WIDTH), F32),
        pltpu.VMEM((2, ts, D_MODEL), F32),
        pltpu.VMEM((2, ts, D_MODEL), BF16),
        pltpu.VMEM((2, ts, D_MODEL), F32),
    ]
    return pl.pallas_call(
        functools.partial(_layer_kernel, ts=ts, steps_per_seq=steps_per_seq, layer=layer),
        grid=(n_steps,),
        in_specs=in_specs,
        out_specs=pl.BlockSpec((1, 2 * ts, D_MODEL), pair_map),
        out_shape=jax.ShapeDtypeStruct(x.shape, F32),
        scratch_shapes=scratch,
        compiler_params=pltpu.CompilerParams(
            dimension_semantics=("arbitrary",),
            vmem_limit_bytes=VMEM_LIMIT_BYTES,
        ),
        name="hybrid_trunk_layer",
    )(sink, x, x, p, cos_t, sin_t, w_in, w_mid, w_tail, w_out, ple_w, gate_w, vec, bconv, cw, cbias, dconv)


W_PREP_ROWS = 256


def _w_in_prep_kernel(w_ref, g_ref, head_ref, mid_ref, tail_ref):
    g = g_ref[...]
    head_ref[...] = (w_ref[:, 0:W_IN_SPLIT] * g).astype(BF16)
    mid_ref[...] = (w_ref[:, W_IN_SPLIT + 2 * N_HEADS:] * g).astype(BF16)
    lane = lax.broadcasted_iota(jnp.int32, (1, 128), 1)
    tail_ref[...] = jnp.where(lane < 2 * N_HEADS, w_ref[:, W_IN_SPLIT:W_IN_SPLIT + 128] * g, 0.0).astype(BF16)


def _w_in_prep(w_in, gain):
    depth, d_in, width = w_in.shape
    mid = width - W_IN_SPLIT - 2 * N_HEADS

    def spec(cols):
        return pl.BlockSpec((None, W_PREP_ROWS, cols), lambda l, r: (l, r, 0))

    return pl.pallas_call(
        _w_in_prep_kernel,
        grid=(depth, d_in // W_PREP_ROWS),
        in_specs=[spec(width), spec(1)],
        out_specs=[spec(W_IN_SPLIT), spec(mid), spec(128)],
        out_shape=[jax.ShapeDtypeStruct((depth, d_in, c), BF16) for c in (W_IN_SPLIT, mid, 128)],
        name="w_in_prep",
    )(w_in, gain[:, :, None])


def kernel(x, p, norm_gain, w_in, w_out, a_q_gain, a_k_gain, a_sink, b_conv, b_a_log, b_dt_bias, b_out_gain,
           c_ln_gain, c_ln_bias, c_w_s, c_b_s, d_conv, d_conv_bias, d_ln_gain, d_ln_bias,
           ple_w, ple_gate_norm, ple_gate_w):
    depth = w_in.shape[0]
    bsz, seq, _ = x.shape
    ts = min(SEQ_TILE, seq)
    assert seq % (2 * ts) == 0 and ts % WINDOW == 0

    inv = ROPE_THETA ** (-jnp.arange(0, HEAD_DIM, 2, dtype=F32) / HEAD_DIM)
    ang_b = (jnp.arange(seq // WINDOW, dtype=F32) * WINDOW)[:, None] * inv[None, :]
    ang_r = jnp.arange(WINDOW, dtype=F32)[:, None] * inv[None, :]
    cb, sb, cr, sr = jnp.cos(ang_b)[:, None, :], jnp.sin(ang_b)[:, None, :], jnp.cos(ang_r)[None], jnp.sin(ang_r)[None]
    cos = (cb * cr - sb * sr).reshape(seq, HEAD_DIM // 2)
    sin = (sb * cr + cb * sr).reshape(seq, HEAD_DIM // 2)
    cos_t = jnp.concatenate([cos, cos, cos, cos], axis=-1)
    sin_t = jnp.concatenate([-sin, sin, -sin, sin], axis=-1)

    w_head, w_mid, w_tail = _w_in_prep(w_in, norm_gain)
    gate_w_b = (ple_gate_w * ple_gate_norm[:, :, None]).astype(BF16)
    w_out_b, ple_w_b = w_out.astype(BF16), ple_w.astype(BF16)

    def z(n):
        return jnp.zeros((depth, n), F32)

    def rep4(v, n):
        return jnp.broadcast_to(v[:, None, :], (depth, n, v.shape[1])).reshape(depth, n * v.shape[1])

    vec = jnp.concatenate([
        rep4(a_q_gain * (HEAD_DIM ** -0.5), 4), rep4(a_k_gain, 2), z(128),
        z(N_HEADS), b_a_log, z(128 - 2 * N_HEADS), z(N_HEADS), b_dt_bias, z(128 - 2 * N_HEADS), z(256),
        rep4(b_out_gain, 4), c_ln_gain, c_ln_bias, d_conv_bias,
        d_ln_gain, d_ln_bias, z(512), z(5 * D_MODEL)], axis=1).astype(F32).reshape(depth, 8, D_MODEL)
    cw = jnp.transpose(c_w_s, (0, 2, 1, 3)).reshape(depth, C_CHUNK, 4 * C_CHUNK)
    cbias = jnp.repeat(jnp.swapaxes(c_b_s, 1, 2), GROUP_WIDTH // 4, axis=2)
    sink = a_sink.astype(F32)

    for i in range(depth):
        x = _layer_call(x, p, i, cos_t, sin_t, sink, w_head, w_mid, w_tail, w_out_b, ple_w_b, gate_w_b, vec,
                        b_conv, cw, cbias, d_conv, ts)
    return x
```

```python
import functools

import jax
import jax.numpy as jnp
from jax import lax
from jax.experimental import pallas as pl
from jax.experimental.pallas import tpu as pltpu

F32 = jnp.float32
BF16 = jnp.bfloat16

EPS = 1e-6
D_MODEL = 1024
D_PLE = 256
GROUP_WIDTH = 256
HEAD_DIM = 64
N_HEADS = 4
WINDOW = 128
ROPE_THETA = 10000.0
B_CONV = 4
DN_CHUNK = 64
C_CHUNK = 128
D_CONV = 31

COL_AQ, COL_AK, COL_AV = 0, 256, 384
COL_B = 512
COL_C = 1280
COL_D = 1792
COL_GATE = 2304
COL_TAIL = 3328
Z_WIDTH = 3456
W_IN_SPLIT = 1280
HALO = 32

SEQ_TILE = 256
PROJ_CHUNK = 256
CONV_ROWS = 32
VMEM_LIMIT_BYTES = 56 * 1024 * 1024


def _sigmoid(x):
    return 0.5 + 0.5 * jnp.tanh(0.5 * x)


def _silu(x):
    h = 0.5 * x
    return h + h * jnp.tanh(h)


def _dot(a, b):
    return jnp.dot(a, b, preferred_element_type=F32)


def _dot_nt(a, b):
    return lax.dot_general(a, b, (((1,), (1,)), ((), ())), preferred_element_type=F32)


def _dot_tn(a, b):
    return lax.dot_general(a, b, (((0,), (0,)), ((), ())), preferred_element_type=F32)


def _split_lhs_dot(a, b_bf16, n):
    acc = None
    r = a
    for i in range(n):
        hi = r.astype(BF16)
        t = _dot(hi, b_bf16)
        acc = t if acc is None else acc + t
        if i + 1 < n:
            r = r - hi.astype(F32)
    return acc


def _split_rhs_dot(a_bf16, b, n):
    acc = None
    r = b
    for i in range(n):
        hi = r.astype(BF16)
        t = _dot(a_bf16, hi)
        acc = t if acc is None else acc + t
        if i + 1 < n:
            r = r - hi.astype(F32)
    return acc


def _run_streams(streams):
    live = list(streams)
    rnd = 0
    while live:
        for item in list(live):
            gen, period = item
            if rnd % period == 0 and next(gen, StopIteration) is StopIteration:
                live.remove(item)
        rnd += 1


def _layer_kernel(sink_ref, x_ref, xn_ref, p_ref, cos_ref, sin_ref, w1_ref, w2_ref, w3_ref, w_out_ref, ple_w_ref,
                  gate_w_ref, vec_ref, bconv_ref, cw_ref, cbias_ref, dconv_ref, o_ref,
                  za_ref, zb_ref, h_ref, kbuf, vbuf, st_ref, mix_ref, ybuf, ysh, yc_ref, x1_ref, hg_ref, sg_ref,
                  *, ts, steps_per_seq, layer):
    step = pl.program_id(0)
    sp = step % steps_per_seq

    lane256 = lax.broadcasted_iota(jnp.int32, (1, 256), 1)
    head_mask = [(lane256 // HEAD_DIM) == h for h in range(N_HEADS)]
    r256 = lax.broadcasted_iota(jnp.int32, (256, 256), 0)
    c256 = lax.broadcasted_iota(jnp.int32, (256, 256), 1)
    same_head = (r256 // HEAD_DIM) == (c256 // HEAD_DIM)
    g_sum = jnp.where(same_head, 1.0, 0.0).astype(BF16)
    g_mean = jnp.where(same_head, 1.0 / HEAD_DIM, 0.0).astype(BF16)
    lane128 = lax.broadcasted_iota(jnp.int32, (1, 128), 1)
    first_half = (lane128 % HEAD_DIM) < (HEAD_DIM // 2)
    lo128 = lane128 < HEAD_DIM
    q_gain = vec_ref[0:1, 0:256]
    k_gain = vec_ref[0:1, 256:384]
    qi = lax.broadcasted_iota(jnp.int32, (WINDOW, 2 * WINDOW), 0)
    kj = lax.broadcasted_iota(jnp.int32, (WINDOW, 2 * WINDOW), 1)
    rel = qi + WINDOW - kj
    band = (rel >= 0) & (rel < WINDOW)
    wb = bconv_ref[...]
    er = lax.broadcasted_iota(jnp.int32, (128, 256), 0)
    ec = lax.broadcasted_iota(jnp.int32, (128, 256), 1) // HEAD_DIM
    e_beta = jnp.where(er == ec, 1.0, 0.0).astype(BF16)
    e_alpha = jnp.where(er == ec + N_HEADS, 1.0, 0.0).astype(BF16)
    tr = lax.broadcasted_iota(jnp.int32, (ts, ts), 0)
    tc = lax.broadcasted_iota(jnp.int32, (ts, ts), 1)
    l_bd = jnp.where(((tr // DN_CHUNK) == (tc // DN_CHUNK)) & (tr >= tc), 1.0, 0.0).astype(BF16)
    ci = lax.broadcasted_iota(jnp.int32, (DN_CHUNK, 256), 0)
    cj = lax.broadcasted_iota(jnp.int32, (DN_CHUNK, 256), 1) % HEAD_DIM
    eye = ci == cj
    tril = ci >= cj
    strict = ci > cj
    same16 = (ci // 16) == (cj // 16)
    same32 = (ci // 32) == (cj // 32)
    eye_f = jnp.where(eye, 1.0, 0.0)
    out_gain = vec_ref[1:2, 0:256]
    wi = lax.broadcasted_iota(jnp.int32, (C_CHUNK, 4 * C_CHUNK), 0)
    wj = lax.broadcasted_iota(jnp.int32, (C_CHUNK, 4 * C_CHUNK), 1) % C_CHUNK
    cw = jnp.where(wi >= wj, cw_ref[...], 0.0).astype(BF16)
    cbias = cbias_ref[...]
    wd = dconv_ref[...]

    def head_mean(v, n=1):
        return _split_lhs_dot(v, g_mean, n)

    def head_sum(v, n=1):
        return _split_lhs_dot(v, g_sum, n)

    def bd(xv):
        xb = xv.astype(BF16)
        zero = jnp.zeros_like(xb)
        return jnp.concatenate([jnp.where(head_mask[h], xb, zero) for h in range(N_HEADS)], axis=0)

    def mm(lhs, xv):
        return _dot(lhs.astype(BF16), bd(xv))

    def rope(v, cos, sin):
        sw = jnp.where(first_half, pltpu.roll(v, 128 - HEAD_DIM // 2, 1), pltpu.roll(v, HEAD_DIM // 2, 1))
        return v * cos + sw * sin

    proj_chunks = ([(w1_ref, c, c, PROJ_CHUNK) for c in range(0, W_IN_SPLIT, PROJ_CHUNK)]
                   + [(w2_ref, c, W_IN_SPLIT + c, PROJ_CHUNK) for c in range(0, COL_TAIL - W_IN_SPLIT, PROJ_CHUNK)]
                   + [(w3_ref, 0, COL_TAIL, Z_WIDTH - COL_TAIL)])

    def in_proj(x_rows, z_ref, slot):
        for r0 in range(0, ts, 64):
            xv = x_rows(r0, 64)
            ms = jnp.mean(xv * xv, axis=-1, keepdims=True)
            h_ref[r0:r0 + 64, :] = (xv * lax.rsqrt(ms + EPS)).astype(BF16)
        for w_ref, wc, zc, width in proj_chunks:
            zv = _dot(h_ref[...], w_ref[:, wc:wc + width])
            if COL_C <= zc < COL_D:
                zv = 0.5 * zv * (1.0 + jnp.tanh(0.7978845608028654 * (zv + 0.044715 * (zv * zv * zv))))
            if zc == COL_D + 256:
                ybuf[slot, HALO:HALO + ts, :] = z_ref[HALO:HALO + ts, COL_D:COL_D + 256] * _sigmoid(zv)
            elif zc >= COL_GATE and zc < COL_TAIL:
                sg_ref[slot, :, zc - COL_GATE:zc - COL_GATE + width] = _silu(zv)
            else:
                z_ref[HALO:HALO + ts, zc:zc + width] = zv
            yield

    def gate_silu(tile, rows, c0, c1):
        return sg_ref[tile, rows, c0:c1]

    def attention(z_ref, tile):
        mix_t = mix_ref.at[tile]
        for blk in range(ts // WINDOW):
            rows = slice(HALO + blk * WINDOW, HALO + (blk + 1) * WINDOW)
            q = z_ref[rows, COL_AQ:COL_AQ + 256]
            k = z_ref[rows, COL_AK:COL_AK + 128]
            v = z_ref[rows, COL_AV:COL_AV + 128]
            cos = cos_ref[tile * ts + blk * WINDOW:tile * ts + (blk + 1) * WINDOW, :]
            sin = sin_ref[tile * ts + blk * WINDOW:tile * ts + (blk + 1) * WINDOW, :]
            qn = q * lax.rsqrt(head_mean(q * q) + EPS) * q_gain
            kn = k * lax.rsqrt(_split_lhs_dot(k * k, g_mean[0:128, 0:128], 1) + EPS) * k_gain
            qa = rope(qn[:, 0:128], cos, sin)
            qb = rope(qn[:, 128:256], cos, sin)
            kr = rope(kn, cos, sin)
            cur = slice(WINDOW + blk * WINDOW, 2 * WINDOW + blk * WINDOW)
            both = slice(blk * WINDOW, (blk + 2) * WINDOW)
            kbuf[0, cur, :] = kr.astype(BF16)
            kbuf[1, cur, :] = pltpu.roll(kr, HEAD_DIM, 1).astype(BF16)
            vbuf[0, cur, :] = v.astype(BF16)
            vbuf[1, cur, :] = pltpu.roll(v, HEAD_DIM, 1).astype(BF16)
            lhs = [jnp.where(lo128, qa, 0.0).astype(BF16), jnp.where(lo128, 0.0, qa).astype(BF16),
                   jnp.where(lo128, qb, 0.0).astype(BF16), jnp.where(lo128, 0.0, qb).astype(BF16)]
            yield
            valid = band
            if blk == 0 and tile == 0:
                valid = valid & (kj >= jnp.where(sp > 0, 0, WINDOW))
            outs = []
            for r in range(N_HEADS):
                slot = 0 if r in (0, 3) else 1
                sc = jnp.where(valid, _dot_nt(lhs[r], kbuf[slot, both, :]), -jnp.inf)
                sink = sink_ref[layer, r]
                m = jnp.maximum(jnp.max(sc, axis=-1, keepdims=True), sink)
                e = jnp.exp(sc - m)
                den = jnp.sum(e, axis=-1, keepdims=True) + jnp.exp(sink - m)
                outs.append(_dot(e.astype(BF16), vbuf[slot, both, :]) / den)
                yield
            o_a0 = jnp.where(lo128, outs[0], outs[1])
            o_a1 = jnp.where(lo128, outs[2], outs[3])
            mrows = slice(blk * WINDOW, (blk + 1) * WINDOW)
            mix_t[mrows, 0:128] = (o_a0 * gate_silu(tile, mrows, 0, 128)).astype(BF16)
            mix_t[mrows, 128:256] = (o_a1 * gate_silu(tile, mrows, 128, 256)).astype(BF16)
            yield
        kbuf[:, 0:WINDOW, :] = kbuf[:, ts:ts + WINDOW, :]
        vbuf[:, 0:WINDOW, :] = vbuf[:, ts:ts + WINDOW, :]

    def deltanet(z_ref, tile):
        mix_t = mix_ref.at[tile]
        cs = range(ts // DN_CHUNK)
        sls = [slice(c * DN_CHUNK, (c + 1) * DN_CHUNK) for c in cs]
        q_c, k_c, v_c = [], [], []
        for c in cs:
            r0 = HALO + c * DN_CHUNK - (B_CONV - 1)
            acc = None
            for kk in range(B_CONV):
                term = z_ref[r0 + kk:r0 + kk + DN_CHUNK, COL_B:COL_B + 768] * wb[kk:kk + 1, :]
                acc = term if acc is None else acc + term
            qkv = _silu(acc)
            q, k = qkv[:, 0:256], qkv[:, 256:512]
            ss = head_sum(jnp.concatenate([q * q, k * k], axis=0))
            q_c.append(q * lax.rsqrt(ss[0:DN_CHUNK] + EPS) * (HEAD_DIM ** -0.5))
            k_c.append(k * lax.rsqrt(ss[DN_CHUNK:] + EPS))
            v_c.append(qkv[:, 512:768])
            yield

        tail = z_ref[HALO:HALO + ts, COL_TAIL:COL_TAIL + 128]
        beta = _sigmoid(tail)
        xg = tail + vec_ref[0:1, 640:768]
        softplus = jnp.maximum(xg, 0.0) + jnp.log1p(jnp.exp(-jnp.abs(xg)))
        gdec = -jnp.exp(vec_ref[0:1, 512:640]) * softplus
        beta_f = _split_lhs_dot(beta, e_beta, 1)
        gdec_f = _split_lhs_dot(gdec, e_alpha, 2)
        gc = _split_rhs_dot(l_bd, gdec_f, 2)
        yield

        gc_c = [gc[s] for s in sls]
        gl = [g[DN_CHUNK - 1:DN_CHUNK, :] for g in gc_c]
        egc = [jnp.exp(g) for g in gc_c]
        ekg = [jnp.exp(gl[c] - gc_c[c]) for c in cs]
        egl = [jnp.exp(g) for g in gl]
        kb = [k_c[c] * beta_f[sls[c]] for c in cs]
        vb = [v_c[c] * beta_f[sls[c]] for c in cs]
        grow = [jnp.sum(jnp.where(eye, g, 0.0), axis=0, keepdims=True) for g in gc_c]
        decay = [jnp.exp(jnp.minimum(gc_c[c] - grow[c], 0.0)) for c in cs]
        yield
        aq = [_dot_nt(jnp.concatenate([kb[c], q_c[c]], axis=0).astype(BF16), bd(k_c[c])) for c in cs]
        a_m = [jnp.where(strict, aq[c][0:DN_CHUNK] * decay[c], 0.0) for c in cs]
        qk = [jnp.where(tril, aq[c][DN_CHUNK:2 * DN_CHUNK] * decay[c], 0.0) for c in cs]
        yield

        d1 = [jnp.where(same16, a, 0.0) for a in a_m]
        pm = [eye_f - d for d in d1]
        d2 = [mm(d, d) for d in d1]
        yield
        r = [mm(jnp.concatenate([pm[c], d2[c]], axis=0), d2[c]) for c in cs]
        pm = [pm[c] + r[c][0:DN_CHUNK] for c in cs]
        d4 = [r[c][DN_CHUNK:] for c in cs]
        yield
        r = [mm(jnp.concatenate([pm[c], d4[c]], axis=0), d4[c]) for c in cs]
        pm = [pm[c] + r[c][0:DN_CHUNK] for c in cs]
        d8 = [r[c][DN_CHUNK:] for c in cs]
        yield
        t0 = [pm[c] + mm(pm[c], d8[c]) for c in cs]
        yield
        m1 = [jnp.where(same32 & jnp.logical_not(same16), a, 0.0) for a in a_m]
        r = [mm(t0[c], m1[c]) for c in cs]
        yield
        t1 = [t0[c] - mm(r[c], t0[c]) for c in cs]
        yield
        m2 = [jnp.where(same32, 0.0, a) for a in a_m]
        r = [mm(t1[c], m2[c]) for c in cs]
        yield
        t2 = [(t1[c] - mm(r[c], t1[c])).astype(BF16) for c in cs]
        yield
        u_c = [_dot(t2[c], bd(vb[c])) for c in cs]
        w_c = [_dot(t2[c], bd(kb[c] * egc[c])) for c in cs]
        yield

        kg = [(k_c[c] * ekg[c]).astype(BF16) for c in cs]
        nmat = [jnp.where(same_head, _dot_tn(w_c[c].astype(BF16), kg[c]), 0.0).astype(BF16) for c in cs]
        bfull = [jnp.where(same_head, _dot_tn(u_c[c].astype(BF16), kg[c]), 0.0) for c in cs]
        bcomp = [b[0:64] + b[64:128] + b[128:192] + b[192:256] for b in bfull]
        yield
        qkb = [q.astype(BF16) for q in qk]
        qt = [(q_c[c] * egc[c] - _dot(qkb[c], bd(w_c[c]))).astype(BF16) for c in cs]
        ou = [_dot(qkb[c], bd(u_c[c])) for c in cs]
        yield

        st = st_ref[...]
        for c in cs:
            o_c = _dot_nt(qt[c], bd(st)) + ou[c]
            st = st * egl[c] - _dot(st.astype(BF16), nmat[c]) + bcomp[c]
            if c == cs[-1]:
                st_ref[...] = st
            yield
            o_n = o_c * lax.rsqrt(head_mean(o_c * o_c) + EPS) * out_gain
            zrows = slice(HALO + c * DN_CHUNK, HALO + (c + 1) * DN_CHUNK)
            mix_t[sls[c], 256:512] = (o_n * gate_silu(tile, sls[c], 256, 512)).astype(BF16)

    def convs(z_ref, tile):
        mix_t, ybuf_t, ysh_t, yc_t = mix_ref.at[tile], ybuf.at[tile], ysh.at[tile], yc_ref.at[tile]
        for c in range(ts // C_CHUNK):
            sl = slice(c * C_CHUNK, (c + 1) * C_CHUNK)
            zrows = slice(HALO + c * C_CHUNK, HALO + (c + 1) * C_CHUNK)
            cu = z_ref[zrows, COL_C:COL_C + 256]
            cv = z_ref[zrows, COL_C + 256:COL_C + 512]
            mu = jnp.mean(cv, axis=-1, keepdims=True)
            dv = cv - mu
            var = jnp.mean(dv * dv, axis=-1, keepdims=True)
            cvn = dv * lax.rsqrt(var + EPS) * vec_ref[1:2, 256:512] + vec_ref[1:2, 512:768]
            yield
            vblk = jnp.concatenate([jnp.where(head_mask[g], cvn, 0.0).astype(BF16) for g in range(4)], axis=0)
            zc = _dot(cw, vblk) + cbias
            mix_t[sl, 512:768] = (cu * zc * gate_silu(tile, sl, 512, 768)).astype(BF16)
            yield

        for r in range(1, 8):
            ysh_t[r - 1] = ybuf_t[r:r + ts + HALO - 8, :]
            if r % 2 == 1:
                yield
        for rb in range(0, ts, CONV_ROWS):
            acc = None
            for kk in range(D_CONV):
                r0 = HALO - (D_CONV - 1) + kk
                a8, r = (r0 // 8) * 8 + rb, r0 % 8
                win = ybuf_t[a8:a8 + CONV_ROWS, :] if r == 0 else ysh_t[r - 1, a8:a8 + CONV_ROWS, :]
                term = win * wd[kk:kk + 1, :]
                acc = term if acc is None else acc + term
            yc_t[rb:rb + CONV_ROWS, :] = acc + vec_ref[1:2, 768:1024]
            yield
        for c in range(ts // C_CHUNK):
            sl = slice(c * C_CHUNK, (c + 1) * C_CHUNK)
            zrows = slice(HALO + c * C_CHUNK, HALO + (c + 1) * C_CHUNK)
            yc = yc_t[sl, :]
            mu = head_mean(yc)
            dy = yc - mu
            var = head_mean(dy * dy)
            yn = dy * lax.rsqrt(var + EPS) * vec_ref[2:3, 0:256] + vec_ref[2:3, 256:512]
            mix_t[sl, 768:1024] = (_silu(yn) * gate_silu(tile, sl, 768, 1024)).astype(BF16)
            yield

    def out_proj(tile):
        mix_t, x1_t, hg_t = mix_ref.at[tile], x1_ref.at[tile], hg_ref.at[tile]
        rows = slice(tile * ts, (tile + 1) * ts)
        for c0 in range(0, D_MODEL, PROJ_CHUNK):
            cols = slice(c0, c0 + PROJ_CHUNK)
            x1_t[:, cols] = x_ref[0, rows, cols] + _dot(mix_t[...], w_out_ref[:, cols])
            yield
        for r0 in range(0, ts, 64):
            x1 = x1_t[r0:r0 + 64, :]
            ms1 = jnp.mean(x1 * x1, axis=-1, keepdims=True)
            hg_t[r0:r0 + 64, :] = (x1 * lax.rsqrt(ms1 + EPS)).astype(BF16)
        yield
        pb = p_ref[0, rows, :].astype(BF16)
        for c0 in range(0, D_MODEL, PROJ_CHUNK):
            cols = slice(c0, c0 + PROJ_CHUNK)
            ple_gate = _sigmoid(_dot(hg_t[...], gate_w_ref[:, cols]))
            pe = _dot(pb, ple_w_ref[:, cols])
            o_ref[0, rows, cols] = x1_t[:, cols] + ple_gate * pe
            yield

    @pl.when(sp == 0)
    def _():
        za_ref[0:HALO, :] = jnp.zeros((HALO, Z_WIDTH), F32)
        kbuf[:, 0:WINDOW, :] = jnp.zeros((2, WINDOW, 128), BF16)
        vbuf[:, 0:WINDOW, :] = jnp.zeros((2, WINDOW, 128), BF16)
        st_ref[...] = jnp.zeros((HEAD_DIM, 256), F32)
        ybuf[0, 0:HALO, :] = jnp.zeros((HALO, GROUP_WIDTH), F32)

    @pl.when(step == 0)
    def _():
        _run_streams([(in_proj(lambda r0, n: x_ref[0, r0:r0 + n, :], za_ref, 0), 1)])

    zb_ref[0:HALO, :] = za_ref[ts:ts + HALO, :]
    ybuf[1, 0:HALO, :] = ybuf[0, ts:ts + HALO, :]
    _run_streams([(deltanet(za_ref, 0), 1), (convs(za_ref, 0), 1), (attention(za_ref, 0), 2),
                  (in_proj(lambda r0, n: x_ref[0, ts + r0:ts + r0 + n, :], zb_ref, 1), 2)])
    za_ref[0:HALO, :] = zb_ref[ts:ts + HALO, :]
    ybuf[0, 0:HALO, :] = ybuf[1, ts:ts + HALO, :]
    _run_streams([(deltanet(zb_ref, 1), 1), (convs(zb_ref, 1), 1), (attention(zb_ref, 1), 2),
                  (in_proj(lambda r0, n: xn_ref[0, r0:r0 + n, :], za_ref, 0), 2), (out_proj(0), 1)])
    _run_streams([(out_proj(1), 1)])


def _layer_spec(arr, layer):
    return pl.BlockSpec((None,) + arr.shape[1:], lambda g: (layer,) + (0,) * (arr.ndim - 1),
                        pipeline_mode=pl.Buffered(1))


def _layer_call(x, p, layer, cos_t, sin_t, sink, w_in, w_mid, w_tail, w_out, ple_w, gate_w, vec, bconv, cw, cbias,
                dconv, ts):
    bsz, seq, _ = x.shape
    tiles_per_seq = seq // ts
    steps_per_seq = tiles_per_seq // 2
    n_steps = bsz * steps_per_seq
    last_tile = bsz * tiles_per_seq - 1

    def pair_map(g):
        return (g // steps_per_seq, g % steps_per_seq, 0)

    def next_tile_map(g):
        t = jnp.minimum(2 * g + 2, last_tile)
        return (t // tiles_per_seq, t % tiles_per_seq, 0)

    in_specs = [
        pl.BlockSpec(memory_space=pltpu.SMEM),
        pl.BlockSpec((1, 2 * ts, D_MODEL), pair_map),
        pl.BlockSpec((1, ts, D_MODEL), next_tile_map),
        pl.BlockSpec((None, 1, 2 * ts, D_PLE), lambda g: (layer,) + pair_map(g)),
        pl.BlockSpec((2 * ts, 128), lambda g: (g % steps_per_seq, 0)),
        pl.BlockSpec((2 * ts, 128), lambda g: (g % steps_per_seq, 0)),
    ] + [_layer_spec(a, layer) for a in (w_in, w_mid, w_tail, w_out, ple_w, gate_w, vec, bconv, cw, cbias, dconv)]
    scratch = [
        pltpu.VMEM((HALO + ts, Z_WIDTH), F32),
        pltpu.VMEM((HALO + ts, Z_WIDTH), F32),
        pltpu.VMEM((ts, D_MODEL), BF16),
        pltpu.VMEM((2, WINDOW + ts, 128), BF16),
        pltpu.VMEM((2, WINDOW + ts, 128), BF16),
        pltpu.VMEM((HEAD_DIM, 256), F32),
        pltpu.VMEM((2, ts, D_MODEL), BF16),
        pltpu.VMEM((2, HALO + ts, GROUP_WIDTH), F32),
        pltpu.VMEM((2, 7, HALO + ts - 8, GROUP_WIDTH), F32),
        pltpu.VMEM((2, ts, GROUP_WIDTH), F32),
        pltpu.VMEM((2, ts, D_MODEL), F32),
        pltpu.VMEM((2, ts, D_MODEL), BF16),
        pltpu.VMEM((2, ts, D_MODEL), F32),
    ]
    return pl.pallas_call(
        functools.partial(_layer_kernel, ts=ts, steps_per_seq=steps_per_seq, layer=layer),
        grid=(n_steps,),
        in_specs=in_specs,
        out_specs=pl.BlockSpec((1, 2 * ts, D_MODEL), pair_map),
        out_shape=jax.ShapeDtypeStruct(x.shape, F32),
        scratch_shapes=scratch,
        compiler_params=pltpu.CompilerParams(
            dimension_semantics=("arbitrary",),
            vmem_limit_bytes=VMEM_LIMIT_BYTES,
        ),
        name="hybrid_trunk_layer",
    )(sink, x, x, p, cos_t, sin_t, w_in, w_mid, w_tail, w_out, ple_w, gate_w, vec, bconv, cw, cbias, dconv)


W_PREP_BLOCK = 256


def _w_in_prep_kernel(wt_ref, g_ref, head_ref, mid_ref, tail_ref):
    n = W_PREP_BLOCK
    eye = (lax.broadcasted_iota(jnp.int32, (n, n), 0) == lax.broadcasted_iota(jnp.int32, (n, n), 1)).astype(BF16)
    g = g_ref[...]
    d_in = g.shape[1]

    def transposed(r0, rows):
        wb = (wt_ref[r0:r0 + rows, :] * g).astype(BF16)
        return jnp.concatenate([_dot_nt(eye, wb[:, k:k + n]) for k in range(0, d_in, n)], axis=0).astype(BF16)

    for c in range(0, W_IN_SPLIT, n):
        head_ref[:, c:c + n] = transposed(c, n)
    mid0 = W_IN_SPLIT + 2 * N_HEADS
    for c in range(0, mid_ref.shape[1], n):
        mid_ref[:, c:c + n] = transposed(mid0 + c, n)
    lane = lax.broadcasted_iota(jnp.int32, (1, 128), 1)
    tail = transposed(W_IN_SPLIT, 128)
    tail_ref[...] = jnp.where(lane < 2 * N_HEADS, tail, jnp.zeros_like(tail))


def _w_in_prep(w_in_t, gain):
    depth, width, d_in = w_in_t.shape
    mid = width - W_IN_SPLIT - 2 * N_HEADS

    def spec(rows, cols):
        return pl.BlockSpec((None, rows, cols), lambda l: (l, 0, 0))

    return pl.pallas_call(
        _w_in_prep_kernel,
        grid=(depth,),
        in_specs=[spec(width, d_in), spec(1, d_in)],
        out_specs=[spec(d_in, W_IN_SPLIT), spec(d_in, mid), spec(d_in, 128)],
        out_shape=[jax.ShapeDtypeStruct((depth, d_in, c), BF16) for c in (W_IN_SPLIT, mid, 128)],
        compiler_params=pltpu.CompilerParams(vmem_limit_bytes=VMEM_LIMIT_BYTES),
        name="w_in_prep",
    )(w_in_t, gain[:, None, :])


def kernel(x, p, norm_gain, w_in, w_out, a_q_gain, a_k_gain, a_sink, b_conv, b_a_log, b_dt_bias, b_out_gain,
           c_ln_gain, c_ln_bias, c_w_s, c_b_s, d_conv, d_conv_bias, d_ln_gain, d_ln_bias,
           ple_w, ple_gate_norm, ple_gate_w):
    depth = w_in.shape[0]
    bsz, seq, _ = x.shape
    ts = min(SEQ_TILE, seq)
    assert seq % (2 * ts) == 0 and ts % WINDOW == 0

    inv = jnp.tile(ROPE_THETA ** (-jnp.arange(0, HEAD_DIM, 2, dtype=F32) / HEAD_DIM), 4)
    sign = jnp.tile(jnp.repeat(jnp.array([-1.0, 1.0], F32), HEAD_DIM // 2), 2)
    ang_b = (jnp.arange(seq // WINDOW, dtype=F32) * WINDOW)[:, None] * inv[None, :]
    ang_r = jnp.arange(WINDOW, dtype=F32)[:, None] * inv[None, :]
    cb, sb, cr, sr = jnp.cos(ang_b)[:, None, :], jnp.sin(ang_b)[:, None, :], jnp.cos(ang_r)[None], jnp.sin(ang_r)[None]
    cos_t = (cb * cr - sb * sr).reshape(seq, 128)
    sin_t = ((sb * cr + cb * sr) * sign).reshape(seq, 128)

    w_head, w_mid, w_tail = _w_in_prep(jnp.swapaxes(w_in, 1, 2), norm_gain)
    gate_w_b = (ple_gate_w * ple_gate_norm[:, :, None]).astype(BF16)
    w_out_b, ple_w_b = w_out.astype(BF16), ple_w.astype(BF16)

    def z(n):
        return jnp.zeros((depth, n), F32)

    def rep4(v, n):
        return jnp.broadcast_to(v[:, None, :], (depth, n, v.shape[1])).reshape(depth, n * v.shape[1])

    vec = jnp.concatenate([
        rep4(a_q_gain * (HEAD_DIM ** -0.5), 4), rep4(a_k_gain, 2), z(128),
        z(N_HEADS), b_a_log, z(128 - 2 * N_HEADS), z(N_HEADS), b_dt_bias, z(128 - 2 * N_HEADS), z(256),
        rep4(b_out_gain, 4), c_ln_gain, c_ln_bias, d_conv_bias,
        d_ln_gain, d_ln_bias, z(512), z(5 * D_MODEL)], axis=1).astype(F32).reshape(depth, 8, D_MODEL)
    cw = jnp.transpose(c_w_s, (0, 2, 1, 3)).reshape(depth, C_CHUNK, 4 * C_CHUNK)
    cbias = jnp.repeat(jnp.swapaxes(c_b_s, 1, 2), GROUP_WIDTH // 4, axis=2)
    sink = a_sink.astype(F32)

    for i in range(depth):
        x = _layer_call(x, p, i, cos_t, sin_t, sink, w_head, w_mid, w_tail, w_out_b, ple_w_b, gate_w_b, vec,
                        b_conv, cw, cbias, d_conv, ts)
    return x
```

```python
import functools

import jax
import jax.numpy as jnp
from jax import lax
from jax.experimental import pallas as pl
from jax.experimental.pallas import tpu as pltpu

F32 = jnp.float32
BF16 = jnp.bfloat16

EPS = 1e-6
D_MODEL = 1024
D_PLE = 256
GROUP_WIDTH = 256
HEAD_DIM = 64
N_HEADS = 4
WINDOW = 128
ROPE_THETA = 10000.0
B_CONV = 4
DN_CHUNK = 64
C_CHUNK = 128
D_CONV = 31

COL_AQ, COL_AK, COL_AV = 0, 256, 384
COL_B = 512
COL_C = 1280
COL_D = 1792
COL_GATE = 2304
COL_TAIL = 3328
Z_WIDTH = 3456
W_IN_SPLIT = 1280
HALO = 32

SEQ_TILE = 256
TILES_PER_STEP = 4
PROJ_CHUNK = 256
CONV_ROWS = 32
VMEM_LIMIT_BYTES = 60 * 1024 * 1024


def _sigmoid(x):
    return 0.5 + 0.5 * jnp.tanh(0.5 * x)


def _silu(x):
    h = 0.5 * x
    return h + h * jnp.tanh(h)


def _dot(a, b):
    return jnp.dot(a, b, preferred_element_type=F32)


def _dot_nt(a, b):
    return lax.dot_general(a, b, (((1,), (1,)), ((), ())), preferred_element_type=F32)


def _dot_tn(a, b):
    return lax.dot_general(a, b, (((0,), (0,)), ((), ())), preferred_element_type=F32)


def _split_lhs_dot(a, b_bf16, n):
    acc = None
    r = a
    for i in range(n):
        hi = r.astype(BF16)
        t = _dot(hi, b_bf16)
        acc = t if acc is None else acc + t
        if i + 1 < n:
            r = r - hi.astype(F32)
    return acc


def _split_rhs_dot(a_bf16, b, n):
    acc = None
    r = b
    for i in range(n):
        hi = r.astype(BF16)
        t = _dot(a_bf16, hi)
        acc = t if acc is None else acc + t
        if i + 1 < n:
            r = r - hi.astype(F32)
    return acc


def _run_streams(streams):
    live = list(streams)
    rnd = 0
    while live:
        for item in list(live):
            gen, period = item
            if rnd % period == 0 and next(gen, StopIteration) is StopIteration:
                live.remove(item)
        rnd += 1


def _layer_kernel(sink_ref, x_ref, xn_ref, p_ref, cos_ref, sin_ref, w1_ref, w2_ref, w3_ref, w_out_ref, ple_w_ref,
                  gate_w_ref, vec_ref, bconv_ref, cw_ref, cbias_ref, dconv_ref, o_ref,
                  za_ref, zb_ref, h_ref, kbuf, vbuf, st_ref, mix_ref, ybuf, ysh, yc_ref, x1_ref, hg_ref, sg_ref,
                  *, ts, steps_per_seq, layer):
    step = pl.program_id(0)
    sp = step % steps_per_seq

    lane256 = lax.broadcasted_iota(jnp.int32, (1, 256), 1)
    head_mask = [(lane256 // HEAD_DIM) == h for h in range(N_HEADS)]
    r256 = lax.broadcasted_iota(jnp.int32, (256, 256), 0)
    c256 = lax.broadcasted_iota(jnp.int32, (256, 256), 1)
    same_head = (r256 // HEAD_DIM) == (c256 // HEAD_DIM)
    g_sum = jnp.where(same_head, 1.0, 0.0).astype(BF16)
    g_mean = jnp.where(same_head, 1.0 / HEAD_DIM, 0.0).astype(BF16)
    lane128 = lax.broadcasted_iota(jnp.int32, (1, 128), 1)
    first_half = (lane128 % HEAD_DIM) < (HEAD_DIM // 2)
    lo128 = lane128 < HEAD_DIM
    q_gain = vec_ref[0:1, 0:256]
    k_gain = vec_ref[0:1, 256:384]
    qi = lax.broadcasted_iota(jnp.int32, (WINDOW, 2 * WINDOW), 0)
    kj = lax.broadcasted_iota(jnp.int32, (WINDOW, 2 * WINDOW), 1)
    rel = qi + WINDOW - kj
    band = (rel >= 0) & (rel < WINDOW)
    wb = bconv_ref[...]
    er = lax.broadcasted_iota(jnp.int32, (128, 256), 0)
    ec = lax.broadcasted_iota(jnp.int32, (128, 256), 1) // HEAD_DIM
    e_beta = jnp.where(er == ec, 1.0, 0.0).astype(BF16)
    e_alpha = jnp.where(er == ec + N_HEADS, 1.0, 0.0).astype(BF16)
    tr = lax.broadcasted_iota(jnp.int32, (ts, ts), 0)
    tc = lax.broadcasted_iota(jnp.int32, (ts, ts), 1)
    l_bd = jnp.where(((tr // DN_CHUNK) == (tc // DN_CHUNK)) & (tr >= tc), 1.0, 0.0).astype(BF16)
    ci = lax.broadcasted_iota(jnp.int32, (DN_CHUNK, 256), 0)
    cj = lax.broadcasted_iota(jnp.int32, (DN_CHUNK, 256), 1) % HEAD_DIM
    eye = ci == cj
    tril = ci >= cj
    strict = ci > cj
    same16 = (ci // 16) == (cj // 16)
    same32 = (ci // 32) == (cj // 32)
    eye_f = jnp.where(eye, 1.0, 0.0)
    out_gain = vec_ref[1:2, 0:256]
    wi = lax.broadcasted_iota(jnp.int32, (C_CHUNK, 4 * C_CHUNK), 0)
    wj = lax.broadcasted_iota(jnp.int32, (C_CHUNK, 4 * C_CHUNK), 1) % C_CHUNK
    cw = jnp.where(wi >= wj, cw_ref[...], 0.0).astype(BF16)
    cbias = cbias_ref[...]
    wd = dconv_ref[...]

    def head_mean(v, n=1):
        return _split_lhs_dot(v, g_mean, n)

    def head_sum(v, n=1):
        return _split_lhs_dot(v, g_sum, n)

    def bd(xv):
        xb = xv.astype(BF16)
        zero = jnp.zeros_like(xb)
        return jnp.concatenate([jnp.where(head_mask[h], xb, zero) for h in range(N_HEADS)], axis=0)

    def mm(lhs, xv):
        return _dot(lhs.astype(BF16), bd(xv))

    def rope(v, cos, sin):
        sw = jnp.where(first_half, pltpu.roll(v, 128 - HEAD_DIM // 2, 1), pltpu.roll(v, HEAD_DIM // 2, 1))
        return v * cos + sw * sin

    proj_chunks = ([(w1_ref, c, c, PROJ_CHUNK) for c in range(0, W_IN_SPLIT, PROJ_CHUNK)]
                   + [(w2_ref, c, W_IN_SPLIT + c, PROJ_CHUNK) for c in range(0, COL_TAIL - W_IN_SPLIT, PROJ_CHUNK)]
                   + [(w3_ref, 0, COL_TAIL, Z_WIDTH - COL_TAIL)])

    def in_proj(x_rows, z_ref, slot):
        for r0 in range(0, ts, 64):
            xv = x_rows(r0, 64)
            ms = jnp.mean(xv * xv, axis=-1, keepdims=True)
            h_ref[r0:r0 + 64, :] = (xv * lax.rsqrt(ms + EPS)).astype(BF16)
        for w_ref, wc, zc, width in proj_chunks:
            zv = _dot(h_ref[...], w_ref[:, wc:wc + width])
            if COL_C <= zc < COL_D:
                zv = 0.5 * zv * (1.0 + jnp.tanh(0.7978845608028654 * (zv + 0.044715 * (zv * zv * zv))))
            if zc == COL_D + 256:
                ybuf[slot, HALO:HALO + ts, :] = z_ref[HALO:HALO + ts, COL_D:COL_D + 256] * _sigmoid(zv)
            elif zc >= COL_GATE and zc < COL_TAIL:
                sg_ref[slot, :, zc - COL_GATE:zc - COL_GATE + width] = _silu(zv)
            else:
                z_ref[HALO:HALO + ts, zc:zc + width] = zv
            yield

    def gate_silu(tile, rows, c0, c1):
        return sg_ref[tile, rows, c0:c1]

    def attention(z_ref, tile):
        mix_t = mix_ref.at[tile % 2]
        for blk in range(ts // WINDOW):
            rows = slice(HALO + blk * WINDOW, HALO + (blk + 1) * WINDOW)
            q = z_ref[rows, COL_AQ:COL_AQ + 256]
            k = z_ref[rows, COL_AK:COL_AK + 128]
            v = z_ref[rows, COL_AV:COL_AV + 128]
            cos = cos_ref[tile * ts + blk * WINDOW:tile * ts + (blk + 1) * WINDOW, :]
            sin = sin_ref[tile * ts + blk * WINDOW:tile * ts + (blk + 1) * WINDOW, :]
            qn = q * lax.rsqrt(head_mean(q * q) + EPS) * q_gain
            kn = k * lax.rsqrt(_split_lhs_dot(k * k, g_mean[0:128, 0:128], 1) + EPS) * k_gain
            qa = rope(qn[:, 0:128], cos, sin)
            qb = rope(qn[:, 128:256], cos, sin)
            kr = rope(kn, cos, sin)
            cur = slice(WINDOW + blk * WINDOW, 2 * WINDOW + blk * WINDOW)
            both = slice(blk * WINDOW, (blk + 2) * WINDOW)
            kbuf[0, cur, :] = kr.astype(BF16)
            kbuf[1, cur, :] = pltpu.roll(kr, HEAD_DIM, 1).astype(BF16)
            vbuf[0, cur, :] = v.astype(BF16)
            vbuf[1, cur, :] = pltpu.roll(v, HEAD_DIM, 1).astype(BF16)
            lhs = [jnp.where(lo128, qa, 0.0).astype(BF16), jnp.where(lo128, 0.0, qa).astype(BF16),
                   jnp.where(lo128, qb, 0.0).astype(BF16), jnp.where(lo128, 0.0, qb).astype(BF16)]
            yield
            valid = band
            if blk == 0 and tile == 0:
                valid = valid & (kj >= jnp.where(sp > 0, 0, WINDOW))
            outs = []
            for r in range(N_HEADS):
                slot = 0 if r in (0, 3) else 1
                sc = jnp.where(valid, _dot_nt(lhs[r], kbuf[slot, both, :]), -jnp.inf)
                sink = sink_ref[layer, r]
                m = jnp.maximum(jnp.max(sc, axis=-1, keepdims=True), sink)
                e = jnp.exp(sc - m)
                den = jnp.sum(e, axis=-1, keepdims=True) + jnp.exp(sink - m)
                outs.append(_dot(e.astype(BF16), vbuf[slot, both, :]) / den)
                yield
            o_a0 = jnp.where(lo128, outs[0], outs[1])
            o_a1 = jnp.where(lo128, outs[2], outs[3])
            mrows = slice(blk * WINDOW, (blk + 1) * WINDOW)
            mix_t[mrows, 0:128] = (o_a0 * gate_silu(tile % 2, mrows, 0, 128)).astype(BF16)
            mix_t[mrows, 128:256] = (o_a1 * gate_silu(tile % 2, mrows, 128, 256)).astype(BF16)
            yield
        kbuf[:, 0:WINDOW, :] = kbuf[:, ts:ts + WINDOW, :]
        vbuf[:, 0:WINDOW, :] = vbuf[:, ts:ts + WINDOW, :]

    def deltanet(z_ref, tile):
        mix_t = mix_ref.at[tile % 2]
        cs = range(ts // DN_CHUNK)
        sls = [slice(c * DN_CHUNK, (c + 1) * DN_CHUNK) for c in cs]
        q_c, k_c, v_c = [], [], []
        for c in cs:
            r0 = HALO + c * DN_CHUNK - (B_CONV - 1)
            acc = None
            for kk in range(B_CONV):
                term = z_ref[r0 + kk:r0 + kk + DN_CHUNK, COL_B:COL_B + 768] * wb[kk:kk + 1, :]
                acc = term if acc is None else acc + term
            qkv = _silu(acc)
            q, k = qkv[:, 0:256], qkv[:, 256:512]
            ss = head_sum(jnp.concatenate([q * q, k * k], axis=0))
            q_c.append(q * lax.rsqrt(ss[0:DN_CHUNK] + EPS) * (HEAD_DIM ** -0.5))
            k_c.append(k * lax.rsqrt(ss[DN_CHUNK:] + EPS))
            v_c.append(qkv[:, 512:768])
            yield

        tail = z_ref[HALO:HALO + ts, COL_TAIL:COL_TAIL + 128]
        beta = _sigmoid(tail)
        xg = tail + vec_ref[0:1, 640:768]
        softplus = jnp.maximum(xg, 0.0) + jnp.log1p(jnp.exp(-jnp.abs(xg)))
        gdec = -jnp.exp(vec_ref[0:1, 512:640]) * softplus
        beta_f = _split_lhs_dot(beta, e_beta, 1)
        gdec_f = _split_lhs_dot(gdec, e_alpha, 2)
        gc = _split_rhs_dot(l_bd, gdec_f, 2)
        yield

        gc_c = [gc[s] for s in sls]
        gl = [g[DN_CHUNK - 1:DN_CHUNK, :] for g in gc_c]
        egc = [jnp.exp(g) for g in gc_c]
        ekg = [jnp.exp(gl[c] - gc_c[c]) for c in cs]
        egl = [jnp.exp(g) for g in gl]
        kb = [k_c[c] * beta_f[sls[c]] for c in cs]
        vb = [v_c[c] * beta_f[sls[c]] for c in cs]
        grow = [jnp.sum(jnp.where(eye, g, 0.0), axis=0, keepdims=True) for g in gc_c]
        decay = [jnp.exp(jnp.minimum(gc_c[c] - grow[c], 0.0)) for c in cs]
        yield
        aq = [_dot_nt(jnp.concatenate([kb[c], q_c[c]], axis=0).astype(BF16), bd(k_c[c])) for c in cs]
        a_m = [jnp.where(strict, aq[c][0:DN_CHUNK] * decay[c], 0.0) for c in cs]
        qk = [jnp.where(tril, aq[c][DN_CHUNK:2 * DN_CHUNK] * decay[c], 0.0) for c in cs]
        yield

        d1 = [jnp.where(same16, a, 0.0) for a in a_m]
        pm = [eye_f - d for d in d1]
        d2 = [mm(d, d) for d in d1]
        yield
        r = [mm(jnp.concatenate([pm[c], d2[c]], axis=0), d2[c]) for c in cs]
        pm = [pm[c] + r[c][0:DN_CHUNK] for c in cs]
        d4 = [r[c][DN_CHUNK:] for c in cs]
        yield
        r = [mm(jnp.concatenate([pm[c], d4[c]], axis=0), d4[c]) for c in cs]
        pm = [pm[c] + r[c][0:DN_CHUNK] for c in cs]
        d8 = [r[c][DN_CHUNK:] for c in cs]
        yield
        t0 = [pm[c] + mm(pm[c], d8[c]) for c in cs]
        yield
        m1 = [jnp.where(same32 & jnp.logical_not(same16), a, 0.0) for a in a_m]
        r = [mm(t0[c], m1[c]) for c in cs]
        yield
        t1 = [t0[c] - mm(r[c], t0[c]) for c in cs]
        yield
        m2 = [jnp.where(same32, 0.0, a) for a in a_m]
        r = [mm(t1[c], m2[c]) for c in cs]
        yield
        t2 = [(t1[c] - mm(r[c], t1[c])).astype(BF16) for c in cs]
        yield
        u_c = [_dot(t2[c], bd(vb[c])) for c in cs]
        w_c = [_dot(t2[c], bd(kb[c] * egc[c])) for c in cs]
        yield

        kg = [(k_c[c] * ekg[c]).astype(BF16) for c in cs]
        nmat = [jnp.where(same_head, _dot_tn(w_c[c].astype(BF16), kg[c]), 0.0).astype(BF16) for c in cs]
        bfull = [jnp.where(same_head, _dot_tn(u_c[c].astype(BF16), kg[c]), 0.0) for c in cs]
        bcomp = [b[0:64] + b[64:128] + b[128:192] + b[192:256] for b in bfull]
        yield
        qkb = [q.astype(BF16) for q in qk]
        qt = [(q_c[c] * egc[c] - _dot(qkb[c], bd(w_c[c]))).astype(BF16) for c in cs]
        ou = [_dot(qkb[c], bd(u_c[c])) for c in cs]
        yield

        st = st_ref[...]
        for c in cs:
            o_c = _dot_nt(qt[c], bd(st)) + ou[c]
            st = st * egl[c] - _dot(st.astype(BF16), nmat[c]) + bcomp[c]
            if c == cs[-1]:
                st_ref[...] = st
            yield
            o_n = o_c * lax.rsqrt(head_mean(o_c * o_c) + EPS) * out_gain
            zrows = slice(HALO + c * DN_CHUNK, HALO + (c + 1) * DN_CHUNK)
            mix_t[sls[c], 256:512] = (o_n * gate_silu(tile % 2, sls[c], 256, 512)).astype(BF16)

    def convs(z_ref, tile):
        mix_t, ybuf_t, ysh_t, yc_t = mix_ref.at[tile % 2], ybuf.at[tile % 2], ysh.at[tile % 2], yc_ref.at[tile % 2]
        for c in range(ts // C_CHUNK):
            sl = slice(c * C_CHUNK, (c + 1) * C_CHUNK)
            zrows = slice(HALO + c * C_CHUNK, HALO + (c + 1) * C_CHUNK)
            cu = z_ref[zrows, COL_C:COL_C + 256]
            cv = z_ref[zrows, COL_C + 256:COL_C + 512]
            mu = jnp.mean(cv, axis=-1, keepdims=True)
            dv = cv - mu
            var = jnp.mean(dv * dv, axis=-1, keepdims=True)
            cvn = dv * lax.rsqrt(var + EPS) * vec_ref[1:2, 256:512] + vec_ref[1:2, 512:768]
            yield
            vblk = jnp.concatenate([jnp.where(head_mask[g], cvn, 0.0).astype(BF16) for g in range(4)], axis=0)
            zc = _dot(cw, vblk) + cbias
            mix_t[sl, 512:768] = (cu * zc * gate_silu(tile % 2, sl, 512, 768)).astype(BF16)
            yield

        for r in range(1, 8):
            ysh_t[r - 1] = ybuf_t[r:r + ts + HALO - 8, :]
            if r % 2 == 1:
                yield
        for rb in range(0, ts, CONV_ROWS):
            acc = None
            for kk in range(D_CONV):
                r0 = HALO - (D_CONV - 1) + kk
                a8, r = (r0 // 8) * 8 + rb, r0 % 8
                win = ybuf_t[a8:a8 + CONV_ROWS, :] if r == 0 else ysh_t[r - 1, a8:a8 + CONV_ROWS, :]
                term = win * wd[kk:kk + 1, :]
                acc = term if acc is None else acc + term
            yc_t[rb:rb + CONV_ROWS, :] = acc + vec_ref[1:2, 768:1024]
            yield
        for c in range(ts // C_CHUNK):
            sl = slice(c * C_CHUNK, (c + 1) * C_CHUNK)
            zrows = slice(HALO + c * C_CHUNK, HALO + (c + 1) * C_CHUNK)
            yc = yc_t[sl, :]
            mu = head_mean(yc)
            dy = yc - mu
            var = head_mean(dy * dy)
            yn = dy * lax.rsqrt(var + EPS) * vec_ref[2:3, 0:256] + vec_ref[2:3, 256:512]
            mix_t[sl, 768:1024] = (_silu(yn) * gate_silu(tile % 2, sl, 768, 1024)).astype(BF16)
            yield

    def out_proj(tile):
        mix_t, x1_t, hg_t = mix_ref.at[tile % 2], x1_ref.at[tile % 2], hg_ref.at[tile % 2]
        rows = slice(tile * ts, (tile + 1) * ts)
        for c0 in range(0, D_MODEL, PROJ_CHUNK):
            cols = slice(c0, c0 + PROJ_CHUNK)
            x1_t[:, cols] = x_ref[0, rows, cols] + _dot(mix_t[...], w_out_ref[:, cols])
            yield
        for r0 in range(0, ts, 64):
            x1 = x1_t[r0:r0 + 64, :]
            ms1 = jnp.mean(x1 * x1, axis=-1, keepdims=True)
            hg_t[r0:r0 + 64, :] = (x1 * lax.rsqrt(ms1 + EPS)).astype(BF16)
        yield
        pb = p_ref[0, rows, :].astype(BF16)
        for c0 in range(0, D_MODEL, PROJ_CHUNK):
            cols = slice(c0, c0 + PROJ_CHUNK)
            ple_gate = _sigmoid(_dot(hg_t[...], gate_w_ref[:, cols]))
            pe = _dot(pb, ple_w_ref[:, cols])
            o_ref[0, rows, cols] = x1_t[:, cols] + ple_gate * pe
            yield

    @pl.when(sp == 0)
    def _():
        za_ref[0:HALO, :] = jnp.zeros((HALO, Z_WIDTH), F32)
        kbuf[:, 0:WINDOW, :] = jnp.zeros((2, WINDOW, 128), BF16)
        vbuf[:, 0:WINDOW, :] = jnp.zeros((2, WINDOW, 128), BF16)
        st_ref[...] = jnp.zeros((HEAD_DIM, 256), F32)
        ybuf[0, 0:HALO, :] = jnp.zeros((HALO, GROUP_WIDTH), F32)

    @pl.when(step == 0)
    def _():
        _run_streams([(in_proj(lambda r0, n: x_ref[0, r0:r0 + n, :], za_ref, 0), 1)])

    z_refs = (za_ref, zb_ref)
    for j in range(TILES_PER_STEP):
        z_cur, z_nxt = z_refs[j % 2], z_refs[(j + 1) % 2]
        z_nxt[0:HALO, :] = z_cur[ts:ts + HALO, :]
        ybuf[(j + 1) % 2, 0:HALO, :] = ybuf[j % 2, ts:ts + HALO, :]
        if j + 1 < TILES_PER_STEP:
            nxt_rows = functools.partial(lambda r0, n, base: x_ref[0, base + r0:base + r0 + n, :], base=(j + 1) * ts)
        else:
            nxt_rows = lambda r0, n: xn_ref[0, r0:r0 + n, :]
        streams = [(deltanet(z_cur, j), 1), (convs(z_cur, j), 1), (attention(z_cur, j), 2),
                   (in_proj(nxt_rows, z_nxt, (j + 1) % 2), 2)]
        if j > 0:
            streams.append((out_proj(j - 1), 1))
        _run_streams(streams)
    _run_streams([(out_proj(TILES_PER_STEP - 1), 1)])


def _layer_spec(arr, layer):
    return pl.BlockSpec((None,) + arr.shape[1:], lambda g: (layer,) + (0,) * (arr.ndim - 1),
                        pipeline_mode=pl.Buffered(1))


def _layer_call(x, p, layer, cos_t, sin_t, sink, w_in, w_mid, w_tail, w_out, ple_w, gate_w, vec, bconv, cw, cbias,
                dconv, ts):
    bsz, seq, _ = x.shape
    tiles_per_seq = seq // ts
    steps_per_seq = tiles_per_seq // TILES_PER_STEP
    n_steps = bsz * steps_per_seq
    last_tile = bsz * tiles_per_seq - 1

    def pair_map(g):
        return (g // steps_per_seq, g % steps_per_seq, 0)

    def next_tile_map(g):
        t = jnp.minimum(TILES_PER_STEP * (g + 1), last_tile)
        return (t // tiles_per_seq, t % tiles_per_seq, 0)

    in_specs = [
        pl.BlockSpec(memory_space=pltpu.SMEM),
        pl.BlockSpec((1, TILES_PER_STEP * ts, D_MODEL), pair_map),
        pl.BlockSpec((1, ts, D_MODEL), next_tile_map),
        pl.BlockSpec((None, 1, TILES_PER_STEP * ts, D_PLE), lambda g: (layer,) + pair_map(g)),
        pl.BlockSpec((TILES_PER_STEP * ts, 128), lambda g: (g % steps_per_seq, 0)),
        pl.BlockSpec((TILES_PER_STEP * ts, 128), lambda g: (g % steps_per_seq, 0)),
    ] + [_layer_spec(a, layer) for a in (w_in, w_mid, w_tail, w_out, ple_w, gate_w, vec, bconv, cw, cbias, dconv)]
    scratch = [
        pltpu.VMEM((HALO + ts, Z_WIDTH), F32),
        pltpu.VMEM((HALO + ts, Z_WIDTH), F32),
        pltpu.VMEM((ts, D_MODEL), BF16),
        pltpu.VMEM((2, WINDOW + ts, 128), BF16),
        pltpu.VMEM((2, WINDOW + ts, 128), BF16),
        pltpu.VMEM((HEAD_DIM, 256), F32),
        pltpu.VMEM((2, ts, D_MODEL), BF16),
        pltpu.VMEM((2, HALO + ts, GROUP_WIDTH), F32),
        pltpu.VMEM((2, 7, HALO + ts - 8, GROUP_WIDTH), F32),
        pltpu.VMEM((2, ts, GROUP_WIDTH), F32),
        pltpu.VMEM((2, ts, D_MODEL), F32),
        pltpu.VMEM((2, ts, D_MODEL), BF16),
        pltpu.VMEM((2, ts, D_MODEL), F32),
    ]
    return pl.pallas_call(
        functools.partial(_layer_kernel, ts=ts, steps_per_seq=steps_per_seq, layer=layer),
        grid=(n_steps,),
        in_specs=in_specs,
        out_specs=pl.BlockSpec((1, TILES_PER_STEP * ts, D_MODEL), pair_map),
        out_shape=jax.ShapeDtypeStruct(x.shape, F32),
        scratch_shapes=scratch,
        compiler_params=pltpu.CompilerParams(
            dimension_semantics=("arbitrary",),
            vmem_limit_bytes=VMEM_LIMIT_BYTES,
        ),
        name="hybrid_trunk_layer",
    )(sink, x, x, p, cos_t, sin_t, w_in, w_mid, w_tail, w_out, ple_w, gate_w, vec, bconv, cw, cbias, dconv)


W_PREP_BLOCK = 256


def _w_in_prep_kernel(wt_ref, g_ref, head_ref, mid_ref, tail_ref):
    n = W_PREP_BLOCK
    eye = (lax.broadcasted_iota(jnp.int32, (n, n), 0) == lax.broadcasted_iota(jnp.int32, (n, n), 1)).astype(BF16)
    g = g_ref[...]
    d_in = g.shape[1]

    def transposed(r0, rows):
        wb = (wt_ref[r0:r0 + rows, :] * g).astype(BF16)
        return jnp.concatenate([_dot_nt(eye, wb[:, k:k + n]) for k in range(0, d_in, n)], axis=0).astype(BF16)

    for c in range(0, W_IN_SPLIT, n):
        head_ref[:, c:c + n] = transposed(c, n)
    mid0 = W_IN_SPLIT + 2 * N_HEADS
    for c in range(0, mid_ref.shape[1], n):
        mid_ref[:, c:c + n] = transposed(mid0 + c, n)
    lane = lax.broadcasted_iota(jnp.int32, (1, 128), 1)
    tail = transposed(W_IN_SPLIT, 128)
    tail_ref[...] = jnp.where(lane < 2 * N_HEADS, tail, jnp.zeros_like(tail))


def _w_in_prep(w_in_t, gain):
    depth, width, d_in = w_in_t.shape
    mid = width - W_IN_SPLIT - 2 * N_HEADS

    def spec(rows, cols):
        return pl.BlockSpec((None, rows, cols), lambda l: (l, 0, 0))

    return pl.pallas_call(
        _w_in_prep_kernel,
        grid=(depth,),
        in_specs=[spec(width, d_in), spec(1, d_in)],
        out_specs=[spec(d_in, W_IN_SPLIT), spec(d_in, mid), spec(d_in, 128)],
        out_shape=[jax.ShapeDtypeStruct((depth, d_in, c), BF16) for c in (W_IN_SPLIT, mid, 128)],
        compiler_params=pltpu.CompilerParams(vmem_limit_bytes=VMEM_LIMIT_BYTES),
        name="w_in_prep",
    )(w_in_t, gain[:, None, :])


def kernel(x, p, norm_gain, w_in, w_out, a_q_gain, a_k_gain, a_sink, b_conv, b_a_log, b_dt_bias, b_out_gain,
           c_ln_gain, c_ln_bias, c_w_s, c_b_s, d_conv, d_conv_bias, d_ln_gain, d_ln_bias,
           ple_w, ple_gate_norm, ple_gate_w):
    depth = w_in.shape[0]
    bsz, seq, _ = x.shape
    ts = min(SEQ_TILE, seq)
    assert seq % (TILES_PER_STEP * ts) == 0 and ts % WINDOW == 0 and TILES_PER_STEP % 2 == 0

    inv = jnp.tile(ROPE_THETA ** (-jnp.arange(0, HEAD_DIM, 2, dtype=F32) / HEAD_DIM), 4)
    sign = jnp.tile(jnp.repeat(jnp.array([-1.0, 1.0], F32), HEAD_DIM // 2), 2)
    ang_b = (jnp.arange(seq // WINDOW, dtype=F32) * WINDOW)[:, None] * inv[None, :]
    ang_r = jnp.arange(WINDOW, dtype=F32)[:, None] * inv[None, :]
    cb, sb, cr, sr = jnp.cos(ang_b)[:, None, :], jnp.sin(ang_b)[:, None, :], jnp.cos(ang_r)[None], jnp.sin(ang_r)[None]
    cos_t = (cb * cr - sb * sr).reshape(seq, 128)
    sin_t = ((sb * cr + cb * sr) * sign).reshape(seq, 128)

    w_head, w_mid, w_tail = _w_in_prep(jnp.swapaxes(w_in, 1, 2), norm_gain)
    gate_w_b = (ple_gate_w * ple_gate_norm[:, :, None]).astype(BF16)
    w_out_b, ple_w_b = w_out.astype(BF16), ple_w.astype(BF16)

    def z(n):
        return jnp.zeros((depth, n), F32)

    def rep4(v, n):
        return jnp.broadcast_to(v[:, None, :], (depth, n, v.shape[1])).reshape(depth, n * v.shape[1])

    vec = jnp.concatenate([
        rep4(a_q_gain * (HEAD_DIM ** -0.5), 4), rep4(a_k_gain, 2), z(128),
        z(N_HEADS), b_a_log, z(128 - 2 * N_HEADS), z(N_HEADS), b_dt_bias, z(128 - 2 * N_HEADS), z(256),
        rep4(b_out_gain, 4), c_ln_gain, c_ln_bias, d_conv_bias,
        d_ln_gain, d_ln_bias, z(512), z(5 * D_MODEL)], axis=1).astype(F32).reshape(depth, 8, D_MODEL)
    cw = jnp.transpose(c_w_s, (0, 2, 1, 3)).reshape(depth, C_CHUNK, 4 * C_CHUNK)
    cbias = jnp.repeat(jnp.swapaxes(c_b_s, 1, 2), GROUP_WIDTH // 4, axis=2)
    sink = a_sink.astype(F32)

    for i in range(depth):
        x = _layer_call(x, p, i, cos_t, sin_t, sink, w_head, w_mid, w_tail, w_out_b, ple_w_b, gate_w_b, vec,
                        b_conv, cw, cbias, d_conv, ts)
    return x
```

```python
import functools

import jax
import jax.numpy as jnp
from jax import lax
from jax.experimental import pallas as pl
from jax.experimental.pallas import tpu as pltpu

F32 = jnp.float32
BF16 = jnp.bfloat16

EPS = 1e-6
D_MODEL = 1024
D_PLE = 256
GROUP_WIDTH = 256
HEAD_DIM = 64
N_HEADS = 4
WINDOW = 128
ROPE_THETA = 10000.0
B_CONV = 4
DN_CHUNK = 64
C_CHUNK = 128
D_CONV = 31

COL_AQ, COL_AK, COL_AV = 0, 256, 384
COL_B = 512
COL_C = 1280
COL_D = 1792
COL_GATE = 2304
COL_TAIL = 3328
Z_WIDTH = 3456
W_IN_SPLIT = 1280
HALO = 32

SEQ_TILE = 256
PROJ_CHUNK = 256
CONV_ROWS = 32
VMEM_LIMIT_BYTES = 56 * 1024 * 1024


def _sigmoid(x):
    return 0.5 + 0.5 * jnp.tanh(0.5 * x)


def _silu(x):
    h = 0.5 * x
    return h + h * jnp.tanh(h)


def _dot(a, b):
    return jnp.dot(a, b, preferred_element_type=F32)


def _dot_nt(a, b):
    return lax.dot_general(a, b, (((1,), (1,)), ((), ())), preferred_element_type=F32)


def _dot_tn(a, b):
    return lax.dot_general(a, b, (((0,), (0,)), ((), ())), preferred_element_type=F32)


def _split_lhs_dot(a, b_bf16, n):
    acc = None
    r = a
    for i in range(n):
        hi = r.astype(BF16)
        t = _dot(hi, b_bf16)
        acc = t if acc is None else acc + t
        if i + 1 < n:
            r = r - hi.astype(F32)
    return acc


def _split_rhs_dot(a_bf16, b, n):
    acc = None
    r = b
    for i in range(n):
        hi = r.astype(BF16)
        t = _dot(a_bf16, hi)
        acc = t if acc is None else acc + t
        if i + 1 < n:
            r = r - hi.astype(F32)
    return acc


def _run_streams(streams):
    live = list(streams)
    rnd = 0
    while live:
        for item in list(live):
            gen, period = item
            if rnd % period == 0 and next(gen, StopIteration) is StopIteration:
                live.remove(item)
        rnd += 1


def _layer_kernel(sink_ref, x_ref, xn_ref, p_ref, cos_ref, sin_ref, w1_ref, w2_ref, w3_ref, w_out_ref, ple_w_ref,
                  gate_w_ref, vec_ref, bconv_ref, cw_ref, cbias_ref, dconv_ref, o_ref,
                  za_ref, zb_ref, h_ref, kbuf, vbuf, st_ref, mix_ref, ybuf, ysh, yc_ref, x1_ref, hg_ref, sg_ref,
                  *, ts, steps_per_seq, layer):
    step = pl.program_id(0)
    sp = step % steps_per_seq

    lane256 = lax.broadcasted_iota(jnp.int32, (1, 256), 1)
    head_mask = [(lane256 // HEAD_DIM) == h for h in range(N_HEADS)]
    r256 = lax.broadcasted_iota(jnp.int32, (256, 256), 0)
    c256 = lax.broadcasted_iota(jnp.int32, (256, 256), 1)
    same_head = (r256 // HEAD_DIM) == (c256 // HEAD_DIM)
    g_sum = jnp.where(same_head, 1.0, 0.0).astype(BF16)
    g_mean = jnp.where(same_head, 1.0 / HEAD_DIM, 0.0).astype(BF16)
    lane128 = lax.broadcasted_iota(jnp.int32, (1, 128), 1)
    first_half = (lane128 % HEAD_DIM) < (HEAD_DIM // 2)
    lo128 = lane128 < HEAD_DIM
    q_gain = vec_ref[0:1, 0:256]
    k_gain = vec_ref[0:1, 256:384]
    qi = lax.broadcasted_iota(jnp.int32, (WINDOW, 2 * WINDOW), 0)
    kj = lax.broadcasted_iota(jnp.int32, (WINDOW, 2 * WINDOW), 1)
    rel = qi + WINDOW - kj
    band = (rel >= 0) & (rel < WINDOW)
    wb = bconv_ref[...]
    er = lax.broadcasted_iota(jnp.int32, (128, 256), 0)
    ec = lax.broadcasted_iota(jnp.int32, (128, 256), 1) // HEAD_DIM
    e_beta = jnp.where(er == ec, 1.0, 0.0).astype(BF16)
    e_alpha = jnp.where(er == ec + N_HEADS, 1.0, 0.0).astype(BF16)
    tr = lax.broadcasted_iota(jnp.int32, (ts, ts), 0)
    tc = lax.broadcasted_iota(jnp.int32, (ts, ts), 1)
    l_bd = jnp.where(((tr // DN_CHUNK) == (tc // DN_CHUNK)) & (tr >= tc), 1.0, 0.0).astype(BF16)
    ci = lax.broadcasted_iota(jnp.int32, (DN_CHUNK, 256), 0)
    cj = lax.broadcasted_iota(jnp.int32, (DN_CHUNK, 256), 1) % HEAD_DIM
    eye = ci == cj
    tril = ci >= cj
    strict = ci > cj
    same16 = (ci // 16) == (cj // 16)
    same32 = (ci // 32) == (cj // 32)
    eye_f = jnp.where(eye, 1.0, 0.0)
    out_gain = vec_ref[1:2, 0:256]
    wi = lax.broadcasted_iota(jnp.int32, (C_CHUNK, 4 * C_CHUNK), 0)
    wj = lax.broadcasted_iota(jnp.int32, (C_CHUNK, 4 * C_CHUNK), 1) % C_CHUNK
    cw = jnp.where(wi >= wj, cw_ref[...], 0.0).astype(BF16)
    cbias = cbias_ref[...]
    wd = dconv_ref[...]

    def head_mean(v, n=1):
        return _split_lhs_dot(v, g_mean, n)

    def head_sum(v, n=1):
        return _split_lhs_dot(v, g_sum, n)

    def bd(xv):
        xb = xv.astype(BF16)
        zero = jnp.zeros_like(xb)
        return jnp.concatenate([jnp.where(head_mask[h], xb, zero) for h in range(N_HEADS)], axis=0)

    def mm(lhs, xv):
        return _dot(lhs.astype(BF16), bd(xv))

    def rope(v, cos, sin):
        sw = jnp.where(first_half, pltpu.roll(v, 128 - HEAD_DIM // 2, 1), pltpu.roll(v, HEAD_DIM // 2, 1))
        return v * cos + sw * sin

    proj_chunks = ([(w1_ref, c, c, PROJ_CHUNK) for c in range(0, W_IN_SPLIT, PROJ_CHUNK)]
                   + [(w2_ref, c, W_IN_SPLIT + c, PROJ_CHUNK) for c in range(0, COL_TAIL - W_IN_SPLIT, PROJ_CHUNK)]
                   + [(w3_ref, 0, COL_TAIL, Z_WIDTH - COL_TAIL)])

    def in_proj(x_rows, z_ref, slot):
        for r0 in range(0, ts, 64):
            xv = x_rows(r0, 64)
            ms = jnp.mean(xv * xv, axis=-1, keepdims=True)
            h_ref[r0:r0 + 64, :] = (xv * lax.rsqrt(ms + EPS)).astype(BF16)
        for w_ref, wc, zc, width in proj_chunks:
            zv = _dot(h_ref[...], w_ref[:, wc:wc + width])
            if COL_C <= zc < COL_D:
                zv = 0.5 * zv * (1.0 + jnp.tanh(0.7978845608028654 * (zv + 0.044715 * (zv * zv * zv))))
            if zc == COL_D + 256:
                ybuf[slot, HALO:HALO + ts, :] = z_ref[HALO:HALO + ts, COL_D:COL_D + 256] * _sigmoid(zv)
            elif zc >= COL_GATE and zc < COL_TAIL:
                sg_ref[slot, :, zc - COL_GATE:zc - COL_GATE + width] = _silu(zv)
            else:
                z_ref[HALO:HALO + ts, zc:zc + width] = zv
            yield

    def gate_silu(tile, rows, c0, c1):
        return sg_ref[tile, rows, c0:c1]

    def attention(z_ref, tile):
        mix_t = mix_ref.at[tile]
        for blk in range(ts // WINDOW):
            rows = slice(HALO + blk * WINDOW, HALO + (blk + 1) * WINDOW)
            q = z_ref[rows, COL_AQ:COL_AQ + 256]
            k = z_ref[rows, COL_AK:COL_AK + 128]
            v = z_ref[rows, COL_AV:COL_AV + 128]
            cos = cos_ref[tile * ts + blk * WINDOW:tile * ts + (blk + 1) * WINDOW, :]
            sin = sin_ref[tile * ts + blk * WINDOW:tile * ts + (blk + 1) * WINDOW, :]
            qn = q * lax.rsqrt(head_mean(q * q) + EPS) * q_gain
            kn = k * lax.rsqrt(_split_lhs_dot(k * k, g_mean[0:128, 0:128], 1) + EPS) * k_gain
            qa = rope(qn[:, 0:128], cos, sin)
            qb = rope(qn[:, 128:256], cos, sin)
            kr = rope(kn, cos, sin)
            cur = slice(WINDOW + blk * WINDOW, 2 * WINDOW + blk * WINDOW)
            both = slice(blk * WINDOW, (blk + 2) * WINDOW)
            kbuf[0, cur, :] = kr.astype(BF16)
            kbuf[1, cur, :] = pltpu.roll(kr, HEAD_DIM, 1).astype(BF16)
            vbuf[0, cur, :] = v.astype(BF16)
            vbuf[1, cur, :] = pltpu.roll(v, HEAD_DIM, 1).astype(BF16)
            lhs = [jnp.where(lo128, qa, 0.0).astype(BF16), jnp.where(lo128, 0.0, qa).astype(BF16),
                   jnp.where(lo128, qb, 0.0).astype(BF16), jnp.where(lo128, 0.0, qb).astype(BF16)]
            yield
            valid = band
            if blk == 0 and tile == 0:
                valid = valid & (kj >= jnp.where(sp > 0, 0, WINDOW))
            outs = []
            for r in range(N_HEADS):
                slot = 0 if r in (0, 3) else 1
                sc = jnp.where(valid, _dot_nt(lhs[r], kbuf[slot, both, :]), -jnp.inf)
                sink = sink_ref[layer, r]
                m = jnp.maximum(jnp.max(sc, axis=-1, keepdims=True), sink)
                e = jnp.exp(sc - m)
                den = jnp.sum(e, axis=-1, keepdims=True) + jnp.exp(sink - m)
                outs.append(_dot(e.astype(BF16), vbuf[slot, both, :]) / den)
                yield
            o_a0 = jnp.where(lo128, outs[0], outs[1])
            o_a1 = jnp.where(lo128, outs[2], outs[3])
            mrows = slice(blk * WINDOW, (blk + 1) * WINDOW)
            mix_t[mrows, 0:128] = (o_a0 * gate_silu(tile, mrows, 0, 128)).astype(BF16)
            mix_t[mrows, 128:256] = (o_a1 * gate_silu(tile, mrows, 128, 256)).astype(BF16)
            yield
        kbuf[:, 0:WINDOW, :] = kbuf[:, ts:ts + WINDOW, :]
        vbuf[:, 0:WINDOW, :] = vbuf[:, ts:ts + WINDOW, :]

    def deltanet(z_ref, tile):
        mix_t = mix_ref.at[tile]
        cs = range(ts // DN_CHUNK)
        sls = [slice(c * DN_CHUNK, (c + 1) * DN_CHUNK) for c in cs]
        q_c, k_c, v_c = [], [], []
        for c in cs:
            r0 = HALO + c * DN_CHUNK - (B_CONV - 1)
            acc = None
            for kk in range(B_CONV):
                term = z_ref[r0 + kk:r0 + kk + DN_CHUNK, COL_B:COL_B + 768] * wb[kk:kk + 1, :]
                acc = term if acc is None else acc + term
            qkv = _silu(acc)
            q, k = qkv[:, 0:256], qkv[:, 256:512]
            ss = head_sum(jnp.concatenate([q * q, k * k], axis=0))
            q_c.append(q * lax.rsqrt(ss[0:DN_CHUNK] + EPS) * (HEAD_DIM ** -0.5))
            k_c.append(k * lax.rsqrt(ss[DN_CHUNK:] + EPS))
            v_c.append(qkv[:, 512:768])
            yield

        tail = z_ref[HALO:HALO + ts, COL_TAIL:COL_TAIL + 128]
        beta = _sigmoid(tail)
        xg = tail + vec_ref[0:1, 640:768]
        softplus = jnp.maximum(xg, 0.0) + jnp.log1p(jnp.exp(-jnp.abs(xg)))
        gdec = -jnp.exp(vec_ref[0:1, 512:640]) * softplus
        beta_f = _split_lhs_dot(beta, e_beta, 1)
        gdec_f = _split_lhs_dot(gdec, e_alpha, 2)
        gc = _split_rhs_dot(l_bd, gdec_f, 2)
        yield

        gc_c = [gc[s] for s in sls]
        gl = [g[DN_CHUNK - 1:DN_CHUNK, :] for g in gc_c]
        egc = [jnp.exp(g) for g in gc_c]
        ekg = [jnp.exp(gl[c] - gc_c[c]) for c in cs]
        egl = [jnp.exp(g) for g in gl]
        kb = [k_c[c] * beta_f[sls[c]] for c in cs]
        vb = [v_c[c] * beta_f[sls[c]] for c in cs]
        grow = [jnp.sum(jnp.where(eye, g, 0.0), axis=0, keepdims=True) for g in gc_c]
        decay = [jnp.exp(jnp.minimum(gc_c[c] - grow[c], 0.0)) for c in cs]
        yield
        aq = [_dot_nt(jnp.concatenate([kb[c], q_c[c]], axis=0).astype(BF16), bd(k_c[c])) for c in cs]
        a_m = [jnp.where(strict, aq[c][0:DN_CHUNK] * decay[c], 0.0) for c in cs]
        qk = [jnp.where(tril, aq[c][DN_CHUNK:2 * DN_CHUNK] * decay[c], 0.0) for c in cs]
        yield

        d1 = [jnp.where(same16, a, 0.0) for a in a_m]
        pm = [eye_f - d for d in d1]
        d2 = [mm(d, d) for d in d1]
        yield
        r = [mm(jnp.concatenate([pm[c], d2[c]], axis=0), d2[c]) for c in cs]
        pm = [pm[c] + r[c][0:DN_CHUNK] for c in cs]
        d4 = [r[c][DN_CHUNK:] for c in cs]
        yield
        r = [mm(jnp.concatenate([pm[c], d4[c]], axis=0), d4[c]) for c in cs]
        pm = [pm[c] + r[c][0:DN_CHUNK] for c in cs]
        d8 = [r[c][DN_CHUNK:] for c in cs]
        yield
        t0 = [pm[c] + mm(pm[c], d8[c]) for c in cs]
        yield
        m1 = [jnp.where(same32 & jnp.logical_not(same16), a, 0.0) for a in a_m]
        r = [mm(t0[c], m1[c]) for c in cs]
        yield
        t1 = [t0[c] - mm(r[c], t0[c]) for c in cs]
        yield
        m2 = [jnp.where(same32, 0.0, a) for a in a_m]
        r = [mm(t1[c], m2[c]) for c in cs]
        yield
        t2 = [(t1[c] - mm(r[c], t1[c])).astype(BF16) for c in cs]
        yield
        u_c = [_dot(t2[c], bd(vb[c])) for c in cs]
        w_c = [_dot(t2[c], bd(kb[c] * egc[c])) for c in cs]
        yield

        eye_b = eye_f.astype(BF16)
        bdw = [bd(w_c[c]) for c in cs]
        bdu = [bd(u_c[c]) for c in cs]
        kgt = [_dot_nt(eye_b, bd(k_c[c] * ekg[c])).astype(BF16) for c in cs]
        yield
        ntc = [_dot(kgt[c], bdw[c]) for c in cs]
        btc = [_dot(kgt[c], bdu[c]) for c in cs]
        qkb = [q.astype(BF16) for q in qk]
        qt = [q_c[c] * egc[c] - _dot(qkb[c], bdw[c]) for c in cs]
        ou = [_dot(qkb[c], bdu[c]) for c in cs]
        yield

        st = st_ref[...]
        for c in cs:
            r = _dot(jnp.concatenate([ntc[c], qt[c]], axis=0).astype(BF16), bd(st))
            o_c = r[DN_CHUNK:] + ou[c]
            st = st * egl[c] - r[0:DN_CHUNK] + btc[c]
            if c == cs[-1]:
                st_ref[...] = st
            yield
            o_n = o_c * lax.rsqrt(head_mean(o_c * o_c) + EPS) * out_gain
            zrows = slice(HALO + c * DN_CHUNK, HALO + (c + 1) * DN_CHUNK)
            mix_t[sls[c], 256:512] = (o_n * gate_silu(tile, sls[c], 256, 512)).astype(BF16)

    def convs(z_ref, tile):
        mix_t, ybuf_t, ysh_t, yc_t = mix_ref.at[tile], ybuf.at[tile], ysh.at[tile], yc_ref.at[tile]
        for c in range(ts // C_CHUNK):
            sl = slice(c * C_CHUNK, (c + 1) * C_CHUNK)
            zrows = slice(HALO + c * C_CHUNK, HALO + (c + 1) * C_CHUNK)
            cu = z_ref[zrows, COL_C:COL_C + 256]
            cv = z_ref[zrows, COL_C + 256:COL_C + 512]
            mu = jnp.mean(cv, axis=-1, keepdims=True)
            dv = cv - mu
            var = jnp.mean(dv * dv, axis=-1, keepdims=True)
            cvn = dv * lax.rsqrt(var + EPS) * vec_ref[1:2, 256:512] + vec_ref[1:2, 512:768]
            yield
            vblk = jnp.concatenate([jnp.where(head_mask[g], cvn, 0.0).astype(BF16) for g in range(4)], axis=0)
            zc = _dot(cw, vblk) + cbias
            mix_t[sl, 512:768] = (cu * zc * gate_silu(tile, sl, 512, 768)).astype(BF16)
            yield

        for r in range(1, 8):
            ysh_t[r - 1] = ybuf_t[r:r + ts + HALO - 8, :]
            if r % 2 == 1:
                yield
        for rb in range(0, ts, CONV_ROWS):
            acc = None
            for kk in range(D_CONV):
                r0 = HALO - (D_CONV - 1) + kk
                a8, r = (r0 // 8) * 8 + rb, r0 % 8
                win = ybuf_t[a8:a8 + CONV_ROWS, :] if r == 0 else ysh_t[r - 1, a8:a8 + CONV_ROWS, :]
                term = win * wd[kk:kk + 1, :]
                acc = term if acc is None else acc + term
            yc_t[rb:rb + CONV_ROWS, :] = acc + vec_ref[1:2, 768:1024]
            yield
        for c in range(ts // C_CHUNK):
            sl = slice(c * C_CHUNK, (c + 1) * C_CHUNK)
            zrows = slice(HALO + c * C_CHUNK, HALO + (c + 1) * C_CHUNK)
            yc = yc_t[sl, :]
            mu = head_mean(yc)
            dy = yc - mu
            var = head_mean(dy * dy)
            yn = dy * lax.rsqrt(var + EPS) * vec_ref[2:3, 0:256] + vec_ref[2:3, 256:512]
            mix_t[sl, 768:1024] = (_silu(yn) * gate_silu(tile, sl, 768, 1024)).astype(BF16)
            yield

    def out_proj(tile):
        mix_t, x1_t, hg_t = mix_ref.at[tile], x1_ref.at[tile], hg_ref.at[tile]
        rows = slice(tile * ts, (tile + 1) * ts)
        for c0 in range(0, D_MODEL, PROJ_CHUNK):
            cols = slice(c0, c0 + PROJ_CHUNK)
            x1_t[:, cols] = x_ref[0, rows, cols] + _dot(mix_t[...], w_out_ref[:, cols])
            yield
        for r0 in range(0, ts, 64):
            x1 = x1_t[r0:r0 + 64, :]
            ms1 = jnp.mean(x1 * x1, axis=-1, keepdims=True)
            hg_t[r0:r0 + 64, :] = (x1 * lax.rsqrt(ms1 + EPS)).astype(BF16)
        yield
        pb = p_ref[0, rows, :].astype(BF16)
        for c0 in range(0, D_MODEL, PROJ_CHUNK):
            cols = slice(c0, c0 + PROJ_CHUNK)
            ple_gate = _sigmoid(_dot(hg_t[...], gate_w_ref[:, cols]))
            pe = _dot(pb, ple_w_ref[:, cols])
            o_ref[0, rows, cols] = x1_t[:, cols] + ple_gate * pe
            yield

    @pl.when(sp == 0)
    def _():
        za_ref[0:HALO, :] = jnp.zeros((HALO, Z_WIDTH), F32)
        kbuf[:, 0:WINDOW, :] = jnp.zeros((2, WINDOW, 128), BF16)
        vbuf[:, 0:WINDOW, :] = jnp.zeros((2, WINDOW, 128), BF16)
        st_ref[...] = jnp.zeros((HEAD_DIM, 256), F32)
        ybuf[0, 0:HALO, :] = jnp.zeros((HALO, GROUP_WIDTH), F32)

    @pl.when(step == 0)
    def _():
        _run_streams([(in_proj(lambda r0, n: x_ref[0, r0:r0 + n, :], za_ref, 0), 1)])

    zb_ref[0:HALO, :] = za_ref[ts:ts + HALO, :]
    ybuf[1, 0:HALO, :] = ybuf[0, ts:ts + HALO, :]
    _run_streams([(deltanet(za_ref, 0), 1), (convs(za_ref, 0), 1), (attention(za_ref, 0), 2),
                  (in_proj(lambda r0, n: x_ref[0, ts + r0:ts + r0 + n, :], zb_ref, 1), 2)])
    za_ref[0:HALO, :] = zb_ref[ts:ts + HALO, :]
    ybuf[0, 0:HALO, :] = ybuf[1, ts:ts + HALO, :]
    _run_streams([(deltanet(zb_ref, 1), 1), (convs(zb_ref, 1), 1), (attention(zb_ref, 1), 2),
                  (in_proj(lambda r0, n: xn_ref[0, r0:r0 + n, :], za_ref, 0), 2), (out_proj(0), 1)])
    _run_streams([(out_proj(1), 1)])


def _layer_spec(arr, layer):
    return pl.BlockSpec((None,) + arr.shape[1:], lambda g: (layer,) + (0,) * (arr.ndim - 1),
                        pipeline_mode=pl.Buffered(1))


def _layer_call(x, p, layer, cos_t, sin_t, sink, w_in, w_mid, w_tail, w_out, ple_w, gate_w, vec, bconv, cw, cbias,
                dconv, ts):
    bsz, seq, _ = x.shape
    tiles_per_seq = seq // ts
    steps_per_seq = tiles_per_seq // 2
    n_steps = bsz * steps_per_seq
    last_tile = bsz * tiles_per_seq - 1

    def pair_map(g):
        return (g // steps_per_seq, g % steps_per_seq, 0)

    def next_tile_map(g):
        t = jnp.minimum(2 * g + 2, last_tile)
        return (t // tiles_per_seq, t % tiles_per_seq, 0)

    in_specs = [
        pl.BlockSpec(memory_space=pltpu.SMEM),
        pl.BlockSpec((1, 2 * ts, D_MODEL), pair_map),
        pl.BlockSpec((1, ts, D_MODEL), next_tile_map),
        pl.BlockSpec((None, 1, 2 * ts, D_PLE), lambda g: (layer,) + pair_map(g)),
        pl.BlockSpec((2 * ts, 128), lambda g: (g % steps_per_seq, 0)),
        pl.BlockSpec((2 * ts, 128), lambda g: (g % steps_per_seq, 0)),
    ] + [_layer_spec(a, layer) for a in (w_in, w_mid, w_tail, w_out, ple_w, gate_w, vec, bconv, cw, cbias, dconv)]
    scratch = [
        pltpu.VMEM((HALO + ts, Z_WIDTH), F32),
        pltpu.VMEM((HALO + ts, Z_WIDTH), F32),
        pltpu.VMEM((ts, D_MODEL), BF16),
        pltpu.VMEM((2, WINDOW + ts, 128), BF16),
        pltpu.VMEM((2, WINDOW + ts, 128), BF16),
        pltpu.VMEM((HEAD_DIM, 256), F32),
        pltpu.VMEM((2, ts, D_MODEL), BF16),
        pltpu.VMEM((2, HALO + ts, GROUP_WIDTH), F32),
        pltpu.VMEM((2, 7, HALO + ts - 8, GROUP_WIDTH), F32),
        pltpu.VMEM((2, ts, GROUP_WIDTH), F32),
        pltpu.VMEM((2, ts, D_MODEL), F32),
        pltpu.VMEM((2, ts, D_MODEL), BF16),
        pltpu.VMEM((2, ts, D_MODEL), F32),
    ]
    return pl.pallas_call(
        functools.partial(_layer_kernel, ts=ts, steps_per_seq=steps_per_seq, layer=layer),
        grid=(n_steps,),
        in_specs=in_specs,
        out_specs=pl.BlockSpec((1, 2 * ts, D_MODEL), pair_map),
        out_shape=jax.ShapeDtypeStruct(x.shape, F32),
        scratch_shapes=scratch,
        compiler_params=pltpu.CompilerParams(
            dimension_semantics=("arbitrary",),
            vmem_limit_bytes=VMEM_LIMIT_BYTES,
        ),
        name="hybrid_trunk_layer",
    )(sink, x, x, p, cos_t, sin_t, w_in, w_mid, w_tail, w_out, ple_w, gate_w, vec, bconv, cw, cbias, dconv)


W_PREP_BLOCK = 256


def _w_in_prep_kernel(wt_ref, g_ref, head_ref, mid_ref, tail_ref):
    n = W_PREP_BLOCK
    eye = (lax.broadcasted_iota(jnp.int32, (n, n), 0) == lax.broadcasted_iota(jnp.int32, (n, n), 1)).astype(BF16)
    g = g_ref[...]
    d_in = g.shape[1]

    def transposed(r0, rows):
        wb = (wt_ref[r0:r0 + rows, :] * g).astype(BF16)
        return jnp.concatenate([_dot_nt(eye, wb[:, k:k + n]) for k in range(0, d_in, n)], axis=0).astype(BF16)

    for c in range(0, W_IN_SPLIT, n):
        head_ref[:, c:c + n] = transposed(c, n)
    mid0 = W_IN_SPLIT + 2 * N_HEADS
    for c in range(0, mid_ref.shape[1], n):
        mid_ref[:, c:c + n] = transposed(mid0 + c, n)
    lane = lax.broadcasted_iota(jnp.int32, (1, 128), 1)
    tail = transposed(W_IN_SPLIT, 128)
    tail_ref[...] = jnp.where(lane < 2 * N_HEADS, tail, jnp.zeros_like(tail))


def _w_in_prep(w_in_t, gain):
    depth, width, d_in = w_in_t.shape
    mid = width - W_IN_SPLIT - 2 * N_HEADS

    def spec(rows, cols):
        return pl.BlockSpec((None, rows, cols), lambda l: (l, 0, 0))

    return pl.pallas_call(
        _w_in_prep_kernel,
        grid=(depth,),
        in_specs=[spec(width, d_in), spec(1, d_in)],
        out_specs=[spec(d_in, W_IN_SPLIT), spec(d_in, mid), spec(d_in, 128)],
        out_shape=[jax.ShapeDtypeStruct((depth, d_in, c), BF16) for c in (W_IN_SPLIT, mid, 128)],
        compiler_params=pltpu.CompilerParams(vmem_limit_bytes=VMEM_LIMIT_BYTES),
        name="w_in_prep",
    )(w_in_t, gain[:, None, :])


def kernel(x, p, norm_gain, w_in, w_out, a_q_gain, a_k_gain, a_sink, b_conv, b_a_log, b_dt_bias, b_out_gain,
           c_ln_gain, c_ln_bias, c_w_s, c_b_s, d_conv, d_conv_bias, d_ln_gain, d_ln_bias,
           ple_w, ple_gate_norm, ple_gate_w):
    depth = w_in.shape[0]
    bsz, seq, _ = x.shape
    ts = min(SEQ_TILE, seq)
    assert seq % (2 * ts) == 0 and ts % WINDOW == 0

    inv = jnp.tile(ROPE_THETA ** (-jnp.arange(0, HEAD_DIM, 2, dtype=F32) / HEAD_DIM), 4)
    sign = jnp.tile(jnp.repeat(jnp.array([-1.0, 1.0], F32), HEAD_DIM // 2), 2)
    ang_b = (jnp.arange(seq // WINDOW, dtype=F32) * WINDOW)[:, None] * inv[None, :]
    ang_r = jnp.arange(WINDOW, dtype=F32)[:, None] * inv[None, :]
    cb, sb, cr, sr = jnp.cos(ang_b)[:, None, :], jnp.sin(ang_b)[:, None, :], jnp.cos(ang_r)[None], jnp.sin(ang_r)[None]
    cos_t = (cb * cr - sb * sr).reshape(seq, 128)
    sin_t = ((sb * cr + cb * sr) * sign).reshape(seq, 128)

    w_head, w_mid, w_tail = _w_in_prep(jnp.swapaxes(w_in, 1, 2), norm_gain)
    gate_w_b = (ple_gate_w * ple_gate_norm[:, :, None]).astype(BF16)
    w_out_b, ple_w_b = w_out.astype(BF16), ple_w.astype(BF16)

    def z(n):
        return jnp.zeros((depth, n), F32)

    def rep4(v, n):
        return jnp.broadcast_to(v[:, None, :], (depth, n, v.shape[1])).reshape(depth, n * v.shape[1])

    vec = jnp.concatenate([
        rep4(a_q_gain * (HEAD_DIM ** -0.5), 4), rep4(a_k_gain, 2), z(128),
        z(N_HEADS), b_a_log, z(128 - 2 * N_HEADS), z(N_HEADS), b_dt_bias, z(128 - 2 * N_HEADS), z(256),
        rep4(b_out_gain, 4), c_ln_gain, c_ln_bias, d_conv_bias,
        d_ln_gain, d_ln_bias, z(512), z(5 * D_MODEL)], axis=1).astype(F32).reshape(depth, 8, D_MODEL)
    cw = jnp.transpose(c_w_s, (0, 2, 1, 3)).reshape(depth, C_CHUNK, 4 * C_CHUNK)
    cbias = jnp.repeat(jnp.swapaxes(c_b_s, 1, 2), GROUP_WIDTH // 4, axis=2)
    sink = a_sink.astype(F32)

    for i in range(depth):
        x = _layer_call(x, p, i, cos_t, sin_t, sink, w_head, w_mid, w_tail, w_out_b, ple_w_b, gate_w_b, vec,
                        b_conv, cw, cbias, d_conv, ts)
    return x
```

```python
import functools

import jax
import jax.numpy as jnp
from jax import lax
from jax.experimental import pallas as pl
from jax.experimental.pallas import tpu as pltpu

F32 = jnp.float32
BF16 = jnp.bfloat16

EPS = 1e-6
D_MODEL = 1024
D_PLE = 256
GROUP_WIDTH = 256
HEAD_DIM = 64
N_HEADS = 4
WINDOW = 128
ROPE_THETA = 10000.0
B_CONV = 4
DN_CHUNK = 64
C_CHUNK = 128
D_CONV = 31

COL_AQ, COL_AK, COL_AV = 0, 256, 384
COL_B = 512
COL_C = 1280
COL_D = 1792
COL_GATE = 2304
COL_TAIL = 3328
Z_WIDTH = 3456
W_IN_SPLIT = 1280
HALO = 32

SEQ_TILE = 256
PROJ_CHUNK = 256
CONV_ROWS = 32
VMEM_LIMIT_BYTES = 56 * 1024 * 1024


def _sigmoid(x):
    return 0.5 + 0.5 * jnp.tanh(0.5 * x)


def _silu(x):
    h = 0.5 * x
    return h + h * jnp.tanh(h)


def _dot(a, b):
    return jnp.dot(a, b, preferred_element_type=F32)


def _dot_nt(a, b):
    return lax.dot_general(a, b, (((1,), (1,)), ((), ())), preferred_element_type=F32)


def _dot_tn(a, b):
    return lax.dot_general(a, b, (((0,), (0,)), ((), ())), preferred_element_type=F32)


def _split_lhs_dot(a, b_bf16, n):
    acc = None
    r = a
    for i in range(n):
        hi = r.astype(BF16)
        t = _dot(hi, b_bf16)
        acc = t if acc is None else acc + t
        if i + 1 < n:
            r = r - hi.astype(F32)
    return acc


def _split_rhs_dot(a_bf16, b, n):
    acc = None
    r = b
    for i in range(n):
        hi = r.astype(BF16)
        t = _dot(a_bf16, hi)
        acc = t if acc is None else acc + t
        if i + 1 < n:
            r = r - hi.astype(F32)
    return acc


def _run_streams(streams):
    live = list(streams)
    rnd = 0
    while live:
        for item in list(live):
            gen, period = item
            if rnd % period == 0 and next(gen, StopIteration) is StopIteration:
                live.remove(item)
        rnd += 1


def _layer_kernel(sink_ref, x_ref, xn_ref, p_ref, cos_ref, sin_ref, w1_ref, w2_ref, w3_ref, w_out_ref, ple_w_ref,
                  gate_w_ref, vec_ref, bconv_ref, cw_ref, cbias_ref, dconv_ref, o_ref,
                  za_ref, zb_ref, h_ref, kbuf, vbuf, st_ref, mix_ref, ybuf, ysh, yc_ref, x1_ref, hg_ref, sg_ref,
                  *, ts, steps_per_seq, layer):
    step = pl.program_id(0)
    sp = step % steps_per_seq

    lane256 = lax.broadcasted_iota(jnp.int32, (1, 256), 1)
    head_mask = [jnp.where((lane256 // HEAD_DIM) == h, 1.0, 0.0).astype(BF16) for h in range(N_HEADS)]
    r256 = lax.broadcasted_iota(jnp.int32, (256, 256), 0)
    c256 = lax.broadcasted_iota(jnp.int32, (256, 256), 1)
    same_head = (r256 // HEAD_DIM) == (c256 // HEAD_DIM)
    g_sum = jnp.where(same_head, 1.0, 0.0).astype(BF16)
    g_mean = jnp.where(same_head, 1.0 / HEAD_DIM, 0.0).astype(BF16)
    lane128 = lax.broadcasted_iota(jnp.int32, (1, 128), 1)
    first_half = (lane128 % HEAD_DIM) < (HEAD_DIM // 2)
    lo128 = lane128 < HEAD_DIM
    lo_f = jnp.where(lo128, 1.0, 0.0)
    hi_f = 1.0 - lo_f
    q_gain = vec_ref[0:1, 0:256]
    k_gain = vec_ref[0:1, 256:384]
    qi = lax.broadcasted_iota(jnp.int32, (WINDOW, 2 * WINDOW), 0)
    kj = lax.broadcasted_iota(jnp.int32, (WINDOW, 2 * WINDOW), 1)
    rel = qi + WINDOW - kj
    band_neg = jnp.where((rel >= 0) & (rel < WINDOW), 0.0, -jnp.inf)
    wb = bconv_ref[...]
    er = lax.broadcasted_iota(jnp.int32, (128, 256), 0)
    ec = lax.broadcasted_iota(jnp.int32, (128, 256), 1) // HEAD_DIM
    e_beta = jnp.where(er == ec, 1.0, 0.0).astype(BF16)
    e_alpha = jnp.where(er == ec + N_HEADS, 1.0, 0.0).astype(BF16)
    tr = lax.broadcasted_iota(jnp.int32, (ts, ts), 0)
    tc = lax.broadcasted_iota(jnp.int32, (ts, ts), 1)
    l_bd = jnp.where(((tr // DN_CHUNK) == (tc // DN_CHUNK)) & (tr >= tc), 1.0, 0.0).astype(BF16)
    ci = lax.broadcasted_iota(jnp.int32, (DN_CHUNK, 256), 0)
    cj = lax.broadcasted_iota(jnp.int32, (DN_CHUNK, 256), 1) % HEAD_DIM
    eye = ci == cj
    tril = ci >= cj
    strict = ci > cj
    same16 = (ci // 16) == (cj // 16)
    same32 = (ci // 32) == (cj // 32)
    eye_f = jnp.where(eye, 1.0, 0.0)
    strict_f = jnp.where(strict, 1.0, 0.0)
    tril_f = jnp.where(tril, 1.0, 0.0)
    same16_f = jnp.where(same16, 1.0, 0.0)
    mid32_f = jnp.where(same32 & jnp.logical_not(same16), 1.0, 0.0)
    far_f = jnp.where(same32, 0.0, 1.0)
    out_gain = vec_ref[1:2, 0:256]
    wi = lax.broadcasted_iota(jnp.int32, (C_CHUNK, 4 * C_CHUNK), 0)
    wj = lax.broadcasted_iota(jnp.int32, (C_CHUNK, 4 * C_CHUNK), 1) % C_CHUNK
    cw = jnp.where(wi >= wj, cw_ref[...], 0.0).astype(BF16)
    cbias = cbias_ref[...]
    wd = dconv_ref[...]

    def head_mean(v, n=1):
        return _split_lhs_dot(v, g_mean, n)

    def head_sum(v, n=1):
        return _split_lhs_dot(v, g_sum, n)

    def bd(xv):
        xb = xv.astype(BF16)
        return jnp.concatenate([xb * head_mask[h] for h in range(N_HEADS)], axis=0)

    def mm(lhs, xv):
        return _dot(lhs.astype(BF16), bd(xv))

    def rope(v, cos, sin):
        sw = jnp.where(first_half, pltpu.roll(v, 128 - HEAD_DIM // 2, 1), pltpu.roll(v, HEAD_DIM // 2, 1))
        return v * cos + sw * sin

    proj_chunks = ([(w1_ref, c, c, PROJ_CHUNK) for c in range(0, W_IN_SPLIT, PROJ_CHUNK)]
                   + [(w2_ref, c, W_IN_SPLIT + c, PROJ_CHUNK) for c in range(0, COL_TAIL - W_IN_SPLIT, PROJ_CHUNK)]
                   + [(w3_ref, 0, COL_TAIL, Z_WIDTH - COL_TAIL)])

    def in_proj(x_rows, z_ref, slot):
        for r0 in range(0, ts, 64):
            xv = x_rows(r0, 64)
            ms = jnp.mean(xv * xv, axis=-1, keepdims=True)
            h_ref[r0:r0 + 64, :] = (xv * lax.rsqrt(ms + EPS)).astype(BF16)
        for w_ref, wc, zc, width in proj_chunks:
            zv = _dot(h_ref[...], w_ref[:, wc:wc + width])
            if COL_C <= zc < COL_D:
                zv = 0.5 * zv * (1.0 + jnp.tanh(0.7978845608028654 * (zv + 0.044715 * (zv * zv * zv))))
            if zc == COL_D + 256:
                ybuf[slot, HALO:HALO + ts, :] = z_ref[HALO:HALO + ts, COL_D:COL_D + 256] * _sigmoid(zv)
            elif zc >= COL_GATE and zc < COL_TAIL:
                sg_ref[slot, :, zc - COL_GATE:zc - COL_GATE + width] = _silu(zv)
            else:
                z_ref[HALO:HALO + ts, zc:zc + width] = zv
            yield

    def gate_silu(tile, rows, c0, c1):
        return sg_ref[tile, rows, c0:c1]

    def attention(z_ref, tile):
        mix_t = mix_ref.at[tile]
        for blk in range(ts // WINDOW):
            rows = slice(HALO + blk * WINDOW, HALO + (blk + 1) * WINDOW)
            q = z_ref[rows, COL_AQ:COL_AQ + 256]
            k = z_ref[rows, COL_AK:COL_AK + 128]
            v = z_ref[rows, COL_AV:COL_AV + 128]
            cos = cos_ref[tile * ts + blk * WINDOW:tile * ts + (blk + 1) * WINDOW, :]
            sin = sin_ref[tile * ts + blk * WINDOW:tile * ts + (blk + 1) * WINDOW, :]
            qn = q * lax.rsqrt(head_mean(q * q) + EPS) * q_gain
            kn = k * lax.rsqrt(_split_lhs_dot(k * k, g_mean[0:128, 0:128], 1) + EPS) * k_gain
            qa = rope(qn[:, 0:128], cos, sin)
            qb = rope(qn[:, 128:256], cos, sin)
            kr = rope(kn, cos, sin)
            cur = slice(WINDOW + blk * WINDOW, 2 * WINDOW + blk * WINDOW)
            both = slice(blk * WINDOW, (blk + 2) * WINDOW)
            kbuf[0, cur, :] = kr.astype(BF16)
            kbuf[1, cur, :] = pltpu.roll(kr, HEAD_DIM, 1).astype(BF16)
            vbuf[0, cur, :] = v.astype(BF16)
            vbuf[1, cur, :] = pltpu.roll(v, HEAD_DIM, 1).astype(BF16)
            lhs = [(qa * lo_f).astype(BF16), (qa * hi_f).astype(BF16), (qb * lo_f).astype(BF16), (qb * hi_f).astype(BF16)]
            yield
            neg = band_neg
            if blk == 0 and tile == 0:
                neg = jnp.where(kj >= jnp.where(sp > 0, 0, WINDOW), neg, -jnp.inf)
            outs = []
            for r in range(N_HEADS):
                slot = 0 if r in (0, 3) else 1
                sc = _dot_nt(lhs[r], kbuf[slot, both, :]) + neg
                sink = sink_ref[layer, r]
                m = jnp.maximum(jnp.max(sc, axis=-1, keepdims=True), sink)
                e = jnp.exp(sc - m)
                den = jnp.sum(e, axis=-1, keepdims=True) + jnp.exp(sink - m)
                outs.append(_dot(e.astype(BF16), vbuf[slot, both, :]) / den)
                yield
            o_a0 = jnp.where(lo128, outs[0], outs[1])
            o_a1 = jnp.where(lo128, outs[2], outs[3])
            mrows = slice(blk * WINDOW, (blk + 1) * WINDOW)
            mix_t[mrows, 0:128] = (o_a0 * gate_silu(tile, mrows, 0, 128)).astype(BF16)
            mix_t[mrows, 128:256] = (o_a1 * gate_silu(tile, mrows, 128, 256)).astype(BF16)
            yield
        kbuf[:, 0:WINDOW, :] = kbuf[:, ts:ts + WINDOW, :]
        vbuf[:, 0:WINDOW, :] = vbuf[:, ts:ts + WINDOW, :]

    def deltanet(z_ref, tile):
        mix_t = mix_ref.at[tile]
        cs = range(ts // DN_CHUNK)
        sls = [slice(c * DN_CHUNK, (c + 1) * DN_CHUNK) for c in cs]
        q_c, k_c, v_c = [], [], []
        for c in cs:
            r0 = HALO + c * DN_CHUNK - (B_CONV - 1)
            acc = None
            for kk in range(B_CONV):
                term = z_ref[r0 + kk:r0 + kk + DN_CHUNK, COL_B:COL_B + 768] * wb[kk:kk + 1, :]
                acc = term if acc is None else acc + term
            qkv = _silu(acc)
            q, k = qkv[:, 0:256], qkv[:, 256:512]
            ss = head_sum(jnp.concatenate([q * q, k * k], axis=0))
            q_c.append(q * lax.rsqrt(ss[0:DN_CHUNK] + EPS) * (HEAD_DIM ** -0.5))
            k_c.append(k * lax.rsqrt(ss[DN_CHUNK:] + EPS))
            v_c.append(qkv[:, 512:768])
            yield

        tail = z_ref[HALO:HALO + ts, COL_TAIL:COL_TAIL + 128]
        beta = _sigmoid(tail)
        xg = tail + vec_ref[0:1, 640:768]
        softplus = jnp.maximum(xg, 0.0) + jnp.log1p(jnp.exp(-jnp.abs(xg)))
        gdec = -jnp.exp(vec_ref[0:1, 512:640]) * softplus
        beta_f = _split_lhs_dot(beta, e_beta, 1)
        gdec_f = _split_lhs_dot(gdec, e_alpha, 2)
        gc = _split_rhs_dot(l_bd, gdec_f, 2)
        yield

        gc_c = [gc[s] for s in sls]
        gl = [g[DN_CHUNK - 1:DN_CHUNK, :] for g in gc_c]
        egc = [jnp.exp(g) for g in gc_c]
        ekg = [jnp.exp(gl[c] - gc_c[c]) for c in cs]
        egl = [jnp.exp(g) for g in gl]
        kb = [k_c[c] * beta_f[sls[c]] for c in cs]
        vb = [v_c[c] * beta_f[sls[c]] for c in cs]
        grow = [jnp.sum(g * eye_f, axis=0, keepdims=True) for g in gc_c]
        decay = [jnp.exp(jnp.minimum(gc_c[c] - grow[c], 0.0)) for c in cs]
        yield
        aq = [_dot_nt(jnp.concatenate([kb[c], q_c[c]], axis=0).astype(BF16), bd(k_c[c])) for c in cs]
        a_m = [aq[c][0:DN_CHUNK] * (decay[c] * strict_f) for c in cs]
        qk = [aq[c][DN_CHUNK:2 * DN_CHUNK] * (decay[c] * tril_f) for c in cs]
        yield

        d1 = [a * same16_f for a in a_m]
        pm = [eye_f - d for d in d1]
        d2 = [mm(d, d) for d in d1]
        yield
        r = [mm(jnp.concatenate([pm[c], d2[c]], axis=0), d2[c]) for c in cs]
        pm = [pm[c] + r[c][0:DN_CHUNK] for c in cs]
        d4 = [r[c][DN_CHUNK:] for c in cs]
        yield
        r = [mm(jnp.concatenate([pm[c], d4[c]], axis=0), d4[c]) for c in cs]
        pm = [pm[c] + r[c][0:DN_CHUNK] for c in cs]
        d8 = [r[c][DN_CHUNK:] for c in cs]
        yield
        t0 = [pm[c] + mm(pm[c], d8[c]) for c in cs]
        yield
        m1 = [a * mid32_f for a in a_m]
        r = [mm(t0[c], m1[c]) for c in cs]
        yield
        t1 = [t0[c] - mm(r[c], t0[c]) for c in cs]
        yield
        m2 = [a * far_f for a in a_m]
        r = [mm(t1[c], m2[c]) for c in cs]
        yield
        t2 = [(t1[c] - mm(r[c], t1[c])).astype(BF16) for c in cs]
        yield
        u_c = [_dot(t2[c], bd(vb[c])) for c in cs]
        w_c = [_dot(t2[c], bd(kb[c] * egc[c])) for c in cs]
        yield

        eye_b = eye_f.astype(BF16)
        bdw = [bd(w_c[c]) for c in cs]
        bdu = [bd(u_c[c]) for c in cs]
        kgt = [_dot_nt(eye_b, bd(k_c[c] * ekg[c])).astype(BF16) for c in cs]
        yield
        ntc = [_dot(kgt[c], bdw[c]) for c in cs]
        btc = [_dot(kgt[c], bdu[c]) for c in cs]
        qkb = [q.astype(BF16) for q in qk]
        qt = [q_c[c] * egc[c] - _dot(qkb[c], bdw[c]) for c in cs]
        ou = [_dot(qkb[c], bdu[c]) for c in cs]
        yield

        st = st_ref[...]
        for c in cs:
            r = _dot(jnp.concatenate([ntc[c], qt[c]], axis=0).astype(BF16), bd(st))
            o_c = r[DN_CHUNK:] + ou[c]
            st = st * egl[c] - r[0:DN_CHUNK] + btc[c]
            if c == cs[-1]:
                st_ref[...] = st
            yield
            o_n = o_c * lax.rsqrt(head_mean(o_c * o_c) + EPS) * out_gain
            zrows = slice(HALO + c * DN_CHUNK, HALO + (c + 1) * DN_CHUNK)
            mix_t[sls[c], 256:512] = (o_n * gate_silu(tile, sls[c], 256, 512)).astype(BF16)

    def convs(z_ref, tile):
        mix_t, ybuf_t, ysh_t, yc_t = mix_ref.at[tile], ybuf.at[tile], ysh.at[tile], yc_ref.at[tile]
        for c in range(ts // C_CHUNK):
            sl = slice(c * C_CHUNK, (c + 1) * C_CHUNK)
            zrows = slice(HALO + c * C_CHUNK, HALO + (c + 1) * C_CHUNK)
            cu = z_ref[zrows, COL_C:COL_C + 256]
            cv = z_ref[zrows, COL_C + 256:COL_C + 512]
            mu = jnp.mean(cv, axis=-1, keepdims=True)
            dv = cv - mu
            var = jnp.mean(dv * dv, axis=-1, keepdims=True)
            cvn = dv * lax.rsqrt(var + EPS) * vec_ref[1:2, 256:512] + vec_ref[1:2, 512:768]
            yield
            cvb = cvn.astype(BF16)
            vblk = jnp.concatenate([cvb * head_mask[g] for g in range(4)], axis=0)
            zc = _dot(cw, vblk) + cbias
            mix_t[sl, 512:768] = (cu * zc * gate_silu(tile, sl, 512, 768)).astype(BF16)
            yield

        for r in range(1, 8):
            ysh_t[r - 1] = ybuf_t[r:r + ts + HALO - 8, :]
            if r % 2 == 1:
                yield
        for rb in range(0, ts, CONV_ROWS):
            acc = None
            for kk in range(D_CONV):
                r0 = HALO - (D_CONV - 1) + kk
                a8, r = (r0 // 8) * 8 + rb, r0 % 8
                win = ybuf_t[a8:a8 + CONV_ROWS, :] if r == 0 else ysh_t[r - 1, a8:a8 + CONV_ROWS, :]
                term = win * wd[kk:kk + 1, :]
                acc = term if acc is None else acc + term
            yc_t[rb:rb + CONV_ROWS, :] = acc + vec_ref[1:2, 768:1024]
            yield
        for c in range(ts // C_CHUNK):
            sl = slice(c * C_CHUNK, (c + 1) * C_CHUNK)
            zrows = slice(HALO + c * C_CHUNK, HALO + (c + 1) * C_CHUNK)
            yc = yc_t[sl, :]
            mu = head_mean(yc)
            dy = yc - mu
            var = head_mean(dy * dy)
            yn = dy * lax.rsqrt(var + EPS) * vec_ref[2:3, 0:256] + vec_ref[2:3, 256:512]
            mix_t[sl, 768:1024] = (_silu(yn) * gate_silu(tile, sl, 768, 1024)).astype(BF16)
            yield

    def out_proj(tile):
        mix_t, x1_t, hg_t = mix_ref.at[tile], x1_ref.at[tile], hg_ref.at[tile]
        rows = slice(tile * ts, (tile + 1) * ts)
        for c0 in range(0, D_MODEL, PROJ_CHUNK):
            cols = slice(c0, c0 + PROJ_CHUNK)
            x1_t[:, cols] = x_ref[0, rows, cols] + _dot(mix_t[...], w_out_ref[:, cols])
            yield
        for r0 in range(0, ts, 64):
            x1 = x1_t[r0:r0 + 64, :]
            ms1 = jnp.mean(x1 * x1, axis=-1, keepdims=True)
            hg_t[r0:r0 + 64, :] = (x1 * lax.rsqrt(ms1 + EPS)).astype(BF16)
        yield
        pb = p_ref[0, rows, :].astype(BF16)
        for c0 in range(0, D_MODEL, PROJ_CHUNK):
            cols = slice(c0, c0 + PROJ_CHUNK)
            ple_gate = _sigmoid(_dot(hg_t[...], gate_w_ref[:, cols]))
            pe = _dot(pb, ple_w_ref[:, cols])
            o_ref[0, rows, cols] = x1_t[:, cols] + ple_gate * pe
            yield

    @pl.when(sp == 0)
    def _():
        za_ref[0:HALO, :] = jnp.zeros((HALO, Z_WIDTH), F32)
        kbuf[:, 0:WINDOW, :] = jnp.zeros((2, WINDOW, 128), BF16)
        vbuf[:, 0:WINDOW, :] = jnp.zeros((2, WINDOW, 128), BF16)
        st_ref[...] = jnp.zeros((HEAD_DIM, 256), F32)
        ybuf[0, 0:HALO, :] = jnp.zeros((HALO, GROUP_WIDTH), F32)

    @pl.when(step == 0)
    def _():
        _run_streams([(in_proj(lambda r0, n: x_ref[0, r0:r0 + n, :], za_ref, 0), 1)])

    zb_ref[0:HALO, :] = za_ref[ts:ts + HALO, :]
    ybuf[1, 0:HALO, :] = ybuf[0, ts:ts + HALO, :]
    _run_streams([(deltanet(za_ref, 0), 1), (convs(za_ref, 0), 1), (attention(za_ref, 0), 2),
                  (in_proj(lambda r0, n: x_ref[0, ts + r0:ts + r0 + n, :], zb_ref, 1), 2)])
    za_ref[0:HALO, :] = zb_ref[ts:ts + HALO, :]
    ybuf[0, 0:HALO, :] = ybuf[1, ts:ts + HALO, :]
    _run_streams([(deltanet(zb_ref, 1), 1), (convs(zb_ref, 1), 1), (attention(zb_ref, 1), 2),
                  (in_proj(lambda r0, n: xn_ref[0, r0:r0 + n, :], za_ref, 0), 2), (out_proj(0), 1)])
    _run_streams([(out_proj(1), 1)])


def _layer_spec(arr, layer):
    return pl.BlockSpec((None,) + arr.shape[1:], lambda g: (layer,) + (0,) * (arr.ndim - 1),
                        pipeline_mode=pl.Buffered(1))


def _layer_call(x, p, layer, cos_t, sin_t, sink, w_in, w_mid, w_tail, w_out, ple_w, gate_w, vec, bconv, cw, cbias,
                dconv, ts):
    bsz, seq, _ = x.shape
    tiles_per_seq = seq // ts
    steps_per_seq = tiles_per_seq // 2
    n_steps = bsz * steps_per_seq
    last_tile = bsz * tiles_per_seq - 1

    def pair_map(g):
        return (g // steps_per_seq, g % steps_per_seq, 0)

    def next_tile_map(g):
        t = jnp.minimum(2 * g + 2, last_tile)
        return (t // tiles_per_seq, t % tiles_per_seq, 0)

    in_specs = [
        pl.BlockSpec(memory_space=pltpu.SMEM),
        pl.BlockSpec((1, 2 * ts, D_MODEL), pair_map),
        pl.BlockSpec((1, ts, D_MODEL), next_tile_map),
        pl.BlockSpec((None, 1, 2 * ts, D_PLE), lambda g: (layer,) + pair_map(g)),
        pl.BlockSpec((2 * ts, 128), lambda g: (g % steps_per_seq, 0)),
        pl.BlockSpec((2 * ts, 128), lambda g: (g % steps_per_seq, 0)),
    ] + [_layer_spec(a, layer) for a in (w_in, w_mid, w_tail, w_out, ple_w, gate_w, vec, bconv, cw, cbias, dconv)]
    scratch = [
        pltpu.VMEM((HALO + ts, Z_WIDTH), F32),
        pltpu.VMEM((HALO + ts, Z_WIDTH), F32),
        pltpu.VMEM((ts, D_MODEL), BF16),
        pltpu.VMEM((2, WINDOW + ts, 128), BF16),
        pltpu.VMEM((2, WINDOW + ts, 128), BF16),
        pltpu.VMEM((HEAD_DIM, 256), F32),
        pltpu.VMEM((2, ts, D_MODEL), BF16),
        pltpu.VMEM((2, HALO + ts, GROUP_WIDTH), F32),
        pltpu.VMEM((2, 7, HALO + ts - 8, GROUP_WIDTH), F32),
        pltpu.VMEM((2, ts, GROUP_WIDTH), F32),
        pltpu.VMEM((2, ts, D_MODEL), F32),
        pltpu.VMEM((2, ts, D_MODEL), BF16),
        pltpu.VMEM((2, ts, D_MODEL), F32),
    ]
    return pl.pallas_call(
        functools.partial(_layer_kernel, ts=ts, steps_per_seq=steps_per_seq, layer=layer),
        grid=(n_steps,),
        in_specs=in_specs,
        out_specs=pl.BlockSpec((1, 2 * ts, D_MODEL), pair_map),
        out_shape=jax.ShapeDtypeStruct(x.shape, F32),
        scratch_shapes=scratch,
        compiler_params=pltpu.CompilerParams(
            dimension_semantics=("arbitrary",),
            vmem_limit_bytes=VMEM_LIMIT_BYTES,
        ),
        name="hybrid_trunk_layer",
    )(sink, x, x, p, cos_t, sin_t, w_in, w_mid, w_tail, w_out, ple_w, gate_w, vec, bconv, cw, cbias, dconv)


W_PREP_BLOCK = 256


def _w_in_prep_kernel(wt_ref, g_ref, head_ref, mid_ref, tail_ref):
    n = W_PREP_BLOCK
    eye = (lax.broadcasted_iota(jnp.int32, (n, n), 0) == lax.broadcasted_iota(jnp.int32, (n, n), 1)).astype(BF16)
    g = g_ref[...]
    d_in = g.shape[1]

    def transposed(r0, rows):
        wb = (wt_ref[r0:r0 + rows, :] * g).astype(BF16)
        return jnp.concatenate([_dot_nt(eye, wb[:, k:k + n]) for k in range(0, d_in, n)], axis=0).astype(BF16)

    for c in range(0, W_IN_SPLIT, n):
        head_ref[:, c:c + n] = transposed(c, n)
    mid0 = W_IN_SPLIT + 2 * N_HEADS
    for c in range(0, mid_ref.shape[1], n):
        mid_ref[:, c:c + n] = transposed(mid0 + c, n)
    lane = lax.broadcasted_iota(jnp.int32, (1, 128), 1)
    tail = transposed(W_IN_SPLIT, 128)
    tail_ref[...] = jnp.where(lane < 2 * N_HEADS, tail, jnp.zeros_like(tail))


def _w_in_prep(w_in_t, gain):
    depth, width, d_in = w_in_t.shape
    mid = width - W_IN_SPLIT - 2 * N_HEADS

    def spec(rows, cols):
        return pl.BlockSpec((None, rows, cols), lambda l: (l, 0, 0))

    return pl.pallas_call(
        _w_in_prep_kernel,
        grid=(depth,),
        in_specs=[spec(width, d_in), spec(1, d_in)],
        out_specs=[spec(d_in, W_IN_SPLIT), spec(d_in, mid), spec(d_in, 128)],
        out_shape=[jax.ShapeDtypeStruct((depth, d_in, c), BF16) for c in (W_IN_SPLIT, mid, 128)],
        compiler_params=pltpu.CompilerParams(vmem_limit_bytes=VMEM_LIMIT_BYTES),
        name="w_in_prep",
    )(w_in_t, gain[:, None, :])


def kernel(x, p, norm_gain, w_in, w_out, a_q_gain, a_k_gain, a_sink, b_conv, b_a_log, b_dt_bias, b_out_gain,
           c_ln_gain, c_ln_bias, c_w_s, c_b_s, d_conv, d_conv_bias, d_ln_gain, d_ln_bias,
           ple_w, ple_gate_norm, ple_gate_w):
    depth = w_in.shape[0]
    bsz, seq, _ = x.shape
    ts = min(SEQ_TILE, seq)
    assert seq % (2 * ts) == 0 and ts % WINDOW == 0

    inv = jnp.tile(ROPE_THETA ** (-jnp.arange(0, HEAD_DIM, 2, dtype=F32) / HEAD_DIM), 4)
    sign = jnp.tile(jnp.repeat(jnp.array([-1.0, 1.0], F32), HEAD_DIM // 2), 2)
    ang_b = (jnp.arange(seq // WINDOW, dtype=F32) * WINDOW)[:, None] * inv[None, :]
    ang_r = jnp.arange(WINDOW, dtype=F32)[:, None] * inv[None, :]
    cb, sb, cr, sr = jnp.cos(ang_b)[:, None, :], jnp.sin(ang_b)[:, None, :], jnp.cos(ang_r)[None], jnp.sin(ang_r)[None]
    cos_t = (cb * cr - sb * sr).reshape(seq, 128)
    sin_t = ((sb * cr + cb * sr) * sign).reshape(seq, 128)

    w_head, w_mid, w_tail = _w_in_prep(jnp.swapaxes(w_in, 1, 2), norm_gain)
    gate_w_b = (ple_gate_w * ple_gate_norm[:, :, None]).astype(BF16)
    w_out_b, ple_w_b = w_out.astype(BF16), ple_w.astype(BF16)

    def z(n):
        return jnp.zeros((depth, n), F32)

    def rep4(v, n):
        return jnp.broadcast_to(v[:, None, :], (depth, n, v.shape[1])).reshape(depth, n * v.shape[1])

    vec = jnp.concatenate([
        rep4(a_q_gain * (HEAD_DIM ** -0.5), 4), rep4(a_k_gain, 2), z(128),
        z(N_HEADS), b_a_log, z(128 - 2 * N_HEADS), z(N_HEADS), b_dt_bias, z(128 - 2 * N_HEADS), z(256),
        rep4(b_out_gain, 4), c_ln_gain, c_ln_bias, d_conv_bias,
        d_ln_gain, d_ln_bias, z(512), z(5 * D_MODEL)], axis=1).astype(F32).reshape(depth, 8, D_MODEL)
    cw = jnp.transpose(c_w_s, (0, 2, 1, 3)).reshape(depth, C_CHUNK, 4 * C_CHUNK)
    cbias = jnp.repeat(jnp.swapaxes(c_b_s, 1, 2), GROUP_WIDTH // 4, axis=2)
    sink = a_sink.astype(F32)

    for i in range(depth):
        x = _layer_call(x, p, i, cos_t, sin_t, sink, w_head, w_mid, w_tail, w_out_b, ple_w_b, gate_w_b, vec,
                        b_conv, cw, cbias, d_conv, ts)
    return x
```

```python
import functools

import jax
import jax.numpy as jnp
from jax import lax
from jax.experimental import pallas as pl
from jax.experimental.pallas import tpu as pltpu

F32 = jnp.float32
BF16 = jnp.bfloat16

EPS = 1e-6
D_MODEL = 1024
D_PLE = 256
GROUP_WIDTH = 256
HEAD_DIM = 64
N_HEADS = 4
WINDOW = 128
ROPE_THETA = 10000.0
B_CONV = 4
DN_CHUNK = 64
C_CHUNK = 128
D_CONV = 31

COL_AQ, COL_AK, COL_AV = 0, 256, 384
COL_B = 512
COL_C = 1280
COL_D = 1792
COL_GATE = 2304
COL_TAIL = 3328
Z_WIDTH = 3456
W_IN_SPLIT = 1280
HALO = 32

SEQ_TILE = 256
PROJ_CHUNK = 256
CONV_ROWS = 32
VMEM_LIMIT_BYTES = 56 * 1024 * 1024


def _sigmoid(x):
    return 0.5 + 0.5 * jnp.tanh(0.5 * x)


def _silu(x):
    h = 0.5 * x
    return h + h * jnp.tanh(h)


def _dot(a, b):
    return jnp.dot(a, b, preferred_element_type=F32)


def _dot_nt(a, b):
    return lax.dot_general(a, b, (((1,), (1,)), ((), ())), preferred_element_type=F32)


def _dot_tn(a, b):
    return lax.dot_general(a, b, (((0,), (0,)), ((), ())), preferred_element_type=F32)


def _split_lhs_dot(a, b_bf16, n):
    acc = None
    r = a
    for i in range(n):
        hi = r.astype(BF16)
        t = _dot(hi, b_bf16)
        acc = t if acc is None else acc + t
        if i + 1 < n:
            r = r - hi.astype(F32)
    return acc


def _split_rhs_dot(a_bf16, b, n):
    acc = None
    r = b
    for i in range(n):
        hi = r.astype(BF16)
        t = _dot(a_bf16, hi)
        acc = t if acc is None else acc + t
        if i + 1 < n:
            r = r - hi.astype(F32)
    return acc


def _run_streams(streams):
    live = list(streams)
    rnd = 0
    while live:
        for item in list(live):
            gen, period = item
            if rnd % period == 0 and next(gen, StopIteration) is StopIteration:
                live.remove(item)
        rnd += 1


def _layer_kernel(sink_ref, x_ref, xn_ref, p_ref, cos_ref, sin_ref, w1_ref, w2_ref, w3_ref, w_out_ref, ple_w_ref,
                  gate_w_ref, vec_ref, bconv_ref, cw_ref, cbias_ref, dconv_ref, o_ref,
                  za_ref, zb_ref, h_ref, kbuf, vbuf, st_ref, mix_ref, ybuf, ysh, yc_ref, x1_ref, hg_ref, sg_ref,
                  *, ts, steps_per_seq, layer):
    step = pl.program_id(0)
    sp = step % steps_per_seq

    lane256 = lax.broadcasted_iota(jnp.int32, (1, 256), 1)
    head_mask = [(lane256 // HEAD_DIM) == h for h in range(N_HEADS)]
    r256 = lax.broadcasted_iota(jnp.int32, (256, 256), 0)
    c256 = lax.broadcasted_iota(jnp.int32, (256, 256), 1)
    same_head = (r256 // HEAD_DIM) == (c256 // HEAD_DIM)
    g_sum = jnp.where(same_head, 1.0, 0.0).astype(BF16)
    g_mean = jnp.where(same_head, 1.0 / HEAD_DIM, 0.0).astype(BF16)
    lane128 = lax.broadcasted_iota(jnp.int32, (1, 128), 1)
    first_half = (lane128 % HEAD_DIM) < (HEAD_DIM // 2)
    lo128 = lane128 < HEAD_DIM
    q_gain = vec_ref[0:1, 0:256]
    k_gain = vec_ref[0:1, 256:384]
    qi = lax.broadcasted_iota(jnp.int32, (WINDOW, 2 * WINDOW), 0)
    kj = lax.broadcasted_iota(jnp.int32, (WINDOW, 2 * WINDOW), 1)
    rel = qi + WINDOW - kj
    band = (rel >= 0) & (rel < WINDOW)
    wb = bconv_ref[...]
    er = lax.broadcasted_iota(jnp.int32, (128, 256), 0)
    ec = lax.broadcasted_iota(jnp.int32, (128, 256), 1) // HEAD_DIM
    e_beta = jnp.where(er == ec, 1.0, 0.0).astype(BF16)
    e_alpha = jnp.where(er == ec + N_HEADS, 1.0, 0.0).astype(BF16)
    tr = lax.broadcasted_iota(jnp.int32, (ts, ts), 0)
    tc = lax.broadcasted_iota(jnp.int32, (ts, ts), 1)
    l_bd = jnp.where(((tr // DN_CHUNK) == (tc // DN_CHUNK)) & (tr >= tc), 1.0, 0.0).astype(BF16)
    ci = lax.broadcasted_iota(jnp.int32, (DN_CHUNK, 256), 0)
    cj = lax.broadcasted_iota(jnp.int32, (DN_CHUNK, 256), 1) % HEAD_DIM
    eye = ci == cj
    tril = ci >= cj
    strict = ci > cj
    same16 = (ci // 16) == (cj // 16)
    same32 = (ci // 32) == (cj // 32)
    eye_f = jnp.where(eye, 1.0, 0.0)
    out_gain = vec_ref[1:2, 0:256]
    wi = lax.broadcasted_iota(jnp.int32, (C_CHUNK, 4 * C_CHUNK), 0)
    wj = lax.broadcasted_iota(jnp.int32, (C_CHUNK, 4 * C_CHUNK), 1) % C_CHUNK
    cw = jnp.where(wi >= wj, cw_ref[...], 0.0).astype(BF16)
    cbias = cbias_ref[...]
    wd = dconv_ref[...]

    def head_mean(v, n=1):
        return _split_lhs_dot(v, g_mean, n)

    def head_sum(v, n=1):
        return _split_lhs_dot(v, g_sum, n)

    def bd(xv):
        xb = xv.astype(BF16)
        zero = jnp.zeros_like(xb)
        return jnp.concatenate([jnp.where(head_mask[h], xb, zero) for h in range(N_HEADS)], axis=0)

    def mm(lhs, xv):
        return _dot(lhs.astype(BF16), bd(xv))

    def rope(v, cos, sin):
        sw = jnp.where(first_half, pltpu.roll(v, 128 - HEAD_DIM // 2, 1), pltpu.roll(v, HEAD_DIM // 2, 1))
        return v * cos + sw * sin

    proj_chunks = ([(w1_ref, c, c, PROJ_CHUNK) for c in range(0, W_IN_SPLIT, PROJ_CHUNK)]
                   + [(w2_ref, c, W_IN_SPLIT + c, PROJ_CHUNK) for c in range(0, COL_TAIL - W_IN_SPLIT, PROJ_CHUNK)]
                   + [(w3_ref, 0, COL_TAIL, Z_WIDTH - COL_TAIL)])

    def in_proj(x_rows, z_ref, slot):
        for r0 in range(0, ts, 64):
            xv = x_rows(r0, 64)
            ms = jnp.mean(xv * xv, axis=-1, keepdims=True)
            h_ref[r0:r0 + 64, :] = (xv * lax.rsqrt(ms + EPS)).astype(BF16)
        for w_ref, wc, zc, width in proj_chunks:
            zv = _dot(h_ref[...], w_ref[:, wc:wc + width])
            if COL_C <= zc < COL_D:
                zv = 0.5 * zv * (1.0 + jnp.tanh(0.7978845608028654 * (zv + 0.044715 * (zv * zv * zv))))
            if zc == COL_D + 256:
                ybuf[slot, HALO:HALO + ts, :] = z_ref[HALO:HALO + ts, COL_D:COL_D + 256] * _sigmoid(zv)
            elif zc >= COL_GATE and zc < COL_TAIL:
                sg_ref[slot, :, zc - COL_GATE:zc - COL_GATE + width] = _silu(zv)
            else:
                z_ref[HALO:HALO + ts, zc:zc + width] = zv
            yield

    def gate_silu(tile, rows, c0, c1):
        return sg_ref[tile, rows, c0:c1]

    def attention(z_ref, tile):
        mix_t = mix_ref.at[tile]
        for blk in range(ts // WINDOW):
            rows = slice(HALO + blk * WINDOW, HALO + (blk + 1) * WINDOW)
            q = z_ref[rows, COL_AQ:COL_AQ + 256]
            k = z_ref[rows, COL_AK:COL_AK + 128]
            v = z_ref[rows, COL_AV:COL_AV + 128]
            cos = cos_ref[tile * ts + blk * WINDOW:tile * ts + (blk + 1) * WINDOW, :]
            sin = sin_ref[tile * ts + blk * WINDOW:tile * ts + (blk + 1) * WINDOW, :]
            qn = q * lax.rsqrt(head_mean(q * q) + EPS) * q_gain
            kn = k * lax.rsqrt(_split_lhs_dot(k * k, g_mean[0:128, 0:128], 1) + EPS) * k_gain
            qa = rope(qn[:, 0:128], cos, sin)
            qb = rope(qn[:, 128:256], cos, sin)
            kr = rope(kn, cos, sin)
            cur = slice(WINDOW + blk * WINDOW, 2 * WINDOW + blk * WINDOW)
            both = slice(blk * WINDOW, (blk + 2) * WINDOW)
            kbuf[0, cur, :] = kr.astype(BF16)
            kbuf[1, cur, :] = pltpu.roll(kr, HEAD_DIM, 1).astype(BF16)
            vbuf[0, cur, :] = v.astype(BF16)
            vbuf[1, cur, :] = pltpu.roll(v, HEAD_DIM, 1).astype(BF16)
            lhs = [jnp.where(lo128, qa, 0.0).astype(BF16), jnp.where(lo128, 0.0, qa).astype(BF16),
                   jnp.where(lo128, qb, 0.0).astype(BF16), jnp.where(lo128, 0.0, qb).astype(BF16)]
            yield
            valid = band
            if blk == 0 and tile == 0:
                valid = valid & (kj >= jnp.where(sp > 0, 0, WINDOW))
            outs = []
            for r in range(N_HEADS):
                slot = 0 if r in (0, 3) else 1
                sc = jnp.where(valid, _dot_nt(lhs[r], kbuf[slot, both, :]), -jnp.inf)
                sink = sink_ref[layer, r]
                m = jnp.maximum(jnp.max(sc, axis=-1, keepdims=True), sink)
                e = jnp.exp(sc - m)
                den = jnp.sum(e, axis=-1, keepdims=True) + jnp.exp(sink - m)
                outs.append(_dot(e.astype(BF16), vbuf[slot, both, :]) / den)
                yield
            o_a0 = jnp.where(lo128, outs[0], outs[1])
            o_a1 = jnp.where(lo128, outs[2], outs[3])
            mrows = slice(blk * WINDOW, (blk + 1) * WINDOW)
            mix_t[mrows, 0:128] = (o_a0 * gate_silu(tile, mrows, 0, 128)).astype(BF16)
            mix_t[mrows, 128:256] = (o_a1 * gate_silu(tile, mrows, 128, 256)).astype(BF16)
            yield
        kbuf[:, 0:WINDOW, :] = kbuf[:, ts:ts + WINDOW, :]
        vbuf[:, 0:WINDOW, :] = vbuf[:, ts:ts + WINDOW, :]

    def deltanet(z_ref, tile):
        mix_t = mix_ref.at[tile]
        cs = range(ts // DN_CHUNK)
        sls = [slice(c * DN_CHUNK, (c + 1) * DN_CHUNK) for c in cs]
        q_c, k_c, v_c = [], [], []
        for c in cs:
            r0 = HALO + c * DN_CHUNK - (B_CONV - 1)
            acc = None
            for kk in range(B_CONV):
                term = z_ref[r0 + kk:r0 + kk + DN_CHUNK, COL_B:COL_B + 768] * wb[kk:kk + 1, :]
                acc = term if acc is None else acc + term
            qkv = _silu(acc)
            q, k = qkv[:, 0:256], qkv[:, 256:512]
            ss = head_sum(jnp.concatenate([q * q, k * k], axis=0))
            q_c.append(q * lax.rsqrt(ss[0:DN_CHUNK] + EPS) * (HEAD_DIM ** -0.5))
            k_c.append(k * lax.rsqrt(ss[DN_CHUNK:] + EPS))
            v_c.append(qkv[:, 512:768])
            yield

        tail = z_ref[HALO:HALO + ts, COL_TAIL:COL_TAIL + 128]
        beta = _sigmoid(tail)
        xg = tail + vec_ref[0:1, 640:768]
        softplus = jnp.maximum(xg, 0.0) + jnp.log1p(jnp.exp(-jnp.abs(xg)))
        gdec = -jnp.exp(vec_ref[0:1, 512:640]) * softplus
        beta_f = _split_lhs_dot(beta, e_beta, 1)
        gdec_f = _split_lhs_dot(gdec, e_alpha, 2)
        gc = _split_rhs_dot(l_bd, gdec_f, 2)
        yield

        gc_c = [gc[s] for s in sls]
        gl = [g[DN_CHUNK - 1:DN_CHUNK, :] for g in gc_c]
        egc = [jnp.exp(g) for g in gc_c]
        ekg = [jnp.exp(gl[c] - gc_c[c]) for c in cs]
        egl = [jnp.exp(g) for g in gl]
        kb = [k_c[c] * beta_f[sls[c]] for c in cs]
        vb = [v_c[c] * beta_f[sls[c]] for c in cs]
        grow = [jnp.sum(jnp.where(eye, g, 0.0), axis=0, keepdims=True) for g in gc_c]
        decay = [jnp.exp(jnp.minimum(gc_c[c] - grow[c], 0.0)) for c in cs]
        yield
        aq = [_dot_nt(jnp.concatenate([kb[c], q_c[c]], axis=0).astype(BF16), bd(k_c[c])) for c in cs]
        a_m = [jnp.where(strict, aq[c][0:DN_CHUNK] * decay[c], 0.0) for c in cs]
        qk = [jnp.where(tril, aq[c][DN_CHUNK:2 * DN_CHUNK] * decay[c], 0.0) for c in cs]
        yield

        d1 = [jnp.where(same16, a, 0.0) for a in a_m]
        pm = [eye_f - d for d in d1]
        d2 = [mm(d, d) for d in d1]
        yield
        r = [mm(jnp.concatenate([pm[c], d2[c]], axis=0), d2[c]) for c in cs]
        pm = [pm[c] + r[c][0:DN_CHUNK] for c in cs]
        d4 = [r[c][DN_CHUNK:] for c in cs]
        yield
        r = [mm(jnp.concatenate([pm[c], d4[c]], axis=0), d4[c]) for c in cs]
        pm = [pm[c] + r[c][0:DN_CHUNK] for c in cs]
        d8 = [r[c][DN_CHUNK:] for c in cs]
        yield
        t0 = [pm[c] + mm(pm[c], d8[c]) for c in cs]
        yield
        m1 = [jnp.where(same32 & jnp.logical_not(same16), a, 0.0) for a in a_m]
        r = [mm(t0[c], m1[c]) for c in cs]
        yield
        t1 = [t0[c] - mm(r[c], t0[c]) for c in cs]
        yield
        m2 = [jnp.where(same32, 0.0, a) for a in a_m]
        r = [mm(t1[c], m2[c]) for c in cs]
        yield
        t2 = [(t1[c] - mm(r[c], t1[c])).astype(BF16) for c in cs]
        yield
        u_c = [_dot(t2[c], bd(vb[c])) for c in cs]
        w_c = [_dot(t2[c], bd(kb[c] * egc[c])) for c in cs]
        yield

        eye_b = eye_f.astype(BF16)
        bdw = [bd(w_c[c]) for c in cs]
        bdu = [bd(u_c[c]) for c in cs]
        kgt = [_dot_nt(eye_b, bd(k_c[c] * ekg[c])).astype(BF16) for c in cs]
        yield
        ntc = [_dot(kgt[c], bdw[c]) for c in cs]
        btc = [_dot(kgt[c], bdu[c]) for c in cs]
        qkb = [q.astype(BF16) for q in qk]
        qt = [q_c[c] * egc[c] - _dot(qkb[c], bdw[c]) for c in cs]
        ou = [_dot(qkb[c], bdu[c]) for c in cs]
        yield

        st = st_ref[...]
        for c in cs:
            r = _dot(jnp.concatenate([ntc[c], qt[c]], axis=0).astype(BF16), bd(st))
            o_c = r[DN_CHUNK:] + ou[c]
            st = st * egl[c] - r[0:DN_CHUNK] + btc[c]
            if c == cs[-1]:
                st_ref[...] = st
            yield
            o_n = o_c * lax.rsqrt(head_mean(o_c * o_c) + EPS) * out_gain
            zrows = slice(HALO + c * DN_CHUNK, HALO + (c + 1) * DN_CHUNK)
            mix_t[sls[c], 256:512] = (o_n * gate_silu(tile, sls[c], 256, 512)).astype(BF16)

    def convs(z_ref, tile):
        mix_t, ybuf_t, ysh_t, yc_t = mix_ref.at[tile], ybuf.at[tile], ysh.at[tile], yc_ref.at[tile]
        for c in range(ts // C_CHUNK):
            sl = slice(c * C_CHUNK, (c + 1) * C_CHUNK)
            zrows = slice(HALO + c * C_CHUNK, HALO + (c + 1) * C_CHUNK)
            cu = z_ref[zrows, COL_C:COL_C + 256]
            cv = z_ref[zrows, COL_C + 256:COL_C + 512]
            mu = jnp.mean(cv, axis=-1, keepdims=True)
            dv = cv - mu
            var = jnp.mean(dv * dv, axis=-1, keepdims=True)
            cvn = dv * lax.rsqrt(var + EPS) * vec_ref[1:2, 256:512] + vec_ref[1:2, 512:768]
            yield
            vblk = jnp.concatenate([jnp.where(head_mask[g], cvn, 0.0).astype(BF16) for g in range(4)], axis=0)
            zc = _dot(cw, vblk) + cbias
            mix_t[sl, 512:768] = (cu * zc * gate_silu(tile, sl, 512, 768)).astype(BF16)
            yield

        for r in range(1, 8):
            ysh_t[r - 1] = ybuf_t[r:r + ts + HALO - 8, :]
            if r % 2 == 1:
                yield
        for rb in range(0, ts, CONV_ROWS):
            acc = None
            for kk in range(D_CONV):
                r0 = HALO - (D_CONV - 1) + kk
                a8, r = (r0 // 8) * 8 + rb, r0 % 8
                win = ybuf_t[a8:a8 + CONV_ROWS, :] if r == 0 else ysh_t[r - 1, a8:a8 + CONV_ROWS, :]
                term = win * wd[kk:kk + 1, :]
                acc = term if acc is None else acc + term
            yc_t[rb:rb + CONV_ROWS, :] = acc + vec_ref[1:2, 768:1024]
            yield
        for c in range(ts // C_CHUNK):
            sl = slice(c * C_CHUNK, (c + 1) * C_CHUNK)
            zrows = slice(HALO + c * C_CHUNK, HALO + (c + 1) * C_CHUNK)
            yc = yc_t[sl, :]
            mu = head_mean(yc)
            dy = yc - mu
            var = head_mean(dy * dy)
            yn = dy * lax.rsqrt(var + EPS) * vec_ref[2:3, 0:256] + vec_ref[2:3, 256:512]
            mix_t[sl, 768:1024] = (_silu(yn) * gate_silu(tile, sl, 768, 1024)).astype(BF16)
            yield

    def out_proj(tile):
        mix_t, x1_t, hg_t = mix_ref.at[tile], x1_ref.at[tile], hg_ref.at[tile]
        rows = slice(tile * ts, (tile + 1) * ts)
        for c0 in range(0, D_MODEL, PROJ_CHUNK):
            cols = slice(c0, c0 + PROJ_CHUNK)
            x1_t[:, cols] = x_ref[0, rows, cols] + _dot(mix_t[...], w_out_ref[:, cols])
            yield
        for r0 in range(0, ts, 64):
            x1 = x1_t[r0:r0 + 64, :]
            ms1 = jnp.mean(x1 * x1, axis=-1, keepdims=True)
            hg_t[r0:r0 + 64, :] = (x1 * lax.rsqrt(ms1 + EPS)).astype(BF16)
        yield
        pb = p_ref[0, rows, :].astype(BF16)
        for c0 in range(0, D_MODEL, PROJ_CHUNK):
            cols = slice(c0, c0 + PROJ_CHUNK)
            ple_gate = _sigmoid(_dot(hg_t[...], gate_w_ref[:, cols]))
            pe = _dot(pb, ple_w_ref[:, cols])
            o_ref[0, rows, cols] = x1_t[:, cols] + ple_gate * pe
            yield

    @pl.when(sp == 0)
    def _():
        za_ref[0:HALO, :] = jnp.zeros((HALO, Z_WIDTH), F32)
        kbuf[:, 0:WINDOW, :] = jnp.zeros((2, WINDOW, 128), BF16)
        vbuf[:, 0:WINDOW, :] = jnp.zeros((2, WINDOW, 128), BF16)
        st_ref[...] = jnp.zeros((HEAD_DIM, 256), F32)
        ybuf[0, 0:HALO, :] = jnp.zeros((HALO, GROUP_WIDTH), F32)

    @pl.when(step == 0)
    def _():
        _run_streams([(in_proj(lambda r0, n: x_ref[0, r0:r0 + n, :], za_ref, 0), 1)])

    zb_ref[0:HALO, :] = za_ref[ts:ts + HALO, :]
    ybuf[1, 0:HALO, :] = ybuf[0, ts:ts + HALO, :]
    _run_streams([(deltanet(za_ref, 0), 1), (convs(za_ref, 0), 1), (attention(za_ref, 0), 2),
                  (in_proj(lambda r0, n: x_ref[0, ts + r0:ts + r0 + n, :], zb_ref, 1), 2)])
    za_ref[0:HALO, :] = zb_ref[ts:ts + HALO, :]
    ybuf[0, 0:HALO, :] = ybuf[1, ts:ts + HALO, :]
    _run_streams([(deltanet(zb_ref, 1), 1), (convs(zb_ref, 1), 1), (attention(zb_ref, 1), 2),
                  (in_proj(lambda r0, n: xn_ref[0, r0:r0 + n, :], za_ref, 0), 1), (out_proj(0), 1)])
    _run_streams([(out_proj(1), 1)])


def _layer_spec(arr, layer):
    return pl.BlockSpec((None,) + arr.shape[1:], lambda g: (layer,) + (0,) * (arr.ndim - 1),
                        pipeline_mode=pl.Buffered(1))


def _layer_call(x, p, layer, cos_t, sin_t, sink, w_in, w_mid, w_tail, w_out, ple_w, gate_w, vec, bconv, cw, cbias,
                dconv, ts):
    bsz, seq, _ = x.shape
    tiles_per_seq = seq // ts
    steps_per_seq = tiles_per_seq // 2
    n_steps = bsz * steps_per_seq
    last_tile = bsz * tiles_per_seq - 1

    def pair_map(g):
        return (g // steps_per_seq, g % steps_per_seq, 0)

    def next_tile_map(g):
        t = jnp.minimum(2 * g + 2, last_tile)
        return (t // tiles_per_seq, t % tiles_per_seq, 0)

    in_specs = [
        pl.BlockSpec(memory_space=pltpu.SMEM),
        pl.BlockSpec((1, 2 * ts, D_MODEL), pair_map),
        pl.BlockSpec((1, ts, D_MODEL), next_tile_map),
        pl.BlockSpec((None, 1, 2 * ts, D_PLE), lambda g: (layer,) + pair_map(g)),
        pl.BlockSpec((2 * ts, 128), lambda g: (g % steps_per_seq, 0)),
        pl.BlockSpec((2 * ts, 128), lambda g: (g % steps_per_seq, 0)),
    ] + [_layer_spec(a, layer) for a in (w_in, w_mid, w_tail, w_out, ple_w, gate_w, vec, bconv, cw, cbias, dconv)]
    scratch = [
        pltpu.VMEM((HALO + ts, Z_WIDTH), F32),
        pltpu.VMEM((HALO + ts, Z_WIDTH), F32),
        pltpu.VMEM((ts, D_MODEL), BF16),
        pltpu.VMEM((2, WINDOW + ts, 128), BF16),
        pltpu.VMEM((2, WINDOW + ts, 128), BF16),
        pltpu.VMEM((HEAD_DIM, 256), F32),
        pltpu.VMEM((2, ts, D_MODEL), BF16),
        pltpu.VMEM((2, HALO + ts, GROUP_WIDTH), F32),
        pltpu.VMEM((2, 7, HALO + ts - 8, GROUP_WIDTH), F32),
        pltpu.VMEM((2, ts, GROUP_WIDTH), F32),
        pltpu.VMEM((2, ts, D_MODEL), F32),
        pltpu.VMEM((2, ts, D_MODEL), BF16),
        pltpu.VMEM((2, ts, D_MODEL), F32),
    ]
    return pl.pallas_call(
        functools.partial(_layer_kernel, ts=ts, steps_per_seq=steps_per_seq, layer=layer),
        grid=(n_steps,),
        in_specs=in_specs,
        out_specs=pl.BlockSpec((1, 2 * ts, D_MODEL), pair_map),
        out_shape=jax.ShapeDtypeStruct(x.shape, F32),
        scratch_shapes=scratch,
        compiler_params=pltpu.CompilerParams(
            dimension_semantics=("arbitrary",),
            vmem_limit_bytes=VMEM_LIMIT_BYTES,
        ),
        name="hybrid_trunk_layer",
    )(sink, x, x, p, cos_t, sin_t, w_in, w_mid, w_tail, w_out, ple_w, gate_w, vec, bconv, cw, cbias, dconv)


W_PREP_BLOCK = 256


def _w_in_prep_kernel(wt_ref, g_ref, head_ref, mid_ref, tail_ref):
    n = W_PREP_BLOCK
    eye = (lax.broadcasted_iota(jnp.int32, (n, n), 0) == lax.broadcasted_iota(jnp.int32, (n, n), 1)).astype(BF16)
    g = g_ref[...]
    d_in = g.shape[1]

    def transposed(r0, rows):
        wb = (wt_ref[r0:r0 + rows, :] * g).astype(BF16)
        return jnp.concatenate([_dot_nt(eye, wb[:, k:k + n]) for k in range(0, d_in, n)], axis=0).astype(BF16)

    for c in range(0, W_IN_SPLIT, n):
        head_ref[:, c:c + n] = transposed(c, n)
    mid0 = W_IN_SPLIT + 2 * N_HEADS
    for c in range(0, mid_ref.shape[1], n):
        mid_ref[:, c:c + n] = transposed(mid0 + c, n)
    lane = lax.broadcasted_iota(jnp.int32, (1, 128), 1)
    tail = transposed(W_IN_SPLIT, 128)
    tail_ref[...] = jnp.where(lane < 2 * N_HEADS, tail, jnp.zeros_like(tail))


def _w_in_prep(w_in_t, gain):
    depth, width, d_in = w_in_t.shape
    mid = width - W_IN_SPLIT - 2 * N_HEADS

    def spec(rows, cols):
        return pl.BlockSpec((None, rows, cols), lambda l: (l, 0, 0))

    return pl.pallas_call(
        _w_in_prep_kernel,
        grid=(depth,),
        in_specs=[spec(width, d_in), spec(1, d_in)],
        out_specs=[spec(d_in, W_IN_SPLIT), spec(d_in, mid), spec(d_in, 128)],
        out_shape=[jax.ShapeDtypeStruct((depth, d_in, c), BF16) for c in (W_IN_SPLIT, mid, 128)],
        compiler_params=pltpu.CompilerParams(vmem_limit_bytes=VMEM_LIMIT_BYTES),
        name="w_in_prep",
    )(w_in_t, gain[:, None, :])


def kernel(x, p, norm_gain, w_in, w_out, a_q_gain, a_k_gain, a_sink, b_conv, b_a_log, b_dt_bias, b_out_gain,
           c_ln_gain, c_ln_bias, c_w_s, c_b_s, d_conv, d_conv_bias, d_ln_gain, d_ln_bias,
           ple_w, ple_gate_norm, ple_gate_w):
    depth = w_in.shape[0]
    bsz, seq, _ = x.shape
    ts = min(SEQ_TILE, seq)
    assert seq % (2 * ts) == 0 and ts % WINDOW == 0

    inv = jnp.tile(ROPE_THETA ** (-jnp.arange(0, HEAD_DIM, 2, dtype=F32) / HEAD_DIM), 4)
    sign = jnp.tile(jnp.repeat(jnp.array([-1.0, 1.0], F32), HEAD_DIM // 2), 2)
    ang_b = (jnp.arange(seq // WINDOW, dtype=F32) * WINDOW)[:, None] * inv[None, :]
    ang_r = jnp.arange(WINDOW, dtype=F32)[:, None] * inv[None, :]
    cb, sb, cr, sr = jnp.cos(ang_b)[:, None, :], jnp.sin(ang_b)[:, None, :], jnp.cos(ang_r)[None], jnp.sin(ang_r)[None]
    cos_t = (cb * cr - sb * sr).reshape(seq, 128)
    sin_t = ((sb * cr + cb * sr) * sign).reshape(seq, 128)

    w_head, w_mid, w_tail = _w_in_prep(jnp.swapaxes(w_in, 1, 2), norm_gain)
    gate_w_b = (ple_gate_w * ple_gate_norm[:, :, None]).astype(BF16)
    w_out_b, ple_w_b = w_out.astype(BF16), ple_w.astype(BF16)

    def z(n):
        return jnp.zeros((depth, n), F32)

    def rep4(v, n):
        return jnp.broadcast_to(v[:, None, :], (depth, n, v.shape[1])).reshape(depth, n * v.shape[1])

    vec = jnp.concatenate([
        rep4(a_q_gain * (HEAD_DIM ** -0.5), 4), rep4(a_k_gain, 2), z(128),
        z(N_HEADS), b_a_log, z(128 - 2 * N_HEADS), z(N_HEADS), b_dt_bias, z(128 - 2 * N_HEADS), z(256),
        rep4(b_out_gain, 4), c_ln_gain, c_ln_bias, d_conv_bias,
        d_ln_gain, d_ln_bias, z(512), z(5 * D_MODEL)], axis=1).astype(F32).reshape(depth, 8, D_MODEL)
    cw = jnp.transpose(c_w_s, (0, 2, 1, 3)).reshape(depth, C_CHUNK, 4 * C_CHUNK)
    cbias = jnp.repeat(jnp.swapaxes(c_b_s, 1, 2), GROUP_WIDTH // 4, axis=2)
    sink = a_sink.astype(F32)

    for i in range(depth):
        x = _layer_call(x, p, i, cos_t, sin_t, sink, w_head, w_mid, w_tail, w_out_b, ple_w_b, gate_w_b, vec,
                        b_conv, cw, cbias, d_conv, ts)
    return x
```

```python
import functools

import jax
import jax.numpy as jnp
from jax import lax
from jax.experimental import pallas as pl
from jax.experimental.pallas import tpu as pltpu

F32 = jnp.float32
BF16 = jnp.bfloat16

EPS = 1e-6
D_MODEL = 1024
D_PLE = 256
GROUP_WIDTH = 256
HEAD_DIM = 64
N_HEADS = 4
WINDOW = 128
ROPE_THETA = 10000.0
B_CONV = 4
DN_CHUNK = 64
C_CHUNK = 128
D_CONV = 31

COL_AQ, COL_AK, COL_AV = 0, 256, 384
COL_B = 512
COL_C = 1280
COL_D = 1792
COL_GATE = 2304
COL_TAIL = 3328
Z_WIDTH = 3456
W_IN_SPLIT = 1280
HALO = 32

SEQ_TILE = 256
PROJ_CHUNK = 256
CONV_ROWS = 32
VMEM_LIMIT_BYTES = 56 * 1024 * 1024


def _sigmoid(x):
    return 0.5 + 0.5 * jnp.tanh(0.5 * x)


def _silu(x):
    h = 0.5 * x
    return h + h * jnp.tanh(h)


def _dot(a, b):
    return jnp.dot(a, b, preferred_element_type=F32)


def _dot_nt(a, b):
    return lax.dot_general(a, b, (((1,), (1,)), ((), ())), preferred_element_type=F32)


def _dot_tn(a, b):
    return lax.dot_general(a, b, (((0,), (0,)), ((), ())), preferred_element_type=F32)


def _split_lhs_dot(a, b_bf16, n):
    acc = None
    r = a
    for i in range(n):
        hi = r.astype(BF16)
        t = _dot(hi, b_bf16)
        acc = t if acc is None else acc + t
        if i + 1 < n:
            r = r - hi.astype(F32)
    return acc


def _split_rhs_dot(a_bf16, b, n):
    acc = None
    r = b
    for i in range(n):
        hi = r.astype(BF16)
        t = _dot(a_bf16, hi)
        acc = t if acc is None else acc + t
        if i + 1 < n:
            r = r - hi.astype(F32)
    return acc


def _run_streams(streams):
    live = list(streams)
    rnd = 0
    while live:
        for item in list(live):
            gen, period = item
            if rnd % period == 0 and next(gen, StopIteration) is StopIteration:
                live.remove(item)
        rnd += 1


def _layer_kernel(sink_ref, x_ref, xn_ref, p_ref, cos_ref, sin_ref, w1_ref, w2_ref, w3_ref, w_out_ref, ple_w_ref,
                  gate_w_ref, vec_ref, bconv_ref, cw_ref, cbias_ref, dconv_ref, o_ref,
                  za_ref, zb_ref, h_ref, kbuf, vbuf, st_ref, mix_ref, ybuf, ysh, yc_ref, x1_ref, hg_ref, sg_ref,
                  *, ts, steps_per_seq, layer):
    step = pl.program_id(0)
    sp = step % steps_per_seq

    lane256 = lax.broadcasted_iota(jnp.int32, (1, 256), 1)
    head_mask = [(lane256 // HEAD_DIM) == h for h in range(N_HEADS)]
    r256 = lax.broadcasted_iota(jnp.int32, (256, 256), 0)
    c256 = lax.broadcasted_iota(jnp.int32, (256, 256), 1)
    same_head = (r256 // HEAD_DIM) == (c256 // HEAD_DIM)
    g_sum = jnp.where(same_head, 1.0, 0.0).astype(BF16)
    g_mean = jnp.where(same_head, 1.0 / HEAD_DIM, 0.0).astype(BF16)
    lane128 = lax.broadcasted_iota(jnp.int32, (1, 128), 1)
    first_half = (lane128 % HEAD_DIM) < (HEAD_DIM // 2)
    lo128 = lane128 < HEAD_DIM
    q_gain = vec_ref[0:1, 0:256]
    k_gain = vec_ref[0:1, 256:384]
    qi = lax.broadcasted_iota(jnp.int32, (WINDOW, 2 * WINDOW), 0)
    kj = lax.broadcasted_iota(jnp.int32, (WINDOW, 2 * WINDOW), 1)
    rel = qi + WINDOW - kj
    band = (rel >= 0) & (rel < WINDOW)
    wb = bconv_ref[...]
    er = lax.broadcasted_iota(jnp.int32, (128, 256), 0)
    ec = lax.broadcasted_iota(jnp.int32, (128, 256), 1) // HEAD_DIM
    e_beta = jnp.where(er == ec, 1.0, 0.0).astype(BF16)
    e_alpha = jnp.where(er == ec + N_HEADS, 1.0, 0.0).astype(BF16)
    tr = lax.broadcasted_iota(jnp.int32, (ts, ts), 0)
    tc = lax.broadcasted_iota(jnp.int32, (ts, ts), 1)
    l_bd = jnp.where(((tr // DN_CHUNK) == (tc // DN_CHUNK)) & (tr >= tc), 1.0, 0.0).astype(BF16)
    ci = lax.broadcasted_iota(jnp.int32, (DN_CHUNK, 256), 0)
    cj = lax.broadcasted_iota(jnp.int32, (DN_CHUNK, 256), 1) % HEAD_DIM
    eye = ci == cj
    tril = ci >= cj
    strict = ci > cj
    same16 = (ci // 16) == (cj // 16)
    same32 = (ci // 32) == (cj // 32)
    eye_f = jnp.where(eye, 1.0, 0.0)
    out_gain = vec_ref[1:2, 0:256]
    wi = lax.broadcasted_iota(jnp.int32, (C_CHUNK, 4 * C_CHUNK), 0)
    wj = lax.broadcasted_iota(jnp.int32, (C_CHUNK, 4 * C_CHUNK), 1) % C_CHUNK
    cw = jnp.where(wi >= wj, cw_ref[...], 0.0).astype(BF16)
    cbias = cbias_ref[...]
    wd = dconv_ref[...]

    def head_mean(v, n=1):
        return _split_lhs_dot(v, g_mean, n)

    def head_sum(v, n=1):
        return _split_lhs_dot(v, g_sum, n)

    def bd(xv):
        xb = xv.astype(BF16)
        zero = jnp.zeros_like(xb)
        return jnp.concatenate([jnp.where(head_mask[h], xb, zero) for h in range(N_HEADS)], axis=0)

    def mm(lhs, xv):
        return _dot(lhs.astype(BF16), bd(xv))

    def rope(v, cos, sin):
        sw = jnp.where(first_half, pltpu.roll(v, 128 - HEAD_DIM // 2, 1), pltpu.roll(v, HEAD_DIM // 2, 1))
        return v * cos + sw * sin

    proj_chunks = ([(w1_ref, c, c, PROJ_CHUNK) for c in range(0, W_IN_SPLIT, PROJ_CHUNK)]
                   + [(w2_ref, c, W_IN_SPLIT + c, PROJ_CHUNK) for c in range(0, COL_TAIL - W_IN_SPLIT, PROJ_CHUNK)]
                   + [(w3_ref, 0, COL_TAIL, Z_WIDTH - COL_TAIL)])

    def in_proj(x_rows, z_ref, slot):
        for r0 in range(0, ts, 64):
            xv = x_rows(r0, 64)
            ms = jnp.mean(xv * xv, axis=-1, keepdims=True)
            h_ref[r0:r0 + 64, :] = (xv * lax.rsqrt(ms + EPS)).astype(BF16)
        for w_ref, wc, zc, width in proj_chunks:
            zv = _dot(h_ref[...], w_ref[:, wc:wc + width])
            if COL_C <= zc < COL_D:
                zv = 0.5 * zv * (1.0 + jnp.tanh(0.7978845608028654 * (zv + 0.044715 * (zv * zv * zv))))
            if zc == COL_D + 256:
                ybuf[slot, HALO:HALO + ts, :] = z_ref[HALO:HALO + ts, COL_D:COL_D + 256] * _sigmoid(zv)
            elif zc >= COL_GATE and zc < COL_TAIL:
                sg_ref[slot, :, zc - COL_GATE:zc - COL_GATE + width] = _silu(zv)
            else:
                z_ref[HALO:HALO + ts, zc:zc + width] = zv
            yield

    def gate_silu(tile, rows, c0, c1):
        return sg_ref[tile, rows, c0:c1]

    def attention(z_ref, tile):
        mix_t = mix_ref.at[tile]
        for blk in range(ts // WINDOW):
            rows = slice(HALO + blk * WINDOW, HALO + (blk + 1) * WINDOW)
            q = z_ref[rows, COL_AQ:COL_AQ + 256]
            k = z_ref[rows, COL_AK:COL_AK + 128]
            v = z_ref[rows, COL_AV:COL_AV + 128]
            cos = cos_ref[tile * ts + blk * WINDOW:tile * ts + (blk + 1) * WINDOW, :]
            sin = sin_ref[tile * ts + blk * WINDOW:tile * ts + (blk + 1) * WINDOW, :]
            qn = q * lax.rsqrt(head_mean(q * q) + EPS) * q_gain
            kn = k * lax.rsqrt(_split_lhs_dot(k * k, g_mean[0:128, 0:128], 1) + EPS) * k_gain
            qa = rope(qn[:, 0:128], cos, sin)
            qb = rope(qn[:, 128:256], cos, sin)
            kr = rope(kn, cos, sin)
            cur = slice(WINDOW + blk * WINDOW, 2 * WINDOW + blk * WINDOW)
            both = slice(blk * WINDOW, (blk + 2) * WINDOW)
            kbuf[0, cur, :] = kr.astype(BF16)
            kbuf[1, cur, :] = pltpu.roll(kr, HEAD_DIM, 1).astype(BF16)
            vbuf[0, cur, :] = v.astype(BF16)
            vbuf[1, cur, :] = pltpu.roll(v, HEAD_DIM, 1).astype(BF16)
            lhs = [jnp.where(lo128, qa, 0.0).astype(BF16), jnp.where(lo128, 0.0, qa).astype(BF16),
                   jnp.where(lo128, qb, 0.0).astype(BF16), jnp.where(lo128, 0.0, qb).astype(BF16)]
            yield
            valid = band
            if blk == 0 and tile == 0:
                valid = valid & (kj >= jnp.where(sp > 0, 0, WINDOW))
            slots = [0 if r in (0, 3) else 1 for r in range(N_HEADS)]
            scs = [_dot_nt(lhs[r], kbuf[slots[r], both, :]) for r in range(N_HEADS)]
            yield
            es, dens = [], []
            for r in range(N_HEADS):
                sc = jnp.where(valid, scs[r], -jnp.inf)
                sink = sink_ref[layer, r]
                m = jnp.maximum(jnp.max(sc, axis=-1, keepdims=True), sink)
                e = jnp.exp(sc - m)
                dens.append(jnp.sum(e, axis=-1, keepdims=True) + jnp.exp(sink - m))
                es.append(e.astype(BF16))
                if r % 2 == 1:
                    yield
            outs = [_dot(es[r], vbuf[slots[r], both, :]) / dens[r] for r in range(N_HEADS)]
            yield
            o_a0 = jnp.where(lo128, outs[0], outs[1])
            o_a1 = jnp.where(lo128, outs[2], outs[3])
            mrows = slice(blk * WINDOW, (blk + 1) * WINDOW)
            mix_t[mrows, 0:128] = (o_a0 * gate_silu(tile, mrows, 0, 128)).astype(BF16)
            mix_t[mrows, 128:256] = (o_a1 * gate_silu(tile, mrows, 128, 256)).astype(BF16)
            yield
        kbuf[:, 0:WINDOW, :] = kbuf[:, ts:ts + WINDOW, :]
        vbuf[:, 0:WINDOW, :] = vbuf[:, ts:ts + WINDOW, :]

    def deltanet(z_ref, tile):
        mix_t = mix_ref.at[tile]
        cs = range(ts // DN_CHUNK)
        sls = [slice(c * DN_CHUNK, (c + 1) * DN_CHUNK) for c in cs]
        q_c, k_c, v_c = [], [], []
        for c in cs:
            r0 = HALO + c * DN_CHUNK - (B_CONV - 1)
            acc = None
            for kk in range(B_CONV):
                term = z_ref[r0 + kk:r0 + kk + DN_CHUNK, COL_B:COL_B + 768] * wb[kk:kk + 1, :]
                acc = term if acc is None else acc + term
            qkv = _silu(acc)
            q, k = qkv[:, 0:256], qkv[:, 256:512]
            ss = head_sum(jnp.concatenate([q * q, k * k], axis=0))
            q_c.append(q * lax.rsqrt(ss[0:DN_CHUNK] + EPS) * (HEAD_DIM ** -0.5))
            k_c.append(k * lax.rsqrt(ss[DN_CHUNK:] + EPS))
            v_c.append(qkv[:, 512:768])
            yield

        tail = z_ref[HALO:HALO + ts, COL_TAIL:COL_TAIL + 128]
        beta = _sigmoid(tail)
        xg = tail + vec_ref[0:1, 640:768]
        softplus = jnp.maximum(xg, 0.0) + jnp.log1p(jnp.exp(-jnp.abs(xg)))
        gdec = -jnp.exp(vec_ref[0:1, 512:640]) * softplus
        beta_f = _split_lhs_dot(beta, e_beta, 1)
        gdec_f = _split_lhs_dot(gdec, e_alpha, 2)
        gc = _split_rhs_dot(l_bd, gdec_f, 2)
        yield

        gc_c = [gc[s] for s in sls]
        gl = [g[DN_CHUNK - 1:DN_CHUNK, :] for g in gc_c]
        egc = [jnp.exp(g) for g in gc_c]
        ekg = [jnp.exp(gl[c] - gc_c[c]) for c in cs]
        egl = [jnp.exp(g) for g in gl]
        kb = [k_c[c] * beta_f[sls[c]] for c in cs]
        vb = [v_c[c] * beta_f[sls[c]] for c in cs]
        grow = [jnp.sum(jnp.where(eye, g, 0.0), axis=0, keepdims=True) for g in gc_c]
        decay = [jnp.exp(jnp.minimum(gc_c[c] - grow[c], 0.0)) for c in cs]
        yield
        aq = [_dot_nt(jnp.concatenate([kb[c], q_c[c]], axis=0).astype(BF16), bd(k_c[c])) for c in cs]
        a_m = [jnp.where(strict, aq[c][0:DN_CHUNK] * decay[c], 0.0) for c in cs]
        qk = [jnp.where(tril, aq[c][DN_CHUNK:2 * DN_CHUNK] * decay[c], 0.0) for c in cs]
        yield

        d1 = [jnp.where(same16, a, 0.0) for a in a_m]
        pm = [eye_f - d for d in d1]
        d2 = [mm(d, d) for d in d1]
        yield
        r = [mm(jnp.concatenate([pm[c], d2[c]], axis=0), d2[c]) for c in cs]
        pm = [pm[c] + r[c][0:DN_CHUNK] for c in cs]
        d4 = [r[c][DN_CHUNK:] for c in cs]
        yield
        r = [mm(jnp.concatenate([pm[c], d4[c]], axis=0), d4[c]) for c in cs]
        pm = [pm[c] + r[c][0:DN_CHUNK] for c in cs]
        d8 = [r[c][DN_CHUNK:] for c in cs]
        yield
        t0 = [pm[c] + mm(pm[c], d8[c]) for c in cs]
        yield
        m1 = [jnp.where(same32 & jnp.logical_not(same16), a, 0.0) for a in a_m]
        r = [mm(t0[c], m1[c]) for c in cs]
        yield
        t1 = [t0[c] - mm(r[c], t0[c]) for c in cs]
        yield
        m2 = [jnp.where(same32, 0.0, a) for a in a_m]
        r = [mm(t1[c], m2[c]) for c in cs]
        yield
        t2 = [(t1[c] - mm(r[c], t1[c])).astype(BF16) for c in cs]
        yield
        u_c = [_dot(t2[c], bd(vb[c])) for c in cs]
        w_c = [_dot(t2[c], bd(kb[c] * egc[c])) for c in cs]
        yield

        eye_b = eye_f.astype(BF16)
        bdw = [bd(w_c[c]) for c in cs]
        bdu = [bd(u_c[c]) for c in cs]
        kgt = [_dot_nt(eye_b, bd(k_c[c] * ekg[c])).astype(BF16) for c in cs]
        yield
        ntc = [_dot(kgt[c], bdw[c]) for c in cs]
        btc = [_dot(kgt[c], bdu[c]) for c in cs]
        qkb = [q.astype(BF16) for q in qk]
        qt = [q_c[c] * egc[c] - _dot(qkb[c], bdw[c]) for c in cs]
        ou = [_dot(qkb[c], bdu[c]) for c in cs]
        yield

        st = st_ref[...]
        for c in cs:
            r = _dot(jnp.concatenate([ntc[c], qt[c]], axis=0).astype(BF16), bd(st))
            o_c = r[DN_CHUNK:] + ou[c]
            st = st * egl[c] - r[0:DN_CHUNK] + btc[c]
            if c == cs[-1]:
                st_ref[...] = st
            yield
            o_n = o_c * lax.rsqrt(head_mean(o_c * o_c) + EPS) * out_gain
            zrows = slice(HALO + c * DN_CHUNK, HALO + (c + 1) * DN_CHUNK)
            mix_t[sls[c], 256:512] = (o_n * gate_silu(tile, sls[c], 256, 512)).astype(BF16)

    def convs(z_ref, tile):
        mix_t, ybuf_t, ysh_t, yc_t = mix_ref.at[tile], ybuf.at[tile], ysh.at[tile], yc_ref.at[tile]
        for c in range(ts // C_CHUNK):
            sl = slice(c * C_CHUNK, (c + 1) * C_CHUNK)
            zrows = slice(HALO + c * C_CHUNK, HALO + (c + 1) * C_CHUNK)
            cu = z_ref[zrows, COL_C:COL_C + 256]
            cv = z_ref[zrows, COL_C + 256:COL_C + 512]
            mu = jnp.mean(cv, axis=-1, keepdims=True)
            dv = cv - mu
            var = jnp.mean(dv * dv, axis=-1, keepdims=True)
            cvn = dv * lax.rsqrt(var + EPS) * vec_ref[1:2, 256:512] + vec_ref[1:2, 512:768]
            yield
            vblk = jnp.concatenate([jnp.where(head_mask[g], cvn, 0.0).astype(BF16) for g in range(4)], axis=0)
            zc = _dot(cw, vblk) + cbias
            mix_t[sl, 512:768] = (cu * zc * gate_silu(tile, sl, 512, 768)).astype(BF16)
            yield

        for r in range(1, 8):
            ysh_t[r - 1] = ybuf_t[r:r + ts + HALO - 8, :]
            if r % 2 == 1:
                yield
        for rb in range(0, ts, CONV_ROWS):
            acc = None
            for kk in range(D_CONV):
                r0 = HALO - (D_CONV - 1) + kk
                a8, r = (r0 // 8) * 8 + rb, r0 % 8
                win = ybuf_t[a8:a8 + CONV_ROWS, :] if r == 0 else ysh_t[r - 1, a8:a8 + CONV_ROWS, :]
                term = win * wd[kk:kk + 1, :]
                acc = term if acc is None else acc + term
            yc_t[rb:rb + CONV_ROWS, :] = acc + vec_ref[1:2, 768:1024]
            yield
        for c in range(ts // C_CHUNK):
            sl = slice(c * C_CHUNK, (c + 1) * C_CHUNK)
            zrows = slice(HALO + c * C_CHUNK, HALO + (c + 1) * C_CHUNK)
            yc = yc_t[sl, :]
            mu = head_mean(yc)
            dy = yc - mu
            var = head_mean(dy * dy)
            yn = dy * lax.rsqrt(var + EPS) * vec_ref[2:3, 0:256] + vec_ref[2:3, 256:512]
            mix_t[sl, 768:1024] = (_silu(yn) * gate_silu(tile, sl, 768, 1024)).astype(BF16)
            yield

    def out_proj(tile):
        mix_t, x1_t, hg_t = mix_ref.at[tile], x1_ref.at[tile], hg_ref.at[tile]
        rows = slice(tile * ts, (tile + 1) * ts)
        for c0 in range(0, D_MODEL, PROJ_CHUNK):
            cols = slice(c0, c0 + PROJ_CHUNK)
            x1_t[:, cols] = x_ref[0, rows, cols] + _dot(mix_t[...], w_out_ref[:, cols])
            yield
        for r0 in range(0, ts, 64):
            x1 = x1_t[r0:r0 + 64, :]
            ms1 = jnp.mean(x1 * x1, axis=-1, keepdims=True)
            hg_t[r0:r0 + 64, :] = (x1 * lax.rsqrt(ms1 + EPS)).astype(BF16)
        yield
        pb = p_ref[0, rows, :].astype(BF16)
        for c0 in range(0, D_MODEL, PROJ_CHUNK):
            cols = slice(c0, c0 + PROJ_CHUNK)
            ple_gate = _sigmoid(_dot(hg_t[...], gate_w_ref[:, cols]))
            pe = _dot(pb, ple_w_ref[:, cols])
            o_ref[0, rows, cols] = x1_t[:, cols] + ple_gate * pe
            yield

    @pl.when(sp == 0)
    def _():
        za_ref[0:HALO, :] = jnp.zeros((HALO, Z_WIDTH), F32)
        kbuf[:, 0:WINDOW, :] = jnp.zeros((2, WINDOW, 128), BF16)
        vbuf[:, 0:WINDOW, :] = jnp.zeros((2, WINDOW, 128), BF16)
        st_ref[...] = jnp.zeros((HEAD_DIM, 256), F32)
        ybuf[0, 0:HALO, :] = jnp.zeros((HALO, GROUP_WIDTH), F32)

    @pl.when(step == 0)
    def _():
        _run_streams([(in_proj(lambda r0, n: x_ref[0, r0:r0 + n, :], za_ref, 0), 1)])

    zb_ref[0:HALO, :] = za_ref[ts:ts + HALO, :]
    ybuf[1, 0:HALO, :] = ybuf[0, ts:ts + HALO, :]
    _run_streams([(deltanet(za_ref, 0), 1), (convs(za_ref, 0), 1), (attention(za_ref, 0), 2),
                  (in_proj(lambda r0, n: x_ref[0, ts + r0:ts + r0 + n, :], zb_ref, 1), 2)])
    za_ref[0:HALO, :] = zb_ref[ts:ts + HALO, :]
    ybuf[0, 0:HALO, :] = ybuf[1, ts:ts + HALO, :]
    _run_streams([(deltanet(zb_ref, 1), 1), (convs(zb_ref, 1), 1), (attention(zb_ref, 1), 2),
                  (in_proj(lambda r0, n: xn_ref[0, r0:r0 + n, :], za_ref, 0), 1), (out_proj(0), 1)])
    _run_streams([(out_proj(1), 1)])


def _layer_spec(arr, layer):
    return pl.BlockSpec((None,) + arr.shape[1:], lambda g: (layer,) + (0,) * (arr.ndim - 1),
                        pipeline_mode=pl.Buffered(1))


def _layer_call(x, p, layer, cos_t, sin_t, sink, w_in, w_mid, w_tail, w_out, ple_w, gate_w, vec, bconv, cw, cbias,
                dconv, ts):
    bsz, seq, _ = x.shape
    tiles_per_seq = seq // ts
    steps_per_seq = tiles_per_seq // 2
    n_steps = bsz * steps_per_seq
    last_tile = bsz * tiles_per_seq - 1

    def pair_map(g):
        return (g // steps_per_seq, g % steps_per_seq, 0)

    def next_tile_map(g):
        t = jnp.minimum(2 * g + 2, last_tile)
        return (t // tiles_per_seq, t % tiles_per_seq, 0)

    in_specs = [
        pl.BlockSpec(memory_space=pltpu.SMEM),
        pl.BlockSpec((1, 2 * ts, D_MODEL), pair_map),
        pl.BlockSpec((1, ts, D_MODEL), next_tile_map),
        pl.BlockSpec((None, 1, 2 * ts, D_PLE), lambda g: (layer,) + pair_map(g)),
        pl.BlockSpec((2 * ts, 128), lambda g: (g % steps_per_seq, 0)),
        pl.BlockSpec((2 * ts, 128), lambda g: (g % steps_per_seq, 0)),
    ] + [_layer_spec(a, layer) for a in (w_in, w_mid, w_tail, w_out, ple_w, gate_w, vec, bconv, cw, cbias, dconv)]
    scratch = [
        pltpu.VMEM((HALO + ts, Z_WIDTH), F32),
        pltpu.VMEM((HALO + ts, Z_WIDTH), F32),
        pltpu.VMEM((ts, D_MODEL), BF16),
        pltpu.VMEM((2, WINDOW + ts, 128), BF16),
        pltpu.VMEM((2, WINDOW + ts, 128), BF16),
        pltpu.VMEM((HEAD_DIM, 256), F32),
        pltpu.VMEM((2, ts, D_MODEL), BF16),
        pltpu.VMEM((2, HALO + ts, GROUP_WIDTH), F32),
        pltpu.VMEM((2, 7, HALO + ts - 8, GROUP_WIDTH), F32),
        pltpu.VMEM((2, ts, GROUP_WIDTH), F32),
        pltpu.VMEM((2, ts, D_MODEL), F32),
        pltpu.VMEM((2, ts, D_MODEL), BF16),
        pltpu.VMEM((2, ts, D_MODEL), F32),
    ]
    return pl.pallas_call(
        functools.partial(_layer_kernel, ts=ts, steps_per_seq=steps_per_seq, layer=layer),
        grid=(n_steps,),
        in_specs=in_specs,
        out_specs=pl.BlockSpec((1, 2 * ts, D_MODEL), pair_map),
        out_shape=jax.ShapeDtypeStruct(x.shape, F32),
        scratch_shapes=scratch,
        compiler_params=pltpu.CompilerParams(
            dimension_semantics=("arbitrary",),
            vmem_limit_bytes=VMEM_LIMIT_BYTES,
        ),
        name="hybrid_trunk_layer",
    )(sink, x, x, p, cos_t, sin_t, w_in, w_mid, w_tail, w_out, ple_w, gate_w, vec, bconv, cw, cbias, dconv)


W_PREP_BLOCK = 256


def _w_in_prep_kernel(wt_ref, g_ref, head_ref, mid_ref, tail_ref):
    n = W_PREP_BLOCK
    eye = (lax.broadcasted_iota(jnp.int32, (n, n), 0) == lax.broadcasted_iota(jnp.int32, (n, n), 1)).astype(BF16)
    g = g_ref[...]
    d_in = g.shape[1]

    def transposed(r0, rows):
        wb = (wt_ref[r0:r0 + rows, :] * g).astype(BF16)
        return jnp.concatenate([_dot_nt(eye, wb[:, k:k + n]) for k in range(0, d_in, n)], axis=0).astype(BF16)

    for c in range(0, W_IN_SPLIT, n):
        head_ref[:, c:c + n] = transposed(c, n)
    mid0 = W_IN_SPLIT + 2 * N_HEADS
    for c in range(0, mid_ref.shape[1], n):
        mid_ref[:, c:c + n] = transposed(mid0 + c, n)
    lane = lax.broadcasted_iota(jnp.int32, (1, 128), 1)
    tail = transposed(W_IN_SPLIT, 128)
    tail_ref[...] = jnp.where(lane < 2 * N_HEADS, tail, jnp.zeros_like(tail))


def _w_in_prep(w_in_t, gain):
    depth, width, d_in = w_in_t.shape
    mid = width - W_IN_SPLIT - 2 * N_HEADS

    def spec(rows, cols):
        return pl.BlockSpec((None, rows, cols), lambda l: (l, 0, 0))

    return pl.pallas_call(
        _w_in_prep_kernel,
        grid=(depth,),
        in_specs=[spec(width, d_in), spec(1, d_in)],
        out_specs=[spec(d_in, W_IN_SPLIT), spec(d_in, mid), spec(d_in, 128)],
        out_shape=[jax.ShapeDtypeStruct((depth, d_in, c), BF16) for c in (W_IN_SPLIT, mid, 128)],
        compiler_params=pltpu.CompilerParams(vmem_limit_bytes=VMEM_LIMIT_BYTES),
        name="w_in_prep",
    )(w_in_t, gain[:, None, :])


def kernel(x, p, norm_gain, w_in, w_out, a_q_gain, a_k_gain, a_sink, b_conv, b_a_log, b_dt_bias, b_out_gain,
           c_ln_gain, c_ln_bias, c_w_s, c_b_s, d_conv, d_conv_bias, d_ln_gain, d_ln_bias,
           ple_w, ple_gate_norm, ple_gate_w):
    depth = w_in.shape[0]
    bsz, seq, _ = x.shape
    ts = min(SEQ_TILE, seq)
    assert seq % (2 * ts) == 0 and ts % WINDOW == 0

    inv = jnp.tile(ROPE_THETA ** (-jnp.arange(0, HEAD_DIM, 2, dtype=F32) / HEAD_DIM), 4)
    sign = jnp.tile(jnp.repeat(jnp.array([-1.0, 1.0], F32), HEAD_DIM // 2), 2)
    ang_b = (jnp.arange(seq // WINDOW, dtype=F32) * WINDOW)[:, None] * inv[None, :]
    ang_r = jnp.arange(WINDOW, dtype=F32)[:, None] * inv[None, :]
    cb, sb, cr, sr = jnp.cos(ang_b)[:, None, :], jnp.sin(ang_b)[:, None, :], jnp.cos(ang_r)[None], jnp.sin(ang_r)[None]
    cos_t = (cb * cr - sb * sr).reshape(seq, 128)
    sin_t = ((sb * cr + cb * sr) * sign).reshape(seq, 128)

    w_head, w_mid, w_tail = _w_in_prep(jnp.swapaxes(w_in, 1, 2), norm_gain)
    gate_w_b = (ple_gate_w * ple_gate_norm[:, :, None]).astype(BF16)
    w_out_b, ple_w_b = w_out.astype(BF16), ple_w.astype(BF16)

    def z(n):
        return jnp.zeros((depth, n), F32)

    def rep4(v, n):
        return jnp.broadcast_to(v[:, None, :], (depth, n, v.shape[1])).reshape(depth, n * v.shape[1])

    vec = jnp.concatenate([
        rep4(a_q_gain * (HEAD_DIM ** -0.5), 4), rep4(a_k_gain, 2), z(128),
        z(N_HEADS), b_a_log, z(128 - 2 * N_HEADS), z(N_HEADS), b_dt_bias, z(128 - 2 * N_HEADS), z(256),
        rep4(b_out_gain, 4), c_ln_gain, c_ln_bias, d_conv_bias,
        d_ln_gain, d_ln_bias, z(512), z(5 * D_MODEL)], axis=1).astype(F32).reshape(depth, 8, D_MODEL)
    cw = jnp.transpose(c_w_s, (0, 2, 1, 3)).reshape(depth, C_CHUNK, 4 * C_CHUNK)
    cbias = jnp.repeat(jnp.swapaxes(c_b_s, 1, 2), GROUP_WIDTH // 4, axis=2)
    sink = a_sink.astype(F32)

    for i in range(depth):
        x = _layer_call(x, p, i, cos_t, sin_t, sink, w_head, w_mid, w_tail, w_out_b, ple_w_b, gate_w_b, vec,
                        b_conv, cw, cbias, d_conv, ts)
    return x
```

```python
import functools

import jax
import jax.numpy as jnp
from jax import lax
from jax.experimental import pallas as pl
from jax.experimental.pallas import tpu as pltpu

F32 = jnp.float32
BF16 = jnp.bfloat16

EPS = 1e-6
D_MODEL = 1024
D_PLE = 256
GROUP_WIDTH = 256
HEAD_DIM = 64
N_HEADS = 4
WINDOW = 128
ROPE_THETA = 10000.0
B_CONV = 4
DN_CHUNK = 64
C_CHUNK = 128
D_CONV = 31

COL_AQ, COL_AK, COL_AV = 0, 256, 384
COL_B = 512
COL_C = 1280
COL_D = 1792
COL_GATE = 2304
COL_TAIL = 3328
Z_WIDTH = 3456
W_IN_SPLIT = 1280
HALO = 32

SEQ_TILE = 256
PROJ_CHUNK = 256
CONV_ROWS = 32
VMEM_LIMIT_BYTES = 56 * 1024 * 1024


def _sigmoid(x):
    return 0.5 + 0.5 * jnp.tanh(0.5 * x)


def _silu(x):
    h = 0.5 * x
    return h + h * jnp.tanh(h)


def _dot(a, b):
    return jnp.dot(a, b, preferred_element_type=F32)


def _dot_nt(a, b):
    return lax.dot_general(a, b, (((1,), (1,)), ((), ())), preferred_element_type=F32)


def _dot_tn(a, b):
    return lax.dot_general(a, b, (((0,), (0,)), ((), ())), preferred_element_type=F32)


def _split_lhs_dot(a, b_bf16, n):
    acc = None
    r = a
    for i in range(n):
        hi = r.astype(BF16)
        t = _dot(hi, b_bf16)
        acc = t if acc is None else acc + t
        if i + 1 < n:
            r = r - hi.astype(F32)
    return acc


def _split_rhs_dot(a_bf16, b, n):
    acc = None
    r = b
    for i in range(n):
        hi = r.astype(BF16)
        t = _dot(a_bf16, hi)
        acc = t if acc is None else acc + t
        if i + 1 < n:
            r = r - hi.astype(F32)
    return acc


def _run_streams(streams):
    live = list(streams)
    rnd = 0
    while live:
        for item in list(live):
            gen, period = item
            if rnd % period == 0 and next(gen, StopIteration) is StopIteration:
                live.remove(item)
        rnd += 1


def _layer_kernel(sink_ref, x_ref, xn_ref, p_ref, cos_ref, sin_ref, w1_ref, w2_ref, w3_ref, w_out_ref, ple_w_ref,
                  gate_w_ref, vec_ref, bconv_ref, cw_ref, cbias_ref, dconv_ref, o_ref,
                  za_ref, zb_ref, h_ref, kbuf, vbuf, st_ref, mix_ref, ybuf, ysh, yc_ref, x1_ref, hg_ref, sg_ref,
                  *, ts, steps_per_seq, layer):
    step = pl.program_id(0)
    sp = step % steps_per_seq

    lane256 = lax.broadcasted_iota(jnp.int32, (1, 256), 1)
    head_mask = [(lane256 // HEAD_DIM) == h for h in range(N_HEADS)]
    r256 = lax.broadcasted_iota(jnp.int32, (256, 256), 0)
    c256 = lax.broadcasted_iota(jnp.int32, (256, 256), 1)
    same_head = (r256 // HEAD_DIM) == (c256 // HEAD_DIM)
    g_sum = jnp.where(same_head, 1.0, 0.0).astype(BF16)
    g_mean = jnp.where(same_head, 1.0 / HEAD_DIM, 0.0).astype(BF16)
    lane128 = lax.broadcasted_iota(jnp.int32, (1, 128), 1)
    first_half = (lane128 % HEAD_DIM) < (HEAD_DIM // 2)
    lo128 = lane128 < HEAD_DIM
    q_gain = vec_ref[0:1, 0:256]
    k_gain = vec_ref[0:1, 256:384]
    qi = lax.broadcasted_iota(jnp.int32, (WINDOW, 2 * WINDOW), 0)
    kj = lax.broadcasted_iota(jnp.int32, (WINDOW, 2 * WINDOW), 1)
    rel = qi + WINDOW - kj
    band = (rel >= 0) & (rel < WINDOW)
    wb = bconv_ref[...]
    er = lax.broadcasted_iota(jnp.int32, (128, 256), 0)
    ec = lax.broadcasted_iota(jnp.int32, (128, 256), 1) // HEAD_DIM
    e_beta = jnp.where(er == ec, 1.0, 0.0).astype(BF16)
    e_alpha = jnp.where(er == ec + N_HEADS, 1.0, 0.0).astype(BF16)
    tr = lax.broadcasted_iota(jnp.int32, (ts, ts), 0)
    tc = lax.broadcasted_iota(jnp.int32, (ts, ts), 1)
    l_bd = jnp.where(((tr // DN_CHUNK) == (tc // DN_CHUNK)) & (tr >= tc), 1.0, 0.0).astype(BF16)
    ci = lax.broadcasted_iota(jnp.int32, (DN_CHUNK, 256), 0)
    cj = lax.broadcasted_iota(jnp.int32, (DN_CHUNK, 256), 1) % HEAD_DIM
    eye = ci == cj
    tril = ci >= cj
    strict = ci > cj
    same16 = (ci // 16) == (cj // 16)
    same32 = (ci // 32) == (cj // 32)
    eye_f = jnp.where(eye, 1.0, 0.0)
    out_gain = vec_ref[1:2, 0:256]
    wi = lax.broadcasted_iota(jnp.int32, (C_CHUNK, 4 * C_CHUNK), 0)
    wj = lax.broadcasted_iota(jnp.int32, (C_CHUNK, 4 * C_CHUNK), 1) % C_CHUNK
    cw = jnp.where(wi >= wj, cw_ref[...], 0.0).astype(BF16)
    cbias = cbias_ref[...]
    wd = dconv_ref[...]

    def head_mean(v, n=1):
        return _split_lhs_dot(v, g_mean, n)

    def head_sum(v, n=1):
        return _split_lhs_dot(v, g_sum, n)

    def bd(xv):
        xb = xv.astype(BF16)
        zero = jnp.zeros_like(xb)
        return jnp.concatenate([jnp.where(head_mask[h], xb, zero) for h in range(N_HEADS)], axis=0)

    def mm(lhs, xv):
        return _dot(lhs.astype(BF16), bd(xv))

    def rope(v, cos, sin):
        sw = jnp.where(first_half, pltpu.roll(v, 128 - HEAD_DIM // 2, 1), pltpu.roll(v, HEAD_DIM // 2, 1))
        return v * cos + sw * sin

    proj_chunks = ([(w1_ref, c, c, PROJ_CHUNK) for c in range(0, W_IN_SPLIT, PROJ_CHUNK)]
                   + [(w2_ref, c, W_IN_SPLIT + c, PROJ_CHUNK) for c in range(0, COL_TAIL - W_IN_SPLIT, PROJ_CHUNK)]
                   + [(w3_ref, 0, COL_TAIL, Z_WIDTH - COL_TAIL)])

    def in_proj(x_rows, z_ref, slot):
        for r0 in range(0, ts, 64):
            xv = x_rows(r0, 64)
            ms = jnp.mean(xv * xv, axis=-1, keepdims=True)
            h_ref[r0:r0 + 64, :] = (xv * lax.rsqrt(ms + EPS)).astype(BF16)
        for w_ref, wc, zc, width in proj_chunks:
            zv = _dot(h_ref[...], w_ref[:, wc:wc + width])
            if COL_C <= zc < COL_D:
                zv = 0.5 * zv * (1.0 + jnp.tanh(0.7978845608028654 * (zv + 0.044715 * (zv * zv * zv))))
            if zc == COL_D + 256:
                ybuf[slot, HALO:HALO + ts, :] = z_ref[HALO:HALO + ts, COL_D:COL_D + 256] * _sigmoid(zv)
            elif zc >= COL_GATE and zc < COL_TAIL:
                sg_ref[slot, :, zc - COL_GATE:zc - COL_GATE + width] = _silu(zv)
            else:
                z_ref[HALO:HALO + ts, zc:zc + width] = zv
            yield

    def gate_silu(tile, rows, c0, c1):
        return sg_ref[tile, rows, c0:c1]

    def attention(z_ref, tile):
        mix_t = mix_ref.at[tile]
        nblk = ts // WINDOW
        slots = [0 if r in (0, 3) else 1 for r in range(N_HEADS)]
        lhs_all, both_all = [], []
        for blk in range(nblk):
            rows = slice(HALO + blk * WINDOW, HALO + (blk + 1) * WINDOW)
            q = z_ref[rows, COL_AQ:COL_AQ + 256]
            k = z_ref[rows, COL_AK:COL_AK + 128]
            v = z_ref[rows, COL_AV:COL_AV + 128]
            cos = cos_ref[tile * ts + blk * WINDOW:tile * ts + (blk + 1) * WINDOW, :]
            sin = sin_ref[tile * ts + blk * WINDOW:tile * ts + (blk + 1) * WINDOW, :]
            qn = q * lax.rsqrt(head_mean(q * q) + EPS) * q_gain
            kn = k * lax.rsqrt(_split_lhs_dot(k * k, g_mean[0:128, 0:128], 1) + EPS) * k_gain
            qa = rope(qn[:, 0:128], cos, sin)
            qb = rope(qn[:, 128:256], cos, sin)
            kr = rope(kn, cos, sin)
            cur = slice(WINDOW + blk * WINDOW, 2 * WINDOW + blk * WINDOW)
            both_all.append(slice(blk * WINDOW, (blk + 2) * WINDOW))
            kbuf[0, cur, :] = kr.astype(BF16)
            kbuf[1, cur, :] = pltpu.roll(kr, HEAD_DIM, 1).astype(BF16)
            vbuf[0, cur, :] = v.astype(BF16)
            vbuf[1, cur, :] = pltpu.roll(v, HEAD_DIM, 1).astype(BF16)
            lhs_all.append([jnp.where(lo128, qa, 0.0).astype(BF16), jnp.where(lo128, 0.0, qa).astype(BF16),
                            jnp.where(lo128, qb, 0.0).astype(BF16), jnp.where(lo128, 0.0, qb).astype(BF16)])
            yield
        scs = [[_dot_nt(lhs_all[blk][r], kbuf[slots[r], both_all[blk], :]) for r in range(N_HEADS)] for blk in range(nblk)]
        yield
        es = [[None] * N_HEADS for _ in range(nblk)]
        dens = [[None] * N_HEADS for _ in range(nblk)]
        for blk in range(nblk):
            valid = band
            if blk == 0 and tile == 0:
                valid = valid & (kj >= jnp.where(sp > 0, 0, WINDOW))
            for r in range(N_HEADS):
                sc = jnp.where(valid, scs[blk][r], -jnp.inf)
                sink = sink_ref[layer, r]
                m = jnp.maximum(jnp.max(sc, axis=-1, keepdims=True), sink)
                e = jnp.exp(sc - m)
                dens[blk][r] = jnp.sum(e, axis=-1, keepdims=True) + jnp.exp(sink - m)
                es[blk][r] = e.astype(BF16)
                if r % 2 == 1:
                    yield
        outs = [[_dot(es[blk][r], vbuf[slots[r], both_all[blk], :]) / dens[blk][r] for r in range(N_HEADS)]
                for blk in range(nblk)]
        yield
        for blk in range(nblk):
            o_a0 = jnp.where(lo128, outs[blk][0], outs[blk][1])
            o_a1 = jnp.where(lo128, outs[blk][2], outs[blk][3])
            mrows = slice(blk * WINDOW, (blk + 1) * WINDOW)
            mix_t[mrows, 0:128] = (o_a0 * gate_silu(tile, mrows, 0, 128)).astype(BF16)
            mix_t[mrows, 128:256] = (o_a1 * gate_silu(tile, mrows, 128, 256)).astype(BF16)
            yield
        kbuf[:, 0:WINDOW, :] = kbuf[:, ts:ts + WINDOW, :]
        vbuf[:, 0:WINDOW, :] = vbuf[:, ts:ts + WINDOW, :]

    def deltanet(z_ref, tile):
        mix_t = mix_ref.at[tile]
        cs = range(ts // DN_CHUNK)
        sls = [slice(c * DN_CHUNK, (c + 1) * DN_CHUNK) for c in cs]
        q_c, k_c, v_c = [], [], []
        for c in cs:
            r0 = HALO + c * DN_CHUNK - (B_CONV - 1)
            acc = None
            for kk in range(B_CONV):
                term = z_ref[r0 + kk:r0 + kk + DN_CHUNK, COL_B:COL_B + 768] * wb[kk:kk + 1, :]
                acc = term if acc is None else acc + term
            qkv = _silu(acc)
            q, k = qkv[:, 0:256], qkv[:, 256:512]
            ss = head_sum(jnp.concatenate([q * q, k * k], axis=0))
            q_c.append(q * lax.rsqrt(ss[0:DN_CHUNK] + EPS) * (HEAD_DIM ** -0.5))
            k_c.append(k * lax.rsqrt(ss[DN_CHUNK:] + EPS))
            v_c.append(qkv[:, 512:768])
            yield

        tail = z_ref[HALO:HALO + ts, COL_TAIL:COL_TAIL + 128]
        beta = _sigmoid(tail)
        xg = tail + vec_ref[0:1, 640:768]
        softplus = jnp.maximum(xg, 0.0) + jnp.log1p(jnp.exp(-jnp.abs(xg)))
        gdec = -jnp.exp(vec_ref[0:1, 512:640]) * softplus
        beta_f = _split_lhs_dot(beta, e_beta, 1)
        gdec_f = _split_lhs_dot(gdec, e_alpha, 2)
        gc = _split_rhs_dot(l_bd, gdec_f, 2)
        yield

        gc_c = [gc[s] for s in sls]
        gl = [g[DN_CHUNK - 1:DN_CHUNK, :] for g in gc_c]
        egc = [jnp.exp(g) for g in gc_c]
        ekg = [jnp.exp(gl[c] - gc_c[c]) for c in cs]
        egl = [jnp.exp(g) for g in gl]
        kb = [k_c[c] * beta_f[sls[c]] for c in cs]
        vb = [v_c[c] * beta_f[sls[c]] for c in cs]
        grow = [jnp.sum(jnp.where(eye, g, 0.0), axis=0, keepdims=True) for g in gc_c]
        decay = [jnp.exp(jnp.minimum(gc_c[c] - grow[c], 0.0)) for c in cs]
        yield
        aq = [_dot_nt(jnp.concatenate([kb[c], q_c[c]], axis=0).astype(BF16), bd(k_c[c])) for c in cs]
        a_m = [jnp.where(strict, aq[c][0:DN_CHUNK] * decay[c], 0.0) for c in cs]
        qk = [jnp.where(tril, aq[c][DN_CHUNK:2 * DN_CHUNK] * decay[c], 0.0) for c in cs]
        yield

        d1 = [jnp.where(same16, a, 0.0) for a in a_m]
        pm = [eye_f - d for d in d1]
        d2 = [mm(d, d) for d in d1]
        yield
        r = [mm(jnp.concatenate([pm[c], d2[c]], axis=0), d2[c]) for c in cs]
        pm = [pm[c] + r[c][0:DN_CHUNK] for c in cs]
        d4 = [r[c][DN_CHUNK:] for c in cs]
        yield
        r = [mm(jnp.concatenate([pm[c], d4[c]], axis=0), d4[c]) for c in cs]
        pm = [pm[c] + r[c][0:DN_CHUNK] for c in cs]
        d8 = [r[c][DN_CHUNK:] for c in cs]
        yield
        t0 = [pm[c] + mm(pm[c], d8[c]) for c in cs]
        yield
        m1 = [jnp.where(same32 & jnp.logical_not(same16), a, 0.0) for a in a_m]
        r = [mm(t0[c], m1[c]) for c in cs]
        yield
        t1 = [t0[c] - mm(r[c], t0[c]) for c in cs]
        yield
        m2 = [jnp.where(same32, 0.0, a) for a in a_m]
        r = [mm(t1[c], m2[c]) for c in cs]
        yield
        t2 = [(t1[c] - mm(r[c], t1[c])).astype(BF16) for c in cs]
        yield
        u_c = [_dot(t2[c], bd(vb[c])) for c in cs]
        w_c = [_dot(t2[c], bd(kb[c] * egc[c])) for c in cs]
        yield

        eye_b = eye_f.astype(BF16)
        bdw = [bd(w_c[c]) for c in cs]
        bdu = [bd(u_c[c]) for c in cs]
        kgt = [_dot_nt(eye_b, bd(k_c[c] * ekg[c])).astype(BF16) for c in cs]
        yield
        ntc = [_dot(kgt[c], bdw[c]) for c in cs]
        btc = [_dot(kgt[c], bdu[c]) for c in cs]
        qkb = [q.astype(BF16) for q in qk]
        qt = [q_c[c] * egc[c] - _dot(qkb[c], bdw[c]) for c in cs]
        ou = [_dot(qkb[c], bdu[c]) for c in cs]
        yield

        st = st_ref[...]
        for c in cs:
            r = _dot(jnp.concatenate([ntc[c], qt[c]], axis=0).astype(BF16), bd(st))
            o_c = r[DN_CHUNK:] + ou[c]
            st = st * egl[c] - r[0:DN_CHUNK] + btc[c]
            if c == cs[-1]:
                st_ref[...] = st
            yield
            o_n = o_c * lax.rsqrt(head_mean(o_c * o_c) + EPS) * out_gain
            zrows = slice(HALO + c * DN_CHUNK, HALO + (c + 1) * DN_CHUNK)
            mix_t[sls[c], 256:512] = (o_n * gate_silu(tile, sls[c], 256, 512)).astype(BF16)

    def convs(z_ref, tile):
        mix_t, ybuf_t, ysh_t, yc_t = mix_ref.at[tile], ybuf.at[tile], ysh.at[tile], yc_ref.at[tile]
        for c in range(ts // C_CHUNK):
            sl = slice(c * C_CHUNK, (c + 1) * C_CHUNK)
            zrows = slice(HALO + c * C_CHUNK, HALO + (c + 1) * C_CHUNK)
            cu = z_ref[zrows, COL_C:COL_C + 256]
            cv = z_ref[zrows, COL_C + 256:COL_C + 512]
            mu = jnp.mean(cv, axis=-1, keepdims=True)
            dv = cv - mu
            var = jnp.mean(dv * dv, axis=-1, keepdims=True)
            cvn = dv * lax.rsqrt(var + EPS) * vec_ref[1:2, 256:512] + vec_ref[1:2, 512:768]
            yield
            vblk = jnp.concatenate([jnp.where(head_mask[g], cvn, 0.0).astype(BF16) for g in range(4)], axis=0)
            zc = _dot(cw, vblk) + cbias
            mix_t[sl, 512:768] = (cu * zc * gate_silu(tile, sl, 512, 768)).astype(BF16)
            yield

        for r in range(1, 8):
            ysh_t[r - 1] = ybuf_t[r:r + ts + HALO - 8, :]
            if r % 2 == 1:
                yield
        for rb in range(0, ts, CONV_ROWS):
            acc = None
            for kk in range(D_CONV):
                r0 = HALO - (D_CONV - 1) + kk
                a8, r = (r0 // 8) * 8 + rb, r0 % 8
                win = ybuf_t[a8:a8 + CONV_ROWS, :] if r == 0 else ysh_t[r - 1, a8:a8 + CONV_ROWS, :]
                term = win * wd[kk:kk + 1, :]
                acc = term if acc is None else acc + term
            yc_t[rb:rb + CONV_ROWS, :] = acc + vec_ref[1:2, 768:1024]
            yield
        for c in range(ts // C_CHUNK):
            sl = slice(c * C_CHUNK, (c + 1) * C_CHUNK)
            zrows = slice(HALO + c * C_CHUNK, HALO + (c + 1) * C_CHUNK)
            yc = yc_t[sl, :]
            mu = head_mean(yc)
            dy = yc - mu
            var = head_mean(dy * dy)
            yn = dy * lax.rsqrt(var + EPS) * vec_ref[2:3, 0:256] + vec_ref[2:3, 256:512]
            mix_t[sl, 768:1024] = (_silu(yn) * gate_silu(tile, sl, 768, 1024)).astype(BF16)
            yield

    def out_proj(tile):
        mix_t, x1_t, hg_t = mix_ref.at[tile], x1_ref.at[tile], hg_ref.at[tile]
        rows = slice(tile * ts, (tile + 1) * ts)
        for c0 in range(0, D_MODEL, PROJ_CHUNK):
            cols = slice(c0, c0 + PROJ_CHUNK)
            x1_t[:, cols] = x_ref[0, rows, cols] + _dot(mix_t[...], w_out_ref[:, cols])
            yield
        for r0 in range(0, ts, 64):
            x1 = x1_t[r0:r0 + 64, :]
            ms1 = jnp.mean(x1 * x1, axis=-1, keepdims=True)
            hg_t[r0:r0 + 64, :] = (x1 * lax.rsqrt(ms1 + EPS)).astype(BF16)
        yield
        pb = p_ref[0, rows, :].astype(BF16)
        for c0 in range(0, D_MODEL, PROJ_CHUNK):
            cols = slice(c0, c0 + PROJ_CHUNK)
            ple_gate = _sigmoid(_dot(hg_t[...], gate_w_ref[:, cols]))
            pe = _dot(pb, ple_w_ref[:, cols])
            o_ref[0, rows, cols] = x1_t[:, cols] + ple_gate * pe
            yield

    @pl.when(sp == 0)
    def _():
        za_ref[0:HALO, :] = jnp.zeros((HALO, Z_WIDTH), F32)
        kbuf[:, 0:WINDOW, :] = jnp.zeros((2, WINDOW, 128), BF16)
        vbuf[:, 0:WINDOW, :] = jnp.zeros((2, WINDOW, 128), BF16)
        st_ref[...] = jnp.zeros((HEAD_DIM, 256), F32)
        ybuf[0, 0:HALO, :] = jnp.zeros((HALO, GROUP_WIDTH), F32)

    @pl.when(step == 0)
    def _():
        _run_streams([(in_proj(lambda r0, n: x_ref[0, r0:r0 + n, :], za_ref, 0), 1)])

    zb_ref[0:HALO, :] = za_ref[ts:ts + HALO, :]
    ybuf[1, 0:HALO, :] = ybuf[0, ts:ts + HALO, :]
    _run_streams([(deltanet(za_ref, 0), 1), (convs(za_ref, 0), 1), (attention(za_ref, 0), 2),
                  (in_proj(lambda r0, n: x_ref[0, ts + r0:ts + r0 + n, :], zb_ref, 1), 2)])
    za_ref[0:HALO, :] = zb_ref[ts:ts + HALO, :]
    ybuf[0, 0:HALO, :] = ybuf[1, ts:ts + HALO, :]
    _run_streams([(deltanet(zb_ref, 1), 1), (convs(zb_ref, 1), 1), (attention(zb_ref, 1), 2),
                  (in_proj(lambda r0, n: xn_ref[0, r0:r0 + n, :], za_ref, 0), 1), (out_proj(0), 1)])
    _run_streams([(out_proj(1), 1)])


def _layer_spec(arr, layer):
    return pl.BlockSpec((None,) + arr.shape[1:], lambda g: (layer,) + (0,) * (arr.ndim - 1),
                        pipeline_mode=pl.Buffered(1))


def _layer_call(x, p, layer, cos_t, sin_t, sink, w_in, w_mid, w_tail, w_out, ple_w, gate_w, vec, bconv, cw, cbias,
                dconv, ts):
    bsz, seq, _ = x.shape
    tiles_per_seq = seq // ts
    steps_per_seq = tiles_per_seq // 2
    n_steps = bsz * steps_per_seq
    last_tile = bsz * tiles_per_seq - 1

    def pair_map(g):
        return (g // steps_per_seq, g % steps_per_seq, 0)

    def next_tile_map(g):
        t = jnp.minimum(2 * g + 2, last_tile)
        return (t // tiles_per_seq, t % tiles_per_seq, 0)

    in_specs = [
        pl.BlockSpec(memory_space=pltpu.SMEM),
        pl.BlockSpec((1, 2 * ts, D_MODEL), pair_map),
        pl.BlockSpec((1, ts, D_MODEL), next_tile_map),
        pl.BlockSpec((None, 1, 2 * ts, D_PLE), lambda g: (layer,) + pair_map(g)),
        pl.BlockSpec((2 * ts, 128), lambda g: (g % steps_per_seq, 0)),
        pl.BlockSpec((2 * ts, 128), lambda g: (g % steps_per_seq, 0)),
    ] + [_layer_spec(a, layer) for a in (w_in, w_mid, w_tail, w_out, ple_w, gate_w, vec, bconv, cw, cbias, dconv)]
    scratch = [
        pltpu.VMEM((HALO + ts, Z_WIDTH), F32),
        pltpu.VMEM((HALO + ts, Z_WIDTH), F32),
        pltpu.VMEM((ts, D_MODEL), BF16),
        pltpu.VMEM((2, WINDOW + ts, 128), BF16),
        pltpu.VMEM((2, WINDOW + ts, 128), BF16),
        pltpu.VMEM((HEAD_DIM, 256), F32),
        pltpu.VMEM((2, ts, D_MODEL), BF16),
        pltpu.VMEM((2, HALO + ts, GROUP_WIDTH), F32),
        pltpu.VMEM((2, 7, HALO + ts - 8, GROUP_WIDTH), F32),
        pltpu.VMEM((2, ts, GROUP_WIDTH), F32),
        pltpu.VMEM((2, ts, D_MODEL), F32),
        pltpu.VMEM((2, ts, D_MODEL), BF16),
        pltpu.VMEM((2, ts, D_MODEL), F32),
    ]
    return pl.pallas_call(
        functools.partial(_layer_kernel, ts=ts, steps_per_seq=steps_per_seq, layer=layer),
        grid=(n_steps,),
        in_specs=in_specs,
        out_specs=pl.BlockSpec((1, 2 * ts, D_MODEL), pair_map),
        out_shape=jax.ShapeDtypeStruct(x.shape, F32),
        scratch_shapes=scratch,
        compiler_params=pltpu.CompilerParams(
            dimension_semantics=("arbitrary",),
            vmem_limit_bytes=VMEM_LIMIT_BYTES,
        ),
        name="hybrid_trunk_layer",
    )(sink, x, x, p, cos_t, sin_t, w_in, w_mid, w_tail, w_out, ple_w, gate_w, vec, bconv, cw, cbias, dconv)


W_PREP_BLOCK = 256


def _w_in_prep_kernel(wt_ref, g_ref, head_ref, mid_ref, tail_ref):
    n = W_PREP_BLOCK
    eye = (lax.broadcasted_iota(jnp.int32, (n, n), 0) == lax.broadcasted_iota(jnp.int32, (n, n), 1)).astype(BF16)
    g = g_ref[...]
    d_in = g.shape[1]

    def transposed(r0, rows):
        wb = (wt_ref[r0:r0 + rows, :] * g).astype(BF16)
        return jnp.concatenate([_dot_nt(eye, wb[:, k:k + n]) for k in range(0, d_in, n)], axis=0).astype(BF16)

    for c in range(0, W_IN_SPLIT, n):
        head_ref[:, c:c + n] = transposed(c, n)
    mid0 = W_IN_SPLIT + 2 * N_HEADS
    for c in range(0, mid_ref.shape[1], n):
        mid_ref[:, c:c + n] = transposed(mid0 + c, n)
    lane = lax.broadcasted_iota(jnp.int32, (1, 128), 1)
    tail = transposed(W_IN_SPLIT, 128)
    tail_ref[...] = jnp.where(lane < 2 * N_HEADS, tail, jnp.zeros_like(tail))


def _w_in_prep(w_in_t, gain):
    depth, width, d_in = w_in_t.shape
    mid = width - W_IN_SPLIT - 2 * N_HEADS

    def spec(rows, cols):
        return pl.BlockSpec((None, rows, cols), lambda l: (l, 0, 0))

    return pl.pallas_call(
        _w_in_prep_kernel,
        grid=(depth,),
        in_specs=[spec(width, d_in), spec(1, d_in)],
        out_specs=[spec(d_in, W_IN_SPLIT), spec(d_in, mid), spec(d_in, 128)],
        out_shape=[jax.ShapeDtypeStruct((depth, d_in, c), BF16) for c in (W_IN_SPLIT, mid, 128)],
        compiler_params=pltpu.CompilerParams(vmem_limit_bytes=VMEM_LIMIT_BYTES),
        name="w_in_prep",
    )(w_in_t, gain[:, None, :])


def kernel(x, p, norm_gain, w_in, w_out, a_q_gain, a_k_gain, a_sink, b_conv, b_a_log, b_dt_bias, b_out_gain,
           c_ln_gain, c_ln_bias, c_w_s, c_b_s, d_conv, d_conv_bias, d_ln_gain, d_ln_bias,
           ple_w, ple_gate_norm, ple_gate_w):
    depth = w_in.shape[0]
    bsz, seq, _ = x.shape
    ts = min(SEQ_TILE, seq)
    assert seq % (2 * ts) == 0 and ts % WINDOW == 0

    inv = jnp.tile(ROPE_THETA ** (-jnp.arange(0, HEAD_DIM, 2, dtype=F32) / HEAD_DIM), 4)
    sign = jnp.tile(jnp.repeat(jnp.array([-1.0, 1.0], F32), HEAD_DIM // 2), 2)
    ang_b = (jnp.arange(seq // WINDOW, dtype=F32) * WINDOW)[:, None] * inv[None, :]
    ang_r = jnp.arange(WINDOW, dtype=F32)[:, None] * inv[None, :]
    cb, sb, cr, sr = jnp.cos(ang_b)[:, None, :], jnp.sin(ang_b)[:, None, :], jnp.cos(ang_r)[None], jnp.sin(ang_r)[None]
    cos_t = (cb * cr - sb * sr).reshape(seq, 128)
    sin_t = ((sb * cr + cb * sr) * sign).reshape(seq, 128)

    w_head, w_mid, w_tail = _w_in_prep(jnp.swapaxes(w_in, 1, 2), norm_gain)
    gate_w_b = (ple_gate_w * ple_gate_norm[:, :, None]).astype(BF16)
    w_out_b, ple_w_b = w_out.astype(BF16), ple_w.astype(BF16)

    def z(n):
        return jnp.zeros((depth, n), F32)

    def rep4(v, n):
        return jnp.broadcast_to(v[:, None, :], (depth, n, v.shape[1])).reshape(depth, n * v.shape[1])

    vec = jnp.concatenate([
        rep4(a_q_gain * (HEAD_DIM ** -0.5), 4), rep4(a_k_gain, 2), z(128),
        z(N_HEADS), b_a_log, z(128 - 2 * N_HEADS), z(N_HEADS), b_dt_bias, z(128 - 2 * N_HEADS), z(256),
        rep4(b_out_gain, 4), c_ln_gain, c_ln_bias, d_conv_bias,
        d_ln_gain, d_ln_bias, z(512), z(5 * D_MODEL)], axis=1).astype(F32).reshape(depth, 8, D_MODEL)
    cw = jnp.transpose(c_w_s, (0, 2, 1, 3)).reshape(depth, C_CHUNK, 4 * C_CHUNK)
    cbias = jnp.repeat(jnp.swapaxes(c_b_s, 1, 2), GROUP_WIDTH // 4, axis=2)
    sink = a_sink.astype(F32)

    for i in range(depth):
        x = _layer_call(x, p, i, cos_t, sin_t, sink, w_head, w_mid, w_tail, w_out_b, ple_w_b, gate_w_b, vec,
                        b_conv, cw, cbias, d_conv, ts)
    return x
```

```python
import functools

import jax
import jax.numpy as jnp
from jax import lax
from jax.experimental import pallas as pl
from jax.experimental.pallas import tpu as pltpu

F32 = jnp.float32
BF16 = jnp.bfloat16

EPS = 1e-6
D_MODEL = 1024
D_PLE = 256
GROUP_WIDTH = 256
HEAD_DIM = 64
N_HEADS = 4
WINDOW = 128
ROPE_THETA = 10000.0
B_CONV = 4
DN_CHUNK = 64
C_CHUNK = 128
D_CONV = 31

COL_AQ, COL_AK, COL_AV = 0, 256, 384
COL_B = 512
COL_C = 1280
COL_D = 1792
COL_GATE = 2304
COL_TAIL = 3328
Z_WIDTH = 3456
W_IN_SPLIT = 1280
HALO = 32

SEQ_TILE = 256
PROJ_CHUNK = 256
CONV_ROWS = 32
VMEM_LIMIT_BYTES = 56 * 1024 * 1024


def _sigmoid(x):
    return 0.5 + 0.5 * jnp.tanh(0.5 * x)


def _silu(x):
    h = 0.5 * x
    return h + h * jnp.tanh(h)


def _dot(a, b):
    return jnp.dot(a, b, preferred_element_type=F32)


def _dot_nt(a, b):
    return lax.dot_general(a, b, (((1,), (1,)), ((), ())), preferred_element_type=F32)


def _dot_tn(a, b):
    return lax.dot_general(a, b, (((0,), (0,)), ((), ())), preferred_element_type=F32)


def _split_lhs_dot(a, b_bf16, n):
    acc = None
    r = a
    for i in range(n):
        hi = r.astype(BF16)
        t = _dot(hi, b_bf16)
        acc = t if acc is None else acc + t
        if i + 1 < n:
            r = r - hi.astype(F32)
    return acc


def _split_rhs_dot(a_bf16, b, n):
    acc = None
    r = b
    for i in range(n):
        hi = r.astype(BF16)
        t = _dot(a_bf16, hi)
        acc = t if acc is None else acc + t
        if i + 1 < n:
            r = r - hi.astype(F32)
    return acc


def _run_streams(streams):
    live = list(streams)
    rnd = 0
    while live:
        for item in list(live):
            gen, period = item
            if rnd % period == 0 and next(gen, StopIteration) is StopIteration:
                live.remove(item)
        rnd += 1


def _layer_kernel(sink_ref, x_ref, xn_ref, p_ref, cos_ref, sin_ref, w1_ref, w2_ref, w3_ref, w_out_ref, ple_w_ref,
                  gate_w_ref, vec_ref, bconv_ref, cw_ref, cbias_ref, dconv_ref, o_ref,
                  za_ref, zb_ref, h_ref, kbuf, vbuf, st_ref, mix_ref, ybuf, ysh, yc_ref, x1_ref, hg_ref, sg_ref,
                  *, ts, steps_per_seq, layer):
    step = pl.program_id(0)
    sp = step % steps_per_seq

    lane256 = lax.broadcasted_iota(jnp.int32, (1, 256), 1)
    head_mask = [(lane256 // HEAD_DIM) == h for h in range(N_HEADS)]
    r256 = lax.broadcasted_iota(jnp.int32, (256, 256), 0)
    c256 = lax.broadcasted_iota(jnp.int32, (256, 256), 1)
    same_head = (r256 // HEAD_DIM) == (c256 // HEAD_DIM)
    g_sum = jnp.where(same_head, 1.0, 0.0).astype(BF16)
    g_mean = jnp.where(same_head, 1.0 / HEAD_DIM, 0.0).astype(BF16)
    lane128 = lax.broadcasted_iota(jnp.int32, (1, 128), 1)
    first_half = (lane128 % HEAD_DIM) < (HEAD_DIM // 2)
    lo128 = lane128 < HEAD_DIM
    q_gain = vec_ref[0:1, 0:256]
    k_gain = vec_ref[0:1, 256:384]
    qi = lax.broadcasted_iota(jnp.int32, (WINDOW, 2 * WINDOW), 0)
    kj = lax.broadcasted_iota(jnp.int32, (WINDOW, 2 * WINDOW), 1)
    rel = qi + WINDOW - kj
    band = (rel >= 0) & (rel < WINDOW)
    wb = bconv_ref[...]
    er = lax.broadcasted_iota(jnp.int32, (128, 256), 0)
    ec = lax.broadcasted_iota(jnp.int32, (128, 256), 1) // HEAD_DIM
    e_beta = jnp.where(er == ec, 1.0, 0.0).astype(BF16)
    e_alpha = jnp.where(er == ec + N_HEADS, 1.0, 0.0).astype(BF16)
    tr = lax.broadcasted_iota(jnp.int32, (ts, ts), 0)
    tc = lax.broadcasted_iota(jnp.int32, (ts, ts), 1)
    l_bd = jnp.where(((tr // DN_CHUNK) == (tc // DN_CHUNK)) & (tr >= tc), 1.0, 0.0).astype(BF16)
    ci = lax.broadcasted_iota(jnp.int32, (DN_CHUNK, 256), 0)
    cj = lax.broadcasted_iota(jnp.int32, (DN_CHUNK, 256), 1) % HEAD_DIM
    eye = ci == cj
    tril = ci >= cj
    strict = ci > cj
    same16 = (ci // 16) == (cj // 16)
    same32 = (ci // 32) == (cj // 32)
    eye_f = jnp.where(eye, 1.0, 0.0)
    out_gain = vec_ref[1:2, 0:256]
    wi = lax.broadcasted_iota(jnp.int32, (C_CHUNK, 4 * C_CHUNK), 0)
    wj = lax.broadcasted_iota(jnp.int32, (C_CHUNK, 4 * C_CHUNK), 1) % C_CHUNK
    cw = jnp.where(wi >= wj, cw_ref[...], 0.0).astype(BF16)
    cbias = cbias_ref[...]
    wd = dconv_ref[...]

    never = step < 0
    tokens = []

    def tie(v):
        if not tokens:
            return v
        tok = tokens.pop(0)
        head = jnp.where(never, tok, v[0:8, 0:128])
        top = jnp.concatenate([head, v[0:8, 128:]], axis=1)
        return jnp.concatenate([top, v[8:]], axis=0)

    def head_mean(v, n=1):
        return _split_lhs_dot(v, g_mean, n)

    def head_sum(v, n=1):
        return _split_lhs_dot(v, g_sum, n)

    def bd(xv):
        xb = xv.astype(BF16)
        zero = jnp.zeros_like(xb)
        return jnp.concatenate([jnp.where(head_mask[h], xb, zero) for h in range(N_HEADS)], axis=0)

    def mm(lhs, xv):
        return _dot(lhs.astype(BF16), bd(xv))

    def rope(v, cos, sin):
        sw = jnp.where(first_half, pltpu.roll(v, 128 - HEAD_DIM // 2, 1), pltpu.roll(v, HEAD_DIM // 2, 1))
        return v * cos + sw * sin

    proj_chunks = ([(w1_ref, c, c, PROJ_CHUNK) for c in range(0, W_IN_SPLIT, PROJ_CHUNK)]
                   + [(w2_ref, c, W_IN_SPLIT + c, PROJ_CHUNK) for c in range(0, COL_TAIL - W_IN_SPLIT, PROJ_CHUNK)]
                   + [(w3_ref, 0, COL_TAIL, Z_WIDTH - COL_TAIL)])

    def in_proj(x_rows, z_ref, slot):
        for r0 in range(0, ts, 64):
            xv = x_rows(r0, 64)
            ms = jnp.mean(xv * xv, axis=-1, keepdims=True)
            h_ref[r0:r0 + 64, :] = (xv * lax.rsqrt(ms + EPS)).astype(BF16)
        for w_ref, wc, zc, width in proj_chunks:
            zv = _dot(h_ref[...], w_ref[:, wc:wc + width])
            tokens.append(zv[0:8, 0:128])
            if COL_C <= zc < COL_D:
                zv = 0.5 * zv * (1.0 + jnp.tanh(0.7978845608028654 * (zv + 0.044715 * (zv * zv * zv))))
            if zc == COL_D + 256:
                ybuf[slot, HALO:HALO + ts, :] = z_ref[HALO:HALO + ts, COL_D:COL_D + 256] * _sigmoid(zv)
            elif zc >= COL_GATE and zc < COL_TAIL:
                sg_ref[slot, :, zc - COL_GATE:zc - COL_GATE + width] = _silu(zv)
            else:
                z_ref[HALO:HALO + ts, zc:zc + width] = zv
            yield

    def gate_silu(tile, rows, c0, c1):
        return sg_ref[tile, rows, c0:c1]

    def attention(z_ref, tile):
        mix_t = mix_ref.at[tile]
        for blk in range(ts // WINDOW):
            rows = slice(HALO + blk * WINDOW, HALO + (blk + 1) * WINDOW)
            q = z_ref[rows, COL_AQ:COL_AQ + 256]
            k = z_ref[rows, COL_AK:COL_AK + 128]
            v = z_ref[rows, COL_AV:COL_AV + 128]
            cos = cos_ref[tile * ts + blk * WINDOW:tile * ts + (blk + 1) * WINDOW, :]
            sin = sin_ref[tile * ts + blk * WINDOW:tile * ts + (blk + 1) * WINDOW, :]
            qn = q * lax.rsqrt(head_mean(q * q) + EPS) * q_gain
            kn = k * lax.rsqrt(_split_lhs_dot(k * k, g_mean[0:128, 0:128], 1) + EPS) * k_gain
            qa = rope(qn[:, 0:128], cos, sin)
            qb = rope(qn[:, 128:256], cos, sin)
            kr = rope(kn, cos, sin)
            cur = slice(WINDOW + blk * WINDOW, 2 * WINDOW + blk * WINDOW)
            both = slice(blk * WINDOW, (blk + 2) * WINDOW)
            kbuf[0, cur, :] = kr.astype(BF16)
            kbuf[1, cur, :] = pltpu.roll(kr, HEAD_DIM, 1).astype(BF16)
            vbuf[0, cur, :] = v.astype(BF16)
            vbuf[1, cur, :] = pltpu.roll(v, HEAD_DIM, 1).astype(BF16)
            lhs = [jnp.where(lo128, qa, 0.0).astype(BF16), jnp.where(lo128, 0.0, qa).astype(BF16),
                   jnp.where(lo128, qb, 0.0).astype(BF16), jnp.where(lo128, 0.0, qb).astype(BF16)]
            yield
            valid = band
            if blk == 0 and tile == 0:
                valid = valid & (kj >= jnp.where(sp > 0, 0, WINDOW))
            slots = [0 if r in (0, 3) else 1 for r in range(N_HEADS)]
            scs = [_dot_nt(lhs[r], kbuf[slots[r], both, :]) for r in range(N_HEADS)]
            yield
            es, dens = [], []
            for r in range(N_HEADS):
                sc = jnp.where(valid, scs[r], -jnp.inf)
                sink = sink_ref[layer, r]
                m = jnp.maximum(jnp.max(sc, axis=-1, keepdims=True), sink)
                e = jnp.exp(sc - m)
                dens.append(jnp.sum(e, axis=-1, keepdims=True) + jnp.exp(sink - m))
                es.append(e.astype(BF16))
                if r % 2 == 1:
                    yield
            outs = [_dot(es[r], vbuf[slots[r], both, :]) / dens[r] for r in range(N_HEADS)]
            yield
            o_a0 = jnp.where(lo128, outs[0], outs[1])
            o_a1 = jnp.where(lo128, outs[2], outs[3])
            mrows = slice(blk * WINDOW, (blk + 1) * WINDOW)
            mix_t[mrows, 0:128] = (o_a0 * gate_silu(tile, mrows, 0, 128)).astype(BF16)
            mix_t[mrows, 128:256] = (o_a1 * gate_silu(tile, mrows, 128, 256)).astype(BF16)
            yield
        kbuf[:, 0:WINDOW, :] = kbuf[:, ts:ts + WINDOW, :]
        vbuf[:, 0:WINDOW, :] = vbuf[:, ts:ts + WINDOW, :]

    def deltanet(z_ref, tile):
        mix_t = mix_ref.at[tile]
        cs = range(ts // DN_CHUNK)
        sls = [slice(c * DN_CHUNK, (c + 1) * DN_CHUNK) for c in cs]
        q_c, k_c, v_c = [], [], []
        for c in cs:
            r0 = HALO + c * DN_CHUNK - (B_CONV - 1)
            acc = None
            for kk in range(B_CONV):
                term = z_ref[r0 + kk:r0 + kk + DN_CHUNK, COL_B:COL_B + 768] * wb[kk:kk + 1, :]
                acc = term if acc is None else acc + term
            qkv = _silu(acc)
            q, k = qkv[:, 0:256], qkv[:, 256:512]
            ss = head_sum(jnp.concatenate([q * q, k * k], axis=0))
            q_c.append(q * lax.rsqrt(ss[0:DN_CHUNK] + EPS) * (HEAD_DIM ** -0.5))
            k_c.append(k * lax.rsqrt(ss[DN_CHUNK:] + EPS))
            v_c.append(qkv[:, 512:768])
            yield

        tail = z_ref[HALO:HALO + ts, COL_TAIL:COL_TAIL + 128]
        beta = _sigmoid(tail)
        xg = tail + vec_ref[0:1, 640:768]
        softplus = jnp.maximum(xg, 0.0) + jnp.log1p(jnp.exp(-jnp.abs(xg)))
        gdec = -jnp.exp(vec_ref[0:1, 512:640]) * softplus
        beta_f = _split_lhs_dot(beta, e_beta, 1)
        gdec_f = _split_lhs_dot(gdec, e_alpha, 2)
        gc = _split_rhs_dot(l_bd, gdec_f, 2)
        yield

        gc_c = [gc[s] for s in sls]
        gl = [g[DN_CHUNK - 1:DN_CHUNK, :] for g in gc_c]
        egc = [jnp.exp(g) for g in gc_c]
        ekg = [jnp.exp(gl[c] - gc_c[c]) for c in cs]
        egl = [jnp.exp(g) for g in gl]
        kb = [k_c[c] * beta_f[sls[c]] for c in cs]
        vb = [v_c[c] * beta_f[sls[c]] for c in cs]
        grow = [jnp.sum(jnp.where(eye, g, 0.0), axis=0, keepdims=True) for g in gc_c]
        decay = [jnp.exp(jnp.minimum(gc_c[c] - grow[c], 0.0)) for c in cs]
        yield
        aq = [_dot_nt(jnp.concatenate([kb[c], q_c[c]], axis=0).astype(BF16), bd(k_c[c])) for c in cs]
        a_m = [jnp.where(strict, aq[c][0:DN_CHUNK] * decay[c], 0.0) for c in cs]
        qk = [jnp.where(tril, aq[c][DN_CHUNK:2 * DN_CHUNK] * decay[c], 0.0) for c in cs]
        yield

        d1 = [jnp.where(same16, a, 0.0) for a in a_m]
        pm = [eye_f - d for d in d1]
        d2 = [mm(d, d) for d in d1]
        yield
        pm[0] = tie(pm[0])
        r = [mm(jnp.concatenate([pm[c], d2[c]], axis=0), d2[c]) for c in cs]
        pm = [pm[c] + r[c][0:DN_CHUNK] for c in cs]
        d4 = [r[c][DN_CHUNK:] for c in cs]
        yield
        pm[0] = tie(pm[0])
        r = [mm(jnp.concatenate([pm[c], d4[c]], axis=0), d4[c]) for c in cs]
        pm = [pm[c] + r[c][0:DN_CHUNK] for c in cs]
        d8 = [r[c][DN_CHUNK:] for c in cs]
        yield
        pm[0] = tie(pm[0])
        t0 = [pm[c] + mm(pm[c], d8[c]) for c in cs]
        yield
        m1 = [jnp.where(same32 & jnp.logical_not(same16), a, 0.0) for a in a_m]
        t0[0] = tie(t0[0])
        r = [mm(t0[c], m1[c]) for c in cs]
        yield
        t1 = [t0[c] - mm(r[c], t0[c]) for c in cs]
        yield
        m2 = [jnp.where(same32, 0.0, a) for a in a_m]
        t1[0] = tie(t1[0])
        r = [mm(t1[c], m2[c]) for c in cs]
        yield
        t2 = [(t1[c] - mm(r[c], t1[c])).astype(BF16) for c in cs]
        yield
        u_c = [_dot(t2[c], bd(vb[c])) for c in cs]
        w_c = [_dot(t2[c], bd(kb[c] * egc[c])) for c in cs]
        yield

        eye_b = eye_f.astype(BF16)
        bdw = [bd(w_c[c]) for c in cs]
        bdu = [bd(u_c[c]) for c in cs]
        kgt = [_dot_nt(eye_b, bd(k_c[c] * ekg[c])).astype(BF16) for c in cs]
        yield
        ntc = [_dot(kgt[c], bdw[c]) for c in cs]
        btc = [_dot(kgt[c], bdu[c]) for c in cs]
        qkb = [q.astype(BF16) for q in qk]
        qt = [q_c[c] * egc[c] - _dot(qkb[c], bdw[c]) for c in cs]
        ou = [_dot(qkb[c], bdu[c]) for c in cs]
        yield

        st = st_ref[...]
        for c in cs:
            r = _dot(jnp.concatenate([ntc[c], qt[c]], axis=0).astype(BF16), bd(st))
            o_c = r[DN_CHUNK:] + ou[c]
            st = st * egl[c] - r[0:DN_CHUNK] + btc[c]
            if c == cs[-1]:
                st_ref[...] = st
            yield
            o_n = o_c * lax.rsqrt(head_mean(o_c * o_c) + EPS) * out_gain
            zrows = slice(HALO + c * DN_CHUNK, HALO + (c + 1) * DN_CHUNK)
            mix_t[sls[c], 256:512] = (o_n * gate_silu(tile, sls[c], 256, 512)).astype(BF16)

    def convs(z_ref, tile):
        mix_t, ybuf_t, ysh_t, yc_t = mix_ref.at[tile], ybuf.at[tile], ysh.at[tile], yc_ref.at[tile]
        for c in range(ts // C_CHUNK):
            sl = slice(c * C_CHUNK, (c + 1) * C_CHUNK)
            zrows = slice(HALO + c * C_CHUNK, HALO + (c + 1) * C_CHUNK)
            cu = z_ref[zrows, COL_C:COL_C + 256]
            cv = z_ref[zrows, COL_C + 256:COL_C + 512]
            mu = jnp.mean(cv, axis=-1, keepdims=True)
            dv = cv - mu
            var = jnp.mean(dv * dv, axis=-1, keepdims=True)
            cvn = dv * lax.rsqrt(var + EPS) * vec_ref[1:2, 256:512] + vec_ref[1:2, 512:768]
            yield
            vblk = jnp.concatenate([jnp.where(head_mask[g], cvn, 0.0).astype(BF16) for g in range(4)], axis=0)
            zc = _dot(cw, vblk) + cbias
            mix_t[sl, 512:768] = (cu * zc * gate_silu(tile, sl, 512, 768)).astype(BF16)
            yield

        for r in range(1, 8):
            ysh_t[r - 1] = ybuf_t[r:r + ts + HALO - 8, :]
            if r % 2 == 1:
                yield
        for rb in range(0, ts, CONV_ROWS):
            acc = None
            for kk in range(D_CONV):
                r0 = HALO - (D_CONV - 1) + kk
                a8, r = (r0 // 8) * 8 + rb, r0 % 8
                win = ybuf_t[a8:a8 + CONV_ROWS, :] if r == 0 else ysh_t[r - 1, a8:a8 + CONV_ROWS, :]
                term = win * wd[kk:kk + 1, :]
                acc = term if acc is None else acc + term
            yc_t[rb:rb + CONV_ROWS, :] = acc + vec_ref[1:2, 768:1024]
            yield
        for c in range(ts // C_CHUNK):
            sl = slice(c * C_CHUNK, (c + 1) * C_CHUNK)
            zrows = slice(HALO + c * C_CHUNK, HALO + (c + 1) * C_CHUNK)
            yc = yc_t[sl, :]
            mu = head_mean(yc)
            dy = yc - mu
            var = head_mean(dy * dy)
            yn = dy * lax.rsqrt(var + EPS) * vec_ref[2:3, 0:256] + vec_ref[2:3, 256:512]
            mix_t[sl, 768:1024] = (_silu(yn) * gate_silu(tile, sl, 768, 1024)).astype(BF16)
            yield

    def out_proj(tile):
        mix_t, x1_t, hg_t = mix_ref.at[tile], x1_ref.at[tile], hg_ref.at[tile]
        rows = slice(tile * ts, (tile + 1) * ts)
        for c0 in range(0, D_MODEL, PROJ_CHUNK):
            cols = slice(c0, c0 + PROJ_CHUNK)
            x1_t[:, cols] = x_ref[0, rows, cols] + _dot(mix_t[...], w_out_ref[:, cols])
            yield
        for r0 in range(0, ts, 64):
            x1 = x1_t[r0:r0 + 64, :]
            ms1 = jnp.mean(x1 * x1, axis=-1, keepdims=True)
            hg_t[r0:r0 + 64, :] = (x1 * lax.rsqrt(ms1 + EPS)).astype(BF16)
        yield
        pb = p_ref[0, rows, :].astype(BF16)
        for c0 in range(0, D_MODEL, PROJ_CHUNK):
            cols = slice(c0, c0 + PROJ_CHUNK)
            ple_gate = _sigmoid(_dot(hg_t[...], gate_w_ref[:, cols]))
            pe = _dot(pb, ple_w_ref[:, cols])
            o_ref[0, rows, cols] = x1_t[:, cols] + ple_gate * pe
            yield

    @pl.when(sp == 0)
    def _():
        za_ref[0:HALO, :] = jnp.zeros((HALO, Z_WIDTH), F32)
        kbuf[:, 0:WINDOW, :] = jnp.zeros((2, WINDOW, 128), BF16)
        vbuf[:, 0:WINDOW, :] = jnp.zeros((2, WINDOW, 128), BF16)
        st_ref[...] = jnp.zeros((HEAD_DIM, 256), F32)
        ybuf[0, 0:HALO, :] = jnp.zeros((HALO, GROUP_WIDTH), F32)

    @pl.when(step == 0)
    def _():
        _run_streams([(in_proj(lambda r0, n: x_ref[0, r0:r0 + n, :], za_ref, 0), 1)])

    tokens.clear()
    zb_ref[0:HALO, :] = za_ref[ts:ts + HALO, :]
    ybuf[1, 0:HALO, :] = ybuf[0, ts:ts + HALO, :]
    _run_streams([(deltanet(za_ref, 0), 1), (convs(za_ref, 0), 1), (attention(za_ref, 0), 2),
                  (in_proj(lambda r0, n: x_ref[0, ts + r0:ts + r0 + n, :], zb_ref, 1), 2)])
    za_ref[0:HALO, :] = zb_ref[ts:ts + HALO, :]
    ybuf[0, 0:HALO, :] = ybuf[1, ts:ts + HALO, :]
    _run_streams([(deltanet(zb_ref, 1), 1), (convs(zb_ref, 1), 1), (attention(zb_ref, 1), 2),
                  (in_proj(lambda r0, n: xn_ref[0, r0:r0 + n, :], za_ref, 0), 1), (out_proj(0), 1)])
    _run_streams([(out_proj(1), 1)])


def _layer_spec(arr, layer):
    return pl.BlockSpec((None,) + arr.shape[1:], lambda g: (layer,) + (0,) * (arr.ndim - 1),
                        pipeline_mode=pl.Buffered(1))


def _layer_call(x, p, layer, cos_t, sin_t, sink, w_in, w_mid, w_tail, w_out, ple_w, gate_w, vec, bconv, cw, cbias,
                dconv, ts):
    bsz, seq, _ = x.shape
    tiles_per_seq = seq // ts
    steps_per_seq = tiles_per_seq // 2
    n_steps = bsz * steps_per_seq
    last_tile = bsz * tiles_per_seq - 1

    def pair_map(g):
        return (g // steps_per_seq, g % steps_per_seq, 0)

    def next_tile_map(g):
        t = jnp.minimum(2 * g + 2, last_tile)
        return (t // tiles_per_seq, t % tiles_per_seq, 0)

    in_specs = [
        pl.BlockSpec(memory_space=pltpu.SMEM),
        pl.BlockSpec((1, 2 * ts, D_MODEL), pair_map),
        pl.BlockSpec((1, ts, D_MODEL), next_tile_map),
        pl.BlockSpec((None, 1, 2 * ts, D_PLE), lambda g: (layer,) + pair_map(g)),
        pl.BlockSpec((2 * ts, 128), lambda g: (g % steps_per_seq, 0)),
        pl.BlockSpec((2 * ts, 128), lambda g: (g % steps_per_seq, 0)),
    ] + [_layer_spec(a, layer) for a in (w_in, w_mid, w_tail, w_out, ple_w, gate_w, vec, bconv, cw, cbias, dconv)]
    scratch = [
        pltpu.VMEM((HALO + ts, Z_WIDTH), F32),
        pltpu.VMEM((HALO + ts, Z_WIDTH), F32),
        pltpu.VMEM((ts, D_MODEL), BF16),
        pltpu.VMEM((2, WINDOW + ts, 128), BF16),
        pltpu.VMEM((2, WINDOW + ts, 128), BF16),
        pltpu.VMEM((HEAD_DIM, 256), F32),
        pltpu.VMEM((2, ts, D_MODEL), BF16),
        pltpu.VMEM((2, HALO + ts, GROUP_WIDTH), F32),
        pltpu.VMEM((2, 7, HALO + ts - 8, GROUP_WIDTH), F32),
        pltpu.VMEM((2, ts, GROUP_WIDTH), F32),
        pltpu.VMEM((2, ts, D_MODEL), F32),
        pltpu.VMEM((2, ts, D_MODEL), BF16),
        pltpu.VMEM((2, ts, D_MODEL), F32),
    ]
    return pl.pallas_call(
        functools.partial(_layer_kernel, ts=ts, steps_per_seq=steps_per_seq, layer=layer),
        grid=(n_steps,),
        in_specs=in_specs,
        out_specs=pl.BlockSpec((1, 2 * ts, D_MODEL), pair_map),
        out_shape=jax.ShapeDtypeStruct(x.shape, F32),
        scratch_shapes=scratch,
        compiler_params=pltpu.CompilerParams(
            dimension_semantics=("arbitrary",),
            vmem_limit_bytes=VMEM_LIMIT_BYTES,
        ),
        name="hybrid_trunk_layer",
    )(sink, x, x, p, cos_t, sin_t, w_in, w_mid, w_tail, w_out, ple_w, gate_w, vec, bconv, cw, cbias, dconv)


W_PREP_BLOCK = 256


def _w_in_prep_kernel(wt_ref, g_ref, head_ref, mid_ref, tail_ref):
    n = W_PREP_BLOCK
    eye = (lax.broadcasted_iota(jnp.int32, (n, n), 0) == lax.broadcasted_iota(jnp.int32, (n, n), 1)).astype(BF16)
    g = g_ref[...]
    d_in = g.shape[1]

    def transposed(r0, rows):
        wb = (wt_ref[r0:r0 + rows, :] * g).astype(BF16)
        return jnp.concatenate([_dot_nt(eye, wb[:, k:k + n]) for k in range(0, d_in, n)], axis=0).astype(BF16)

    for c in range(0, W_IN_SPLIT, n):
        head_ref[:, c:c + n] = transposed(c, n)
    mid0 = W_IN_SPLIT + 2 * N_HEADS
    for c in range(0, mid_ref.shape[1], n):
        mid_ref[:, c:c + n] = transposed(mid0 + c, n)
    lane = lax.broadcasted_iota(jnp.int32, (1, 128), 1)
    tail = transposed(W_IN_SPLIT, 128)
    tail_ref[...] = jnp.where(lane < 2 * N_HEADS, tail, jnp.zeros_like(tail))


def _w_in_prep(w_in_t, gain):
    depth, width, d_in = w_in_t.shape
    mid = width - W_IN_SPLIT - 2 * N_HEADS

    def spec(rows, cols):
        return pl.BlockSpec((None, rows, cols), lambda l: (l, 0, 0))

    return pl.pallas_call(
        _w_in_prep_kernel,
        grid=(depth,),
        in_specs=[spec(width, d_in), spec(1, d_in)],
        out_specs=[spec(d_in, W_IN_SPLIT), spec(d_in, mid), spec(d_in, 128)],
        out_shape=[jax.ShapeDtypeStruct((depth, d_in, c), BF16) for c in (W_IN_SPLIT, mid, 128)],
        compiler_params=pltpu.CompilerParams(vmem_limit_bytes=VMEM_LIMIT_BYTES),
        name="w_in_prep",
    )(w_in_t, gain[:, None, :])


def kernel(x, p, norm_gain, w_in, w_out, a_q_gain, a_k_gain, a_sink, b_conv, b_a_log, b_dt_bias, b_out_gain,
           c_ln_gain, c_ln_bias, c_w_s, c_b_s, d_conv, d_conv_bias, d_ln_gain, d_ln_bias,
           ple_w, ple_gate_norm, ple_gate_w):
    depth = w_in.shape[0]
    bsz, seq, _ = x.shape
    ts = min(SEQ_TILE, seq)
    assert seq % (2 * ts) == 0 and ts % WINDOW == 0

    inv = jnp.tile(ROPE_THETA ** (-jnp.arange(0, HEAD_DIM, 2, dtype=F32) / HEAD_DIM), 4)
    sign = jnp.tile(jnp.repeat(jnp.array([-1.0, 1.0], F32), HEAD_DIM // 2), 2)
    ang_b = (jnp.arange(seq // WINDOW, dtype=F32) * WINDOW)[:, None] * inv[None, :]
    ang_r = jnp.arange(WINDOW, dtype=F32)[:, None] * inv[None, :]
    cb, sb, cr, sr = jnp.cos(ang_b)[:, None, :], jnp.sin(ang_b)[:, None, :], jnp.cos(ang_r)[None], jnp.sin(ang_r)[None]
    cos_t = (cb * cr - sb * sr).reshape(seq, 128)
    sin_t = ((sb * cr + cb * sr) * sign).reshape(seq, 128)

    w_head, w_mid, w_tail = _w_in_prep(jnp.swapaxes(w_in, 1, 2), norm_gain)
    gate_w_b = (ple_gate_w * ple_gate_norm[:, :, None]).astype(BF16)
    w_out_b, ple_w_b = w_out.astype(BF16), ple_w.astype(BF16)

    def z(n):
        return jnp.zeros((depth, n), F32)

    def rep4(v, n):
        return jnp.broadcast_to(v[:, None, :], (depth, n, v.shape[1])).reshape(depth, n * v.shape[1])

    vec = jnp.concatenate([
        rep4(a_q_gain * (HEAD_DIM ** -0.5), 4), rep4(a_k_gain, 2), z(128),
        z(N_HEADS), b_a_log, z(128 - 2 * N_HEADS), z(N_HEADS), b_dt_bias, z(128 - 2 * N_HEADS), z(256),
        rep4(b_out_gain, 4), c_ln_gain, c_ln_bias, d_conv_bias,
        d_ln_gain, d_ln_bias, z(512), z(5 * D_MODEL)], axis=1).astype(F32).reshape(depth, 8, D_MODEL)
    cw = jnp.transpose(c_w_s, (0, 2, 1, 3)).reshape(depth, C_CHUNK, 4 * C_CHUNK)
    cbias = jnp.repeat(jnp.swapaxes(c_b_s, 1, 2), GROUP_WIDTH // 4, axis=2)
    sink = a_sink.astype(F32)

    for i in range(depth):
        x = _layer_call(x, p, i, cos_t, sin_t, sink, w_head, w_mid, w_tail, w_out_b, ple_w_b, gate_w_b, vec,
                        b_conv, cw, cbias, d_conv, ts)
    return x
```
